```python
import jax, jax.numpy as jnp
from jax import lax
import numpy as np

D_MODEL = 1024
BATCH = 2
SEQ = 8192
DEPTH = 1
DEC_BATCH = 128
DEC_SEQ = 4
PAST_LEN = 8192
PAGE_SIZE = 128

HEAD_DIM = 64
N_HEADS_A = 6
N_HEADS_B = 6
N_HEADS_M = 4
WIDTH_A = N_HEADS_A * HEAD_DIM
WIDTH_B = N_HEADS_B * HEAD_DIM
WIDTH_M = N_HEADS_M * HEAD_DIM
MIX_WIDTH = WIDTH_A + WIDTH_B + WIDTH_M
IN_WIDTH = 2 * WIDTH_A + 3 * WIDTH_B + WIDTH_M
SPLITS = (WIDTH_A, 2 * WIDTH_A, 2 * WIDTH_A + WIDTH_B, 2 * WIDTH_A + 2 * WIDTH_B, 2 * WIDTH_A + 3 * WIDTH_B)
CHUNK = 128
DILATIONS = ((128, 1), (512, 4), (2048, 16))
MAX_WINDOW = 2048
BLOCK = 128
N_MEM = 256
ROPE_THETA = 500000.0
ROT_DIM = HEAD_DIM // 4
SCALE = HEAD_DIM ** -0.5
N_EXPERTS = 32
TOP_K = 4
D_EXPERT = D_MODEL
SWIGLU_LIMIT = 7.0
SWIGLU_ALPHA = 1.702
EPS = 1e-6

kernel_name = 'hymba_gmlp_longnet_memxattn_moe_step'


def rmsnorm(x, g):
    xf = x.astype(jnp.float32)
    y = xf * lax.rsqrt(jnp.mean(xf * xf, axis=-1, keepdims=True) + EPS)
    return (y * g.astype(jnp.float32)).astype(x.dtype)


def partial_rope(x, pos):
    half = ROT_DIM // 2
    inv_freq = jnp.power(ROPE_THETA, -jnp.arange(half, dtype=jnp.float32) / half)
    ang = pos.astype(jnp.float32)[:, None] * inv_freq[None, :]
    cos = jnp.cos(ang)[:, None, :]
    sin = jnp.sin(ang)[:, None, :]
    xr = x[..., :ROT_DIM].astype(jnp.float32)
    x1, x2 = xr[..., :half], xr[..., half:]
    rot = jnp.concatenate([x1 * cos - x2 * sin, x2 * cos + x1 * sin], axis=-1)
    return jnp.concatenate([rot.astype(x.dtype), x[..., ROT_DIM:]], axis=-1)


def pre_mix(x, pos, norm1_g, w_in, gv_a, gq_b, gk_b, gq_m):
    B, S, _ = x.shape
    z = rmsnorm(x, norm1_g) @ w_in
    u_a, v_a, q_b, k_b, v_b, q_m = jnp.split(z, SPLITS, axis=-1)
    v_a = rmsnorm(v_a, gv_a)
    q_b = partial_rope(rmsnorm(q_b.reshape(B, S, N_HEADS_B, HEAD_DIM), gq_b), pos)
    k_b = partial_rope(rmsnorm(k_b.reshape(B, S, N_HEADS_B, HEAD_DIM), gk_b), pos)
    v_b = v_b.reshape(B, S, N_HEADS_B, HEAD_DIM)
    q_m = rmsnorm(q_m.reshape(B, S, N_HEADS_M, HEAD_DIM), gq_m)
    return u_a, v_a, q_b, k_b, v_b, q_m


def spatial_gate(u, v_n, w_s, b_s):
    B, S, _ = u.shape
    T = min(CHUNK, S)
    vc = v_n.reshape(B, S // T, T, N_HEADS_A, HEAD_DIM)
    w = jnp.where(jnp.tril(jnp.ones((T, T), dtype=bool)), w_s[:, :T, :T], 0)
    mixed = jnp.einsum('gts,bnsgc->bntgc', w, vc) + jnp.transpose(b_s[:, :T])[None, None, :, :, None]
    return u * mixed.reshape(B, S, WIDTH_A)


def dilated_branch_prompt(q, k, v, window, dil):
    B, S, H, Dh = q.shape
    n_sub = window // dil
    L = -(-S // dil)
    nb = -(-L // BLOCK)

    def to_blocks(t):
        t = jnp.pad(t, ((0, 0), (0, L * dil - S), (0, 0), (0, 0)))
        t = t.reshape(B, L, dil, H, Dh).transpose(0, 2, 1, 3, 4)
        t = jnp.pad(t, ((0, 0), (0, 0), (0, nb * BLOCK - L), (0, 0), (0, 0)))
        return t.reshape(B, dil, nb, BLOCK, H, Dh)

    def with_prev(t):
        prev = jnp.pad(t, ((0, 0), (0, 0), (1, 0), (0, 0), (0, 0), (0, 0)))[:, :, :-1]
        return jnp.concatenate([prev, t], axis=3)

    def from_blocks(t):
        rest = t.shape[4:]
        t = t.reshape((B, dil, nb * BLOCK) + rest)[:, :, :L]
        t = jnp.swapaxes(t, 1, 2).reshape((B, L * dil) + rest)
        return t[:, :S]

    qb = to_blocks(q)
    kk = with_prev(to_blocks(k))
    vv = with_prev(to_blocks(v))
    qi = jnp.arange(BLOCK)[:, None]
    kj = jnp.arange(2 * BLOCK)[None, :]
    dist = BLOCK + qi - kj
    band = (dist >= 0) & (dist <= n_sub)
    mask = band[None] & ((jnp.arange(nb)[:, None, None] > 0) | (kj[None] >= BLOCK))
    s = jnp.einsum('brnqhd,brnkhd->brnhqk', qb, kk).astype(jnp.float32) * SCALE
    s = jnp.where(mask[None, None, :, None], s, -jnp.inf)
    m = jnp.max(s, axis=-1, keepdims=True)
    p = jnp.exp(s - m)
    l = jnp.sum(p, axis=-1, keepdims=True)
    o = jnp.einsum('brnhqk,brnkhd->brnqhd', p, vv.astype(jnp.float32)) / jnp.swapaxes(l, 3, 4)
    lse = jnp.swapaxes((m + jnp.log(l))[..., 0], 3, 4)
    return from_blocks(o), from_blocks(lse)


def combine_by_denominator(outs, lses):
    w = jax.nn.softmax(jnp.stack(lses), axis=0)
    return jnp.sum(w[..., None] * jnp.stack(outs), axis=0)


def dilated_attention_prompt(q, k, v):
    outs, lses = [], []
    for window, dil in DILATIONS:
        o, lse = dilated_branch_prompt(q, k, v, window, dil)
        outs.append(o)
        lses.append(lse)
    return combine_by_denominator(outs, lses)


def dilated_attention_sample(q, k_all, v_all, n_past):
    T = q.shape[1]
    outs, lses = [], []
    for window, dil in DILATIONS:
        offs = dil * jnp.arange(window // dil + 1)
        idx = n_past + jnp.arange(T)[:, None] - offs[None, :]
        valid = idx >= 0
        idx = jnp.maximum(idx, 0)
        kg = k_all[:, idx]
        vg = v_all[:, idx]
        s = jnp.einsum('bthd,btkhd->bthk', q, kg).astype(jnp.float32) * SCALE
        s = jnp.where(valid[None, :, None, :], s, -jnp.inf)
        m = jnp.max(s, axis=-1, keepdims=True)
        p = jnp.exp(s - m)
        l = jnp.sum(p, axis=-1, keepdims=True)
        outs.append(jnp.einsum('bthk,btkhd->bthd', p, vg.astype(jnp.float32)) / l)
        lses.append((m + jnp.log(l))[..., 0])
    return combine_by_denominator(outs, lses)


def memory_kv(mem, mem_norm_g, w_mem_kv, gk_m):
    B, M, _ = mem.shape
    kv = rmsnorm(mem, mem_norm_g) @ w_mem_kv
    k = rmsnorm(kv[..., :WIDTH_M].reshape(B, M, N_HEADS_M, HEAD_DIM), gk_m)
    v = kv[..., WIDTH_M:].reshape(B, M, N_HEADS_M, HEAD_DIM)
    return k, v


def memory_attend(q_m, k_m, v_m):
    s = jnp.einsum('bshd,bmhd->bhsm', q_m, k_m).astype(jnp.float32) * SCALE
    p = jax.nn.softmax(s, axis=-1)
    return jnp.einsum('bhsm,bmhd->bshd', p, v_m.astype(jnp.float32))


def moe_ffn(h, w_router, b_router, w_gate_up, b_gate_up, w_down, b_down):
    shp = h.shape
    t = h.reshape(-1, D_MODEL)
    logits = (t @ w_router + b_router).astype(jnp.float32)
    top_val, top_idx = lax.top_k(logits, TOP_K)
    gates = jax.nn.softmax(top_val, axis=-1)
    comb = jnp.einsum('nk,nke->ne', gates, jax.nn.one_hot(top_idx, N_EXPERTS, dtype=jnp.float32))
    y = jnp.zeros(t.shape, jnp.float32)
    for e in range(N_EXPERTS):
        gu = t @ w_gate_up[e] + b_gate_up[e]
        gate = jnp.minimum(gu[:, :D_EXPERT], SWIGLU_LIMIT)
        up = jnp.clip(gu[:, D_EXPERT:], -SWIGLU_LIMIT, SWIGLU_LIMIT)
        act = (up + 1) * (gate * jax.nn.sigmoid(SWIGLU_ALPHA * gate))
        y = y + comb[:, e:e + 1] * (act @ w_down[e] + b_down[e]).astype(jnp.float32)
    return y.reshape(shp).astype(h.dtype)


def post_mix(x, o_a, o_b, o_m, w_out, norm2_g, w_router, b_router, w_gate_up, b_gate_up, w_down, b_down):
    B, S, _ = x.shape
    mixed = jnp.concatenate([o_a.astype(x.dtype),
                             o_b.reshape(B, S, WIDTH_B).astype(x.dtype),
                             o_m.reshape(B, S, WIDTH_M).astype(x.dtype)], axis=-1)
    x = x + mixed @ w_out
    return x + moe_ffn(rmsnorm(x, norm2_g), w_router, b_router, w_gate_up, b_gate_up, w_down, b_down)


def setup_inputs(seed: int = 0) -> dict:
    key = jax.random.key(seed)
    ks = jax.random.split(key, 32)
    f32 = jnp.float32
    w_buf = min(MAX_WINDOW, PAST_LEN)
    L = DEPTH

    def nrm(k, shape, scale):
        return scale * jax.random.normal(k, shape, f32)

    def gain(k, shape):
        return 1.0 + 0.05 * jax.random.normal(k, shape, f32)

    return {
        'x_prompt': nrm(ks[0], (BATCH, SEQ, D_MODEL), 1.0),
        'x_sample': nrm(ks[1], (DEC_BATCH, DEC_SEQ, D_MODEL), 1.0),
        'mem_prompt': nrm(ks[2], (BATCH, N_MEM, D_MODEL), 1.0),
        'cache_win_k': nrm(ks[3], (L, DEC_BATCH, w_buf, N_HEADS_B, HEAD_DIM), 1.0),
        'cache_win_v': nrm(ks[4], (L, DEC_BATCH, w_buf, N_HEADS_B, HEAD_DIM), 1.0),
        'cache_mem_k': nrm(ks[5], (L, DEC_BATCH, N_MEM, N_HEADS_M, HEAD_DIM), 1.0),
        'cache_mem_v': nrm(ks[6], (L, DEC_BATCH, N_MEM, N_HEADS_M, HEAD_DIM), 1.0),
        'norm1_g': gain(ks[7], (L, D_MODEL)),
        'w_in': nrm(ks[8], (L, D_MODEL, IN_WIDTH), D_MODEL ** -0.5),
        'gv_a': gain(ks[9], (L, WIDTH_A)),
        'w_s': nrm(ks[10], (L, N_HEADS_A, CHUNK, CHUNK), CHUNK ** -0.5),
        'b_s': gain(ks[11], (L, N_HEADS_A, CHUNK)),
        'gq_b': gain(ks[12], (L, HEAD_DIM)),
        'gk_b': gain(ks[13], (L, HEAD_DIM)),
        'gq_m': gain(ks[14], (L, HEAD_DIM)),
        'gk_m': gain(ks[15], (L, HEAD_DIM)),
        'mem_norm_g': gain(ks[16], (L, D_MODEL)),
        'w_mem_kv': nrm(ks[17], (L, D_MODEL, 2 * WIDTH_M), D_MODEL ** -0.5),
        'w_out': nrm(ks[18], (L, MIX_WIDTH, D_MODEL), MIX_WIDTH ** -0.5),
        'norm2_g': gain(ks[19], (L, D_MODEL)),
        'w_router': nrm(ks[20], (L, D_MODEL, N_EXPERTS), D_MODEL ** -0.5),
        'b_router': nrm(ks[21], (L, N_EXPERTS), 0.01),
        'w_gate_up': nrm(ks[22], (L, N_EXPERTS, D_MODEL, 2 * D_EXPERT), D_MODEL ** -0.5),
        'b_gate_up': nrm(ks[23], (L, N_EXPERTS, 2 * D_EXPERT), 0.01),
        'w_down': nrm(ks[24], (L, N_EXPERTS, D_EXPERT, D_MODEL), D_EXPERT ** -0.5),
        'b_down': nrm(ks[25], (L, N_EXPERTS, D_MODEL), 0.01),
    }


def reference(x_prompt, x_sample, mem_prompt, cache_win_k, cache_win_v, cache_mem_k, cache_mem_v,
              norm1_g, w_in, gv_a, w_s, b_s, gq_b, gk_b, gq_m, gk_m, mem_norm_g, w_mem_kv, w_out,
              norm2_g, w_router, b_router, w_gate_up, b_gate_up, w_down, b_down):
    S = x_prompt.shape[1]
    T = x_sample.shape[1]
    pos_p = jnp.arange(S)
    pos_s = PAST_LEN + jnp.arange(T)
    n_keep = min(MAX_WINDOW, S)
    n_past = cache_win_k.shape[2]
    xp, xs = x_prompt, x_sample
    win_k_p, win_v_p, mem_k_p, mem_v_p, win_k_s, win_v_s, gate_v_s = [], [], [], [], [], [], []
    for l in range(DEPTH):
        ffn = (w_router[l], b_router[l], w_gate_up[l], b_gate_up[l], w_down[l], b_down[l])
        u_a, v_a, q_b, k_b, v_b, q_m = pre_mix(xp, pos_p, norm1_g[l], w_in[l], gv_a[l], gq_b[l], gk_b[l], gq_m[l])
        o_a = spatial_gate(u_a, v_a, w_s[l], b_s[l])
        o_b = dilated_attention_prompt(q_b, k_b, v_b)
        k_m, v_m = memory_kv(mem_prompt, mem_norm_g[l], w_mem_kv[l], gk_m[l])
        o_m = memory_attend(q_m, k_m, v_m)
        xp = post_mix(xp, o_a, o_b, o_m, w_out[l], norm2_g[l], *ffn)
        win_k_p.append(k_b[:, S - n_keep:])
        win_v_p.append(v_b[:, S - n_keep:])
        mem_k_p.append(k_m)
        mem_v_p.append(v_m)
        u_a, v_a, q_b, k_b, v_b, q_m = pre_mix(xs, pos_s, norm1_g[l], w_in[l], gv_a[l], gq_b[l], gk_b[l], gq_m[l])
        o_a = spatial_gate(u_a, v_a, w_s[l], b_s[l])
        k_all = jnp.concatenate([cache_win_k[l].astype(k_b.dtype), k_b], axis=1)
        v_all = jnp.concatenate([cache_win_v[l].astype(v_b.dtype), v_b], axis=1)
        o_b = dilated_attention_sample(q_b, k_all, v_all, n_past)
        o_m = memory_attend(q_m, cache_mem_k[l], cache_mem_v[l])
        xs = post_mix(xs, o_a, o_b, o_m, w_out[l], norm2_g[l], *ffn)
        win_k_s.append(k_b)
        win_v_s.append(v_b)
        gate_v_s.append(v_a)
    y_prompt = xp
    y_sample = xs
    new_win_k_prompt = jnp.stack(win_k_p)
    new_win_v_prompt = jnp.stack(win_v_p)
    new_mem_k_prompt = jnp.stack(mem_k_p)
    new_mem_v_prompt = jnp.stack(mem_v_p)
    new_win_k_sample = jnp.stack(win_k_s)
    new_win_v_sample = jnp.stack(win_v_s)
    new_gate_v_sample = jnp.stack(gate_v_s)
    return (y_prompt, y_sample, new_win_k_prompt, new_win_v_prompt, new_mem_k_prompt, new_mem_v_prompt,
            new_win_k_sample, new_win_v_sample, new_gate_v_sample)
```

```python
import functools

import numpy as np
import jax
import jax.numpy as jnp
from jax import lax
from jax.experimental import pallas as pl
from jax.experimental.pallas import tpu as pltpu

F32 = jnp.float32
BF16 = jnp.bfloat16

D_MODEL = 1024
HEAD_DIM = 64
WIDTH_A = 384
WIDTH_B = 384
WIDTH_M = 256
IN_WIDTH = 2 * WIDTH_A + 3 * WIDTH_B + WIDTH_M
CHUNK = 128
DILATIONS = ((128, 1), (512, 4), (2048, 16))
N_SUB = 128
MAX_WINDOW = 2048
N_MEM = 256
ROPE_THETA = 500000.0
ROT_HALF = 8
SCALE = HEAD_DIM ** -0.5
N_EXPERTS = 32
TOP_K = 4
SWIGLU_LIMIT = 7.0
SWIGLU_ALPHA = 1.702
EPS = 1e-6
PAST_LEN = 8192

LANE = 128
NEG = -1e30
TM = 512
SPAN = 2048
QB = 128
MOE_TM = 256
PIECE = 16
SLOTS = TOP_K * MOE_TM // PIECE + 2 * N_EXPERTS
VMEM_LIMIT = 52 * 1024 * 1024


def _cparams(sem):
    return pltpu.CompilerParams(dimension_semantics=sem, vmem_limit_bytes=VMEM_LIMIT)


def _premix_body(x_ref, g1_ref, win_ref, gva_ref, gq_ref, gk_ref, gqm_ref, bd_ref,
                 rc_ref, rs1_ref, rs2_ref,
                 u_ref, va_ref, q3_ref, k3_ref, v3_ref, qm_ref):
    x = x_ref[...]
    tm = x.shape[0]
    ms = jnp.mean(x * x, axis=-1, keepdims=True)
    h = (x * lax.rsqrt(ms + EPS) * g1_ref[...]).astype(BF16)
    z = jnp.dot(h, win_ref[...], preferred_element_type=F32)
    u_ref[...] = z[:, :WIDTH_A]
    va = z[:, WIDTH_A:2 * WIDTH_A]
    va_ms = jnp.mean(va * va, axis=-1, keepdims=True)
    va_ref[...] = va * lax.rsqrt(va_ms + EPS) * gva_ref[...]
    q0, k0, v0, m0 = 2 * WIDTH_A, 2 * WIDTH_A + WIDTH_B, 2 * WIDTH_A + 2 * WIDTH_B, 2 * WIDTH_A + 3 * WIDTH_B
    tiles = ([z[:, q0 + LANE * j:q0 + LANE * (j + 1)] for j in range(3)]
             + [z[:, k0 + LANE * j:k0 + LANE * (j + 1)] for j in range(3)]
             + [z[:, m0 + LANE * j:m0 + LANE * (j + 1)] for j in range(2)])
    sq = jnp.concatenate([(t * t).astype(BF16) for t in tiles], axis=0)
    ssum = jnp.dot(sq, bd_ref[...], preferred_element_type=F32)
    inv = [lax.rsqrt(ssum[i * tm:(i + 1) * tm] * (1.0 / HEAD_DIM) + EPS) for i in range(8)]
    rc, rs1, rs2 = rc_ref[...], rs1_ref[...], rs2_ref[...]

    def rope(t):
        return t * rc + pltpu.roll(t, LANE - ROT_HALF, 1) * rs1 + pltpu.roll(t, ROT_HALF, 1) * rs2

    for j in range(3):
        q3_ref[j] = rope(tiles[j] * inv[j] * gq_ref[...]) * SCALE
        k3_ref[j] = rope(tiles[3 + j] * inv[3 + j] * gk_ref[...])
        v3_ref[j] = z[:, v0 + LANE * j:v0 + LANE * (j + 1)]
    for j in range(2):
        qm_ref[:, LANE * j:LANE * (j + 1)] = (tiles[6 + j] * inv[6 + j] * gqm_ref[...] * SCALE).astype(BF16)


def _premix(x, tabs, n_tab_tiles, w):
    t = x.shape[0]
    nt = t // TM
    full = lambda shape: pl.BlockSpec(shape, lambda i: (0,) * len(shape))
    tab = pl.BlockSpec((TM, LANE), lambda i: (i % n_tab_tiles, 0))
    row = lambda width: pl.BlockSpec((TM, width), lambda i: (i, 0))
    pair = pl.BlockSpec((3, TM, LANE), lambda i: (0, i, 0))
    return pl.pallas_call(
        _premix_body,
        grid=(nt,),
        in_specs=[row(D_MODEL), full((1, D_MODEL)), full((D_MODEL, IN_WIDTH)), full((1, WIDTH_A)),
                  full((1, LANE)), full((1, LANE)), full((1, LANE)), full((LANE, LANE)), tab, tab, tab],
        out_specs=[row(WIDTH_A), row(WIDTH_A), pair, pair, pair, row(WIDTH_M)],
        out_shape=[jax.ShapeDtypeStruct((t, WIDTH_A), F32), jax.ShapeDtypeStruct((t, WIDTH_A), F32),
                   jax.ShapeDtypeStruct((3, t, LANE), F32), jax.ShapeDtypeStruct((3, t, LANE), F32),
                   jax.ShapeDtypeStruct((3, t, LANE), F32), jax.ShapeDtypeStruct((t, WIDTH_M), BF16)],
        compiler_params=_cparams(("arbitrary",)),
        name="premix",
    )(x, w["g1"], w["w_in"], w["gva"], w["gq"], w["gk"], w["gqm"], w["bd"], *tabs)


def _rope_tables(pos):
    inv_freq = jnp.power(ROPE_THETA, -jnp.arange(ROT_HALF, dtype=F32) / ROT_HALF)
    ang = pos.astype(F32)[:, None] * inv_freq[None, :]
    cos, sin = jnp.cos(ang), jnp.sin(ang)
    t = pos.shape[0]
    rest = HEAD_DIM - 2 * ROT_HALF
    c = jnp.concatenate([cos, cos, jnp.ones((t, rest), F32)], axis=1)
    s1 = jnp.concatenate([-sin, jnp.zeros((t, HEAD_DIM - ROT_HALF), F32)], axis=1)
    s2 = jnp.concatenate([jnp.zeros((t, ROT_HALF), F32), sin, jnp.zeros((t, rest), F32)], axis=1)
    two = lambda a: jnp.concatenate([a, a], axis=1)
    return two(c), two(s1), two(s2)


def _attn_body(q_ref, kc_ref, kp_ref, vc_ref, vp_ref, o_ref, m_s, l_s, a_s):
    span_idx = pl.program_id(1)
    p0 = span_idx * SPAN
    lane = lax.broadcasted_iota(jnp.int32, (QB, LANE), 1)
    low = lane < HEAD_DIM
    qi = lax.broadcasted_iota(jnp.int32, (QB, 2 * QB), 0)
    kj = lax.broadcasted_iota(jnp.int32, (QB, 2 * QB), 1)
    band = (kj >= qi) & (kj <= qi + N_SUB)

    def rows(ref, start, n, d):
        if d == 1:
            return ref[0, pl.ds(pl.multiple_of(start, QB), n), :]
        return ref[0, pl.ds(start, n, stride=d), :]

    def unit(d, qstart, a_ref, a_start, first):
        kpos0 = p0 + qstart - QB * d
        mask = band & (kpos0 + d * kj >= 0)
        qb = rows(q_ref, qstart, QB, d)
        kb = jnp.concatenate([rows(a_ref[0], a_start, QB, d), rows(kc_ref, qstart, QB, d)], axis=0).astype(BF16)
        vb = jnp.concatenate([rows(a_ref[1], a_start, QB, d), rows(vc_ref, qstart, QB, d)], axis=0).astype(BF16)
        stats = []
        for hm in (low, jnp.logical_not(low)):
            qh = jnp.where(hm, qb, 0.0).astype(BF16)
            s = lax.dot_general(qh, kb, (((1,), (1,)), ((), ())), preferred_element_type=F32)
            s = jnp.where(mask, s, NEG)
            m = jnp.max(s, axis=-1, keepdims=True)
            e = jnp.exp(s - m)
            l = jnp.sum(e, axis=-1, keepdims=True)
            acc = jnp.dot(e.astype(BF16), vb, preferred_element_type=F32)
            stats.append((m, l, acc))
        m_new = jnp.where(low, stats[0][0], stats[1][0])
        l_new = jnp.where(low, stats[0][1], stats[1][1])
        a_new = jnp.where(low, stats[0][2], stats[1][2])
        if d == 1:
            sl = (pl.ds(pl.multiple_of(qstart, QB), QB), slice(None))
        else:
            sl = (pl.ds(qstart, QB, stride=d), slice(None))
        if first:
            m_s[sl] = m_new
            l_s[sl] = l_new
            a_s[sl] = a_new
        else:
            m_old, l_old, a_old = m_s[sl], l_s[sl], a_s[sl]
            m_t = jnp.maximum(m_old, m_new)
            wa = jnp.exp(m_old - m_t)
            wb = jnp.exp(m_new - m_t)
            m_s[sl] = m_t
            l_s[sl] = wa * l_old + wb * l_new
            a_s[sl] = wa * a_old + wb * a_new

    prev = (kp_ref, vp_ref)
    cur = (kc_ref, vc_ref)
    nblk = SPAN // QB
    first = True
    for _, d in DILATIONS:
        per_res = nblk // d
        def head_unit(r, c, d=d, first=first):
            unit(d, r, prev, SPAN - QB * d + r, first)
            return c
        lax.fori_loop(0, d, head_unit, 0)
        if per_res > 1:
            def tail_unit(i, c, d=d, per_res=per_res, first=first):
                r = i // (per_res - 1)
                blk = i % (per_res - 1) + 1
                qstart = r + d * QB * blk
                unit(d, qstart, cur, qstart - QB * d, first)
                return c
            lax.fori_loop(0, d * (per_res - 1), tail_unit, 0)
        first = False
    o_ref[...] = (a_s[...] / l_s[...]).astype(o_ref.dtype)


def _attn_prompt(q3, k3, v3, batch, seq):
    nspan = seq // SPAN
    cur = pl.BlockSpec((1, SPAN, LANE), lambda b, s, p: (p, b * nspan + s, 0))
    prv = pl.BlockSpec((1, SPAN, LANE), lambda b, s, p: (p, b * nspan + jnp.maximum(s - 1, 0), 0))
    return pl.pallas_call(
        _attn_body,
        grid=(batch, nspan, 3),
        in_specs=[cur, cur, prv, cur, prv],
        out_specs=pl.BlockSpec((SPAN, LANE), lambda b, s, p: (b * nspan + s, p)),
        out_shape=jax.ShapeDtypeStruct((batch * seq, WIDTH_B), BF16),
        scratch_shapes=[pltpu.VMEM((SPAN, LANE), F32)] * 3,
        compiler_params=_cparams(("arbitrary", "arbitrary", "arbitrary")),
        name="attn_prompt",
    )(q3, k3, k3, v3, v3)


def _memkv_body(mem_ref, g_ref, w_ref, gk_ref, bd_ref, k_ref, v_ref):
    x = mem_ref[...]
    ms = jnp.mean(x * x, axis=-1, keepdims=True)
    h = (x * lax.rsqrt(ms + EPS) * g_ref[...]).astype(BF16)
    kv = jnp.dot(h, w_ref[...], preferred_element_type=F32)
    n = x.shape[0]
    kt = [kv[:, LANE * j:LANE * (j + 1)] for j in range(2)]
    sq = jnp.concatenate([(t * t).astype(BF16) for t in kt], axis=0)
    ssum = jnp.dot(sq, bd_ref[...], preferred_element_type=F32)
    for j in range(2):
        inv = lax.rsqrt(ssum[j * n:(j + 1) * n] * (1.0 / HEAD_DIM) + EPS)
        k_ref[:, LANE * j:LANE * (j + 1)] = kt[j] * inv * gk_ref[...]
    v_ref[...] = kv[:, WIDTH_M:]


def _memkv(mem, w):
    n = mem.shape[0]
    return pl.pallas_call(
        _memkv_body,
        out_shape=[jax.ShapeDtypeStruct((n, WIDTH_M), F32)] * 2,
        compiler_params=pltpu.CompilerParams(vmem_limit_bytes=VMEM_LIMIT),
        name="memkv",
    )(mem, w["gmem"], w["w_mem_kv"], w["gkm"], w["bd"])


def _memattn_body(q_ref, k_ref, v_ref, o_ref):
    lane = lax.broadcasted_iota(jnp.int32, (q_ref.shape[0], LANE), 1)
    low = lane < HEAD_DIM
    for j in range(2):
        qp = q_ref[:, LANE * j:LANE * (j + 1)].astype(F32)
        kp = k_ref[0, :, LANE * j:LANE * (j + 1)].astype(BF16)
        vp = v_ref[0, :, LANE * j:LANE * (j + 1)].astype(BF16)
        outs = []
        for hm in (low, jnp.logical_not(low)):
            qh = jnp.where(hm, qp, 0.0).astype(BF16)
            s = lax.dot_general(qh, kp, (((1,), (1,)), ((), ())), preferred_element_type=F32)
            m = jnp.max(s, axis=-1, keepdims=True)
            e = jnp.exp(s - m)
            l = jnp.sum(e, axis=-1, keepdims=True)
            outs.append(jnp.dot(e.astype(BF16), vp, preferred_element_type=F32) / l)
        o_ref[:, LANE * j:LANE * (j + 1)] = jnp.where(low, outs[0], outs[1]).astype(o_ref.dtype)


def _memattn_prompt(qm, km, vm, batch, seq):
    tiles_per_b = seq // TM
    kv = pl.BlockSpec((1, N_MEM, WIDTH_M), lambda i: (i // tiles_per_b, 0, 0))
    return pl.pallas_call(
        _memattn_body,
        grid=(batch * tiles_per_b,),
        in_specs=[pl.BlockSpec((TM, WIDTH_M), lambda i: (i, 0)), kv, kv],
        out_specs=pl.BlockSpec((TM, WIDTH_M), lambda i: (i, 0)),
        out_shape=jax.ShapeDtypeStruct((batch * seq, WIDTH_M), BF16),
        compiler_params=_cparams(("arbitrary",)),
        name="memattn_prompt",
    )(qm, km, vm)


def _sample_attn_body(qbd_ref, kt_ref, vt_ref, kn_ref, vn_ref, cnt_ref, cntn_ref, hmask_ref,
                      qmbd_ref, kmt_ref, vmt_ref, hmaskm_ref, ob_ref, om_ref):
    dec = kn_ref.shape[1]
    qbd = qbd_ref[0]
    kt = kt_ref[0].astype(BF16)
    vt = vt_ref[0].astype(BF16)
    s = jnp.dot(qbd, kt, preferred_element_type=F32)
    qf = qbd.astype(F32)
    kn = kn_ref[0]
    vn = vn_ref[0]
    cnt = cnt_ref[...]
    cntn = cntn_ref[...]
    s_new = [jnp.sum(qf * kn[j:j + 1, :], axis=-1, keepdims=True) for j in range(dec)]
    m = jnp.max(jnp.where(cnt > 0, s, NEG), axis=-1, keepdims=True)
    for j in range(dec):
        m = jnp.maximum(m, jnp.where(cntn[:, j:j + 1] > 0, s_new[j], NEG))
    e = cnt * jnp.exp(jnp.where(cnt > 0, s - m, 0.0))
    l = jnp.sum(e, axis=-1, keepdims=True)
    acc = lax.dot_general(e.astype(BF16), vt, (((1,), (1,)), ((), ())), preferred_element_type=F32)
    for j in range(dec):
        w = cntn[:, j:j + 1]
        ej = w * jnp.exp(jnp.where(w > 0, s_new[j] - m, 0.0))
        l = l + ej
        acc = acc + ej * vn[j:j + 1, :]
    r = acc / l * hmask_ref[...]
    out = r[0:8]
    for h in range(1, WIDTH_B // HEAD_DIM):
        out = out + r[8 * h:8 * h + 8]
    ob_ref[0] = out
    qm = qmbd_ref[0]
    sm = jnp.dot(qm, kmt_ref[0].astype(BF16), preferred_element_type=F32)
    mm = jnp.max(sm, axis=-1, keepdims=True)
    em = jnp.exp(sm - mm)
    lm = jnp.sum(em, axis=-1, keepdims=True)
    am = lax.dot_general(em.astype(BF16), vmt_ref[0].astype(BF16), (((1,), (1,)), ((), ())),
                         preferred_element_type=F32)
    rm = am / lm * hmaskm_ref[...]
    outm = rm[0:8]
    for h in range(1, WIDTH_M // HEAD_DIM):
        outm = outm + rm[8 * h:8 * h + 8]
    om_ref[0] = outm


def _sample_counts(dec, w_buf):
    t = np.arange(8)[:, None]
    t = np.where(t < dec, t, 0)
    def mult(dist):
        c = np.zeros(dist.shape, np.float32)
        for window, dil in DILATIONS:
            c += ((dist >= 0) & (dist % dil == 0) & (dist <= window)).astype(np.float32)
        return c
    cache = mult(w_buf + t - np.arange(w_buf)[None, :])
    new = mult(t - np.arange(dec)[None, :])
    nb, nm = WIDTH_B // HEAD_DIM, WIDTH_M // HEAD_DIM
    hmask = (np.arange(8 * nb)[:, None] // 8 == np.arange(WIDTH_B)[None, :] // HEAD_DIM).astype(np.float32)
    hmaskm = (np.arange(8 * nm)[:, None] // 8 == np.arange(WIDTH_M)[None, :] // HEAD_DIM).astype(np.float32)
    return np.tile(cache, (nb, 1)), np.tile(new, (nb, 1)), hmask, hmaskm


def _block_diag_queries(q, dec, hmask):
    width = q.shape[-1]
    nh = width // HEAD_DIM
    qb = q.reshape(-1, 1, dec, width)
    qb = jnp.pad(qb, ((0, 0), (0, 0), (0, 8 - dec), (0, 0)))
    qb = jnp.broadcast_to(qb, (qb.shape[0], nh, 8, width)).reshape(-1, 8 * nh, width)
    return (qb * hmask[None]).astype(BF16)


def _sample_attn(q, kn, vn, qm, kt, vt, kmt, vmt, dec):
    bd = kt.shape[0]
    w_buf = kt.shape[-1]
    cnt, cntn, hmask, hmaskm = _sample_counts(dec, w_buf)
    qbd = _block_diag_queries(q, dec, hmask)
    qmbd = _block_diag_queries(qm.astype(F32), dec, hmaskm)
    nb8, nm8 = qbd.shape[1], qmbd.shape[1]
    per_b = lambda shape: pl.BlockSpec((1,) + shape, lambda b: (b,) + (0,) * len(shape))
    full = lambda shape: pl.BlockSpec(shape, lambda b: (0,) * len(shape))
    ob, om = pl.pallas_call(
        _sample_attn_body,
        grid=(bd,),
        in_specs=[per_b((nb8, WIDTH_B)), per_b((WIDTH_B, w_buf)), per_b((WIDTH_B, w_buf)),
                  per_b((dec, WIDTH_B)), per_b((dec, WIDTH_B)),
                  full((nb8, w_buf)), full((nb8, dec)), full((nb8, WIDTH_B)),
                  per_b((nm8, WIDTH_M)), per_b((WIDTH_M, N_MEM)), per_b((WIDTH_M, N_MEM)), full((nm8, WIDTH_M))],
        out_specs=[per_b((8, WIDTH_B)), per_b((8, WIDTH_M))],
        out_shape=[jax.ShapeDtypeStruct((bd, 8, WIDTH_B), F32), jax.ShapeDtypeStruct((bd, 8, WIDTH_M), F32)],
        compiler_params=_cparams(("arbitrary",)),
        name="sample_attn",
    )(qbd, kt, vt, kn.reshape(bd, dec, WIDTH_B), vn.reshape(bd, dec, WIDTH_B),
      jnp.asarray(cnt), jnp.asarray(cntn), jnp.asarray(hmask),
      qmbd, kmt, vmt, jnp.asarray(hmaskm))
    return (ob[:, :dec].reshape(bd * dec, WIDTH_B).astype(BF16),
            om[:, :dec].reshape(bd * dec, WIDTH_M).astype(BF16))


def _post_body(x_ref, u_ref, va_ref, ob_ref, om_ref, wg_ref, bg_ref, wout_ref, g2_ref, wr_ref, br_ref,
               x1_ref, h2s_ref, te_ref, tg_ref, *, tc):
    tm = x_ref.shape[0]
    lane = lax.broadcasted_iota(jnp.int32, (tc, LANE), 1)
    low = lane < HEAD_DIM
    u = u_ref[...]
    va = va_ref[...].astype(BF16)
    oa_rows = []
    for c in range(tm // tc):
        r0 = c * tc
        tiles = []
        for p in range(3):
            vp = va[r0:r0 + tc, LANE * p:LANE * (p + 1)]
            r = jnp.dot(wg_ref[p], vp, preferred_element_type=F32)
            tiles.append(jnp.where(low, r[:tc], r[tc:]))
        mixed = jnp.concatenate(tiles, axis=1) + bg_ref[...]
        oa_rows.append(u[r0:r0 + tc] * mixed)
    oa = jnp.concatenate(oa_rows, axis=0) if len(oa_rows) > 1 else oa_rows[0]
    mixed_all = jnp.concatenate([oa.astype(BF16), ob_ref[...], om_ref[...]], axis=1)
    x1 = x_ref[...] + jnp.dot(mixed_all, wout_ref[...], preferred_element_type=F32)
    x1_ref[...] = x1
    ms = jnp.mean(x1 * x1, axis=-1, keepdims=True)
    h2 = x1 * lax.rsqrt(ms + EPS) * g2_ref[...]
    hb = h2.astype(BF16).astype(F32)
    bits = lax.bitcast_convert_type(hb, jnp.uint32)
    half = D_MODEL // 2
    words = lax.bitcast_convert_type(bits[:, half:] | (bits[:, :half] >> 16), jnp.int32)
    for j in range(4):
        h2s_ref[pl.ds(j, tm, stride=4), :] = words[:, LANE * j:LANE * (j + 1)]
    logits = jnp.dot(h2, wr_ref[...], preferred_element_type=F32, precision=lax.Precision.HIGHEST) + br_ref[...]
    lane_i = lax.broadcasted_iota(jnp.int32, (tm, LANE), 1)
    lane_r = lane_i.astype(F32)
    vals = logits
    tops, idxs = [], []
    for _ in range(TOP_K):
        mk = jnp.max(vals, axis=-1, keepdims=True)
        ik = jnp.min(jnp.where(vals == mk, lane_r, float(LANE)), axis=-1, keepdims=True)
        vals = jnp.where(lane_r == ik, -jnp.inf, vals)
        tops.append(mk)
        idxs.append(ik)
    es = [jnp.exp(t - tops[0]) for t in tops]
    den = es[0] + es[1] + es[2] + es[3]
    te = jnp.zeros((tm, LANE), jnp.int32)
    tg = jnp.zeros((tm, LANE), F32)
    for k in range(TOP_K):
        te = jnp.where(lane_i == k, idxs[k].astype(jnp.int32), te)
        tg = jnp.where(lane_i == k, es[k] / den, tg)
    te_ref[...] = te
    tg_ref[...] = tg


def _post(x, u, va, ob, om, wg, bg, w, tc):
    t = x.shape[0]
    nt = t // TM
    full = lambda shape: pl.BlockSpec(shape, lambda i: (0,) * len(shape))
    row = lambda width: pl.BlockSpec((TM, width), lambda i: (i, 0))
    return pl.pallas_call(
        functools.partial(_post_body, tc=tc),
        grid=(nt,),
        in_specs=[row(D_MODEL), row(WIDTH_A), row(WIDTH_A), row(WIDTH_B), row(WIDTH_M),
                  full((3, 2 * tc, tc)), full((tc, WIDTH_A)), full((D_MODEL, D_MODEL)), full((1, D_MODEL)),
                  full((D_MODEL, LANE)), full((1, LANE))],
        out_specs=[row(D_MODEL), pl.BlockSpec((4 * TM, LANE), lambda i: (i, 0)), row(LANE), row(LANE)],
        out_shape=[jax.ShapeDtypeStruct((t, D_MODEL), F32), jax.ShapeDtypeStruct((4 * t, LANE), jnp.int32),
                   jax.ShapeDtypeStruct((t, LANE), jnp.int32), jax.ShapeDtypeStruct((t, LANE), F32)],
        compiler_params=_cparams(("arbitrary",)),
        name="post",
    )(x, u, va, ob, om, wg, bg, w["w_out"], w["g2"], w["w_router"], w["b_router"])


def _dispatch_body(tok_ref, valid_ref, src_ref, o_ref):
    t = pl.program_id(0)

    @pl.when(valid_ref[t] > 0)
    def _():
        def step(g, c):
            for uu in range(8):
                mi = g * 8 + uu
                n = tok_ref[t * MOE_TM + mi]
                o_ref[pl.ds(pl.multiple_of(mi * 4, 4), 4), :] = src_ref[pl.ds(pl.multiple_of(n * 4, 4), 4), :]
            return c
        lax.fori_loop(0, MOE_TM // 8, step, 0)

    @pl.when(valid_ref[t] == 0)
    def _():
        o_ref[...] = jnp.zeros(o_ref.shape, o_ref.dtype)


def _dispatch(row_token, tile_valid, h2s, n_tiles):
    return pl.pallas_call(
        _dispatch_body,
        grid_spec=pltpu.PrefetchScalarGridSpec(
            num_scalar_prefetch=2, grid=(n_tiles,),
            in_specs=[pl.BlockSpec(memory_space=pltpu.VMEM)],
            out_specs=pl.BlockSpec((4 * MOE_TM, LANE), lambda t, *_: (t, 0))),
        out_shape=jax.ShapeDtypeStruct((4 * MOE_TM * n_tiles, LANE), jnp.int32),
        compiler_params=_cparams(("arbitrary",)),
        name="moe_dispatch",
    )(row_token, tile_valid, h2s)


def _experts_body(te_ref, valid_ref, first_ref, xs_ref, wgu_ref, bgu_ref, wd_ref, bd_ref, ys_ref, wgu_s, wd_s):
    t = pl.program_id(0)

    @pl.when(first_ref[t] > 0)
    def _():
        wgu_s[...] = wgu_ref[0].astype(BF16)
        wd_s[...] = wd_ref[0].astype(BF16)

    @pl.when(valid_ref[t] > 0)
    def _():
        lo, hi = [], []
        for j in range(4):
            wj = lax.bitcast_convert_type(xs_ref[pl.ds(j, MOE_TM, stride=4), :], jnp.uint32)
            lo.append(lax.bitcast_convert_type(wj << 16, F32))
            hi.append(lax.bitcast_convert_type(wj & jnp.uint32(0xFFFF0000), F32))
        x = jnp.concatenate(lo + hi, axis=1).astype(BF16)
        gu = jnp.dot(x, wgu_s[...], preferred_element_type=F32) + bgu_ref[0]
        gate = jnp.minimum(gu[:, :D_MODEL], SWIGLU_LIMIT)
        up = jnp.clip(gu[:, D_MODEL:], -SWIGLU_LIMIT, SWIGLU_LIMIT)
        act = (up + 1.0) * (gate * (1.0 / (1.0 + jnp.exp(-SWIGLU_ALPHA * gate))))
        y = jnp.dot(act.astype(BF16), wd_s[...], preferred_element_type=F32) + bd_ref[0]
        ys_ref[...] = y.astype(ys_ref.dtype)

    @pl.when(valid_ref[t] == 0)
    def _():
        ys_ref[...] = jnp.zeros(ys_ref.shape, ys_ref.dtype)


def _experts(tile_expert, tile_valid, tile_first, xs, w, n_tiles):
    return pl.pallas_call(
        _experts_body,
        grid_spec=pltpu.PrefetchScalarGridSpec(
            num_scalar_prefetch=3, grid=(n_tiles,),
            in_specs=[pl.BlockSpec((4 * MOE_TM, LANE), lambda t, *_: (t, 0)),
                      pl.BlockSpec((1, D_MODEL, 2 * D_MODEL), lambda t, te, *_: (te[t], 0, 0)),
                      pl.BlockSpec((1, 1, 2 * D_MODEL), lambda t, te, *_: (te[t], 0, 0)),
                      pl.BlockSpec((1, D_MODEL, D_MODEL), lambda t, te, *_: (te[t], 0, 0)),
                      pl.BlockSpec((1, 1, D_MODEL), lambda t, te, *_: (te[t], 0, 0))],
            out_specs=pl.BlockSpec((MOE_TM, D_MODEL), lambda t, *_: (t, 0)),
            scratch_shapes=[pltpu.VMEM((D_MODEL, 2 * D_MODEL), BF16), pltpu.VMEM((D_MODEL, D_MODEL), BF16)]),
        out_shape=jax.ShapeDtypeStruct((MOE_TM * n_tiles, D_MODEL), BF16),
        compiler_params=_cparams(("arbitrary",)),
        name="moe_experts",
    )(tile_expert, tile_valid, tile_first, xs, w["w_gate_up"], w["b_gate_up"], w["w_down"], w["b_down"])


def _combine_body(piece_ref, cnt_ref, lidx_ref, g_ref, x1_ref, ys_hbm, y_ref, buf, sem):
    t = pl.program_id(0)
    nt = pl.num_programs(0)

    def piece_copy(tile, slot, i):
        src = piece_ref[tile * SLOTS + i]
        return pltpu.make_async_copy(ys_hbm.at[pl.ds(pl.multiple_of(src * PIECE, PIECE), PIECE), :],
                                     buf.at[slot, pl.ds(pl.multiple_of(i * PIECE, PIECE), PIECE), :],
                                     sem.at[slot])

    def start_tile(tile, slot):
        lax.fori_loop(0, cnt_ref[tile], lambda i, c: (piece_copy(tile, slot, i).start(), c)[1], 0)

    def wait_tile(tile, slot):
        lax.fori_loop(0, cnt_ref[tile], lambda i, c: (piece_copy(tile, slot, i).wait(), c)[1], 0)

    @pl.when(t == 0)
    def _():
        buf[...] = jnp.zeros(buf.shape, buf.dtype)
        start_tile(0, 0)

    @pl.when(t + 1 < nt)
    def _():
        start_tile(t + 1, (t + 1) % 2)

    slot = t % 2
    wait_tile(t, slot)
    col = lax.broadcasted_iota(jnp.int32, (MOE_TM, SLOTS * PIECE), 1)
    lidx = lidx_ref[...]
    g = g_ref[...]
    p = jnp.zeros((MOE_TM, SLOTS * PIECE), F32)
    for k in range(TOP_K):
        p = jnp.where(col == lidx[:, k:k + 1], g[:, k:k + 1], p)
    moe = jnp.dot(p.astype(BF16), buf[slot], preferred_element_type=F32)
    y_ref[...] = x1_ref[...] + moe


def _combine(pieces, piece_cnt, lidx, gates, x1, ys):
    n = x1.shape[0]
    nt = n // MOE_TM
    return pl.pallas_call(
        _combine_body,
        grid_spec=pltpu.PrefetchScalarGridSpec(
            num_scalar_prefetch=2, grid=(nt,),
            in_specs=[pl.BlockSpec((MOE_TM, LANE), lambda t, *_: (t, 0)),
                      pl.BlockSpec((MOE_TM, LANE), lambda t, *_: (t, 0)),
                      pl.BlockSpec((MOE_TM, D_MODEL), lambda t, *_: (t, 0)),
                      pl.BlockSpec(memory_space=pl.ANY)],
            out_specs=pl.BlockSpec((MOE_TM, D_MODEL), lambda t, *_: (t, 0)),
            scratch_shapes=[pltpu.VMEM((2, SLOTS * PIECE, D_MODEL), BF16), pltpu.SemaphoreType.DMA((2,))]),
        out_shape=jax.ShapeDtypeStruct((n, D_MODEL), F32),
        compiler_params=_cparams(("arbitrary",)),
        name="moe_combine",
    )(pieces, piece_cnt, lidx, gates, x1, ys)


def _moe(x1, h2s, te, tg, w):
    n = x1.shape[0]
    n_tiles = (n * TOP_K) // MOE_TM + N_EXPERTS
    rows = n_tiles * MOE_TM
    i32 = jnp.int32
    top_e = te[:, :TOP_K]
    sel = (top_e[:, :, None] == jnp.arange(N_EXPERTS, dtype=i32)[None, None, :]).astype(i32).sum(axis=1)
    cum = jnp.cumsum(sel, axis=0)
    rank = cum - sel
    counts = cum[-1]
    tiles_e = (counts + MOE_TM - 1) // MOE_TM
    tile_end = jnp.cumsum(tiles_e)
    tile_start = tile_end - tiles_e
    off = tile_start * MOE_TM
    dest = jnp.take_along_axis(off[None, :] + rank, top_e, axis=1)
    tok = jnp.broadcast_to(jnp.arange(n, dtype=i32)[:, None], (n, TOP_K))
    row_token = jnp.zeros((rows,), i32).at[dest.reshape(-1)].set(tok.reshape(-1))
    tix = jnp.arange(n_tiles, dtype=i32)
    total_tiles = tile_end[-1]
    tile_valid = (tix < total_tiles).astype(i32)
    tile_expert = jnp.minimum((tix[:, None] >= tile_end[None, :]).astype(i32).sum(axis=1), N_EXPERTS - 1)
    last_expert = jnp.minimum((jnp.maximum(total_tiles - 1, 0) >= tile_end).astype(i32).sum(), N_EXPERTS - 1)
    tile_expert = jnp.where(tile_valid > 0, tile_expert, last_expert)
    tile_first = ((tix == tile_start[tile_expert]) & (tile_valid > 0)).astype(i32)
    nt = n // MOE_TM
    start_rank = jnp.concatenate([rank[::MOE_TM], counts[None, :]], axis=0)
    a = off[None, :] + start_rank[:-1]
    b = off[None, :] + start_rank[1:]
    fp = a // PIECE
    npc = jnp.where(b > a, (b - 1) // PIECE - fp + 1, 0)
    sb_end = jnp.cumsum(npc, axis=1)
    sb = sb_end - npc
    piece_cnt = sb_end[:, -1]
    s_ix = jnp.arange(SLOTS, dtype=i32)
    e_of_s = jnp.minimum((s_ix[None, None, :] >= sb_end[:, :, None]).astype(i32).sum(axis=1), N_EXPERTS - 1)
    pieces = jnp.take_along_axis(fp - sb, e_of_s, axis=1) + s_ix[None, :]
    tile_of_tok = jnp.arange(n, dtype=i32) // MOE_TM
    base = (sb - fp)[tile_of_tok]
    lidx = (jnp.take_along_axis(base, top_e, axis=1) + dest // PIECE) * PIECE + dest % PIECE
    lidx = jnp.pad(lidx, ((0, 0), (0, LANE - TOP_K)), constant_values=-1)

    xs = _dispatch(row_token, tile_valid, h2s, n_tiles)
    ys = _experts(tile_expert, tile_valid, tile_first, xs, w, n_tiles)
    return _combine(pieces.reshape(-1).astype(i32), piece_cnt.astype(i32), lidx.astype(i32), tg, x1, ys)


def _pair_major_to_rows(a3):
    return jnp.transpose(a3, (1, 0, 2)).reshape(a3.shape[1], 3 * LANE)


def kernel(x_prompt, x_sample, mem_prompt, cache_win_k, cache_win_v, cache_mem_k, cache_mem_v, norm1_g, w_in, gv_a, w_s, b_s, gq_b, gk_b, gq_m, gk_m, mem_norm_g, w_mem_kv, w_out, norm2_g, w_router, b_router, w_gate_up, b_gate_up, w_down, b_down):
    batch, seq, _ = x_prompt.shape
    bd, dec, _ = x_sample.shape
    depth = norm1_g.shape[0]
    assert depth == 1 and seq % SPAN == 0 and (bd * dec) % TM == 0 and PAST_LEN % CHUNK == 0
    w_buf = cache_win_k.shape[2]
    assert w_buf == MAX_WINDOW and dec <= 8
    l = 0
    two = lambda g: jnp.concatenate([g, g])[None, :]
    head = np.arange(LANE) // HEAD_DIM
    w = dict(
        g1=norm1_g[l][None], w_in=w_in[l].astype(BF16), gva=gv_a[l][None],
        gq=two(gq_b[l]), gk=two(gk_b[l]), gqm=two(gq_m[l]), gkm=two(gk_m[l]),
        bd=jnp.asarray(head[:, None] == head[None, :], BF16),
        gmem=mem_norm_g[l][None], w_mem_kv=w_mem_kv[l].astype(BF16),
        w_out=w_out[l].astype(BF16), g2=norm2_g[l][None],
        w_router=jnp.pad(w_router[l], ((0, 0), (0, LANE - N_EXPERTS))),
        b_router=jnp.pad(b_router[l], (0, LANE - N_EXPERTS), constant_values=-jnp.inf)[None],
        w_gate_up=w_gate_up[l], b_gate_up=b_gate_up[l][:, None, :], w_down=w_down[l], b_down=b_down[l][:, None, :],
    )
    ngrp = WIDTH_A // HEAD_DIM
    wtri = jnp.where(jnp.tril(jnp.ones((CHUNK, CHUNK), bool)), w_s[l], 0).astype(BF16)
    wg_p = wtri.reshape(ngrp // 2, 2 * CHUNK, CHUNK)
    bg_p = jnp.repeat(jnp.transpose(b_s[l]), HEAD_DIM, axis=1)
    wsm = jnp.where(jnp.tril(jnp.ones((dec, dec), bool)), w_s[l][:, :dec, :dec], 0)
    eye = jnp.eye(bd, dtype=F32)
    wg_s = jnp.einsum('ab,gts->gatbs', eye, wsm).reshape(ngrp, bd * dec, bd * dec).astype(BF16)
    wg_s = wg_s.reshape(ngrp // 2, 2 * bd * dec, bd * dec)
    bg_s = jnp.tile(jnp.repeat(jnp.transpose(b_s[l][:, :dec]), HEAD_DIM, axis=1), (bd, 1))

    xp = x_prompt.reshape(batch * seq, D_MODEL)
    tabs_p = _rope_tables(jnp.arange(seq))
    u_p, va_p, q3_p, k3_p, v3_p, qm_p = _premix(xp, tabs_p, seq // TM, w)
    ob_p = _attn_prompt(q3_p, k3_p, v3_p, batch, seq)
    km, vm = _memkv(mem_prompt.reshape(batch * N_MEM, D_MODEL), w)
    om_p = _memattn_prompt(qm_p, km.reshape(batch, N_MEM, WIDTH_M), vm.reshape(batch, N_MEM, WIDTH_M), batch, seq)
    x1_p, h2s_p, te_p, tg_p = _post(xp, u_p, va_p, ob_p, om_p, wg_p, bg_p, w, CHUNK)

    xs = x_sample.reshape(bd * dec, D_MODEL)
    tabs_s = _rope_tables(jnp.tile(PAST_LEN + jnp.arange(dec), bd))
    u_s, va_s, q3_s, k3_s, v3_s, qm_s = _premix(xs, tabs_s, 1, w)
    q_s, k_s, v_s = (_pair_major_to_rows(a) for a in (q3_s, k3_s, v3_s))
    nb = WIDTH_B // HEAD_DIM
    nm = WIDTH_M // HEAD_DIM
    kt = jnp.transpose(cache_win_k[l], (0, 2, 3, 1)).reshape(bd, WIDTH_B, w_buf)
    vt = jnp.transpose(cache_win_v[l], (0, 2, 3, 1)).reshape(bd, WIDTH_B, w_buf)
    kmt = jnp.transpose(cache_mem_k[l], (0, 2, 3, 1)).reshape(bd, WIDTH_M, N_MEM)
    vmt = jnp.transpose(cache_mem_v[l], (0, 2, 3, 1)).reshape(bd, WIDTH_M, N_MEM)
    ob_s, om_s = _sample_attn(q_s, k_s, v_s, qm_s, kt, vt, kmt, vmt, dec)
    x1_s, h2s_s, te_s, tg_s = _post(xs, u_s, va_s, ob_s, om_s, wg_s, bg_s, w, bd * dec)

    y = _moe(jnp.concatenate([x1_p, x1_s]), jnp.concatenate([h2s_p, h2s_s]),
             jnp.concatenate([te_p, te_s]), jnp.concatenate([tg_p, tg_s]), w)
    n_p = batch * seq
    y_prompt = y[:n_p].reshape(batch, seq, D_MODEL)
    y_sample = y[n_p:].reshape(bd, dec, D_MODEL)

    n_keep = min(MAX_WINDOW, seq)
    k_p = _pair_major_to_rows(k3_p).reshape(batch, seq, nb, HEAD_DIM)
    v_p = _pair_major_to_rows(v3_p).reshape(batch, seq, nb, HEAD_DIM)
    return (y_prompt, y_sample,
            k_p[None, :, seq - n_keep:], v_p[None, :, seq - n_keep:],
            km.reshape(1, batch, N_MEM, nm, HEAD_DIM), vm.reshape(1, batch, N_MEM, nm, HEAD_DIM),
            k_s.reshape(1, bd, dec, nb, HEAD_DIM), v_s.reshape(1, bd, dec, nb, HEAD_DIM),
            va_s.reshape(1, bd, dec, WIDTH_A))
```

```python
import functools

import numpy as np
import jax
import jax.numpy as jnp
from jax import lax
from jax.experimental import pallas as pl
from jax.experimental.pallas import tpu as pltpu

F32 = jnp.float32
BF16 = jnp.bfloat16

D_MODEL = 1024
HEAD_DIM = 64
WIDTH_A = 384
WIDTH_B = 384
WIDTH_M = 256
IN_WIDTH = 2 * WIDTH_A + 3 * WIDTH_B + WIDTH_M
CHUNK = 128
DILATIONS = ((128, 1), (512, 4), (2048, 16))
N_SUB = 128
MAX_WINDOW = 2048
N_MEM = 256
ROPE_THETA = 500000.0
ROT_HALF = 8
SCALE = HEAD_DIM ** -0.5
N_EXPERTS = 32
TOP_K = 4
SWIGLU_LIMIT = 7.0
SWIGLU_ALPHA = 1.702
EPS = 1e-6
PAST_LEN = 8192

LANE = 128
NEG = -1e30
TM = 512
SPAN = 2048
QB = 128
ATTN_UNROLL = 4
MOE_TM = 256
PIECE = 8
NPIECE = MOE_TM // PIECE
SEG_ROWS = TOP_K * MOE_TM + N_EXPERTS * PIECE
SEG_PIECES = SEG_ROWS // PIECE
VMEM_LIMIT = 52 * 1024 * 1024


def _cparams(sem):
    return pltpu.CompilerParams(dimension_semantics=sem, vmem_limit_bytes=VMEM_LIMIT)


def _premix_body(x_ref, g1_ref, win_ref, gva_ref, gq_ref, gk_ref, gqm_ref, bd_ref,
                 rc_ref, rs1_ref, rs2_ref,
                 u_ref, va_ref, q3_ref, k3_ref, v3_ref, qm_ref):
    x = x_ref[...]
    tm = x.shape[0]
    ms = jnp.mean(x * x, axis=-1, keepdims=True)
    h = (x * lax.rsqrt(ms + EPS) * g1_ref[...]).astype(BF16)
    z = jnp.dot(h, win_ref[...], preferred_element_type=F32)
    u_ref[...] = z[:, :WIDTH_A]
    va = z[:, WIDTH_A:2 * WIDTH_A]
    va_ms = jnp.mean(va * va, axis=-1, keepdims=True)
    va_ref[...] = va * lax.rsqrt(va_ms + EPS) * gva_ref[...]
    q0, k0, v0, m0 = 2 * WIDTH_A, 2 * WIDTH_A + WIDTH_B, 2 * WIDTH_A + 2 * WIDTH_B, 2 * WIDTH_A + 3 * WIDTH_B
    tiles = ([z[:, q0 + LANE * j:q0 + LANE * (j + 1)] for j in range(3)]
             + [z[:, k0 + LANE * j:k0 + LANE * (j + 1)] for j in range(3)]
             + [z[:, m0 + LANE * j:m0 + LANE * (j + 1)] for j in range(2)])
    sq = jnp.concatenate([(t * t).astype(BF16) for t in tiles], axis=0)
    ssum = jnp.dot(sq, bd_ref[...], preferred_element_type=F32)
    inv = [lax.rsqrt(ssum[i * tm:(i + 1) * tm] * (1.0 / HEAD_DIM) + EPS) for i in range(8)]
    rc, rs1, rs2 = rc_ref[...], rs1_ref[...], rs2_ref[...]

    def rope(t):
        return t * rc + pltpu.roll(t, LANE - ROT_HALF, 1) * rs1 + pltpu.roll(t, ROT_HALF, 1) * rs2

    for j in range(3):
        q3_ref[j] = rope(tiles[j] * inv[j] * gq_ref[...]) * SCALE
        k3_ref[j] = rope(tiles[3 + j] * inv[3 + j] * gk_ref[...])
        v3_ref[j] = z[:, v0 + LANE * j:v0 + LANE * (j + 1)]
    for j in range(2):
        qm_ref[:, LANE * j:LANE * (j + 1)] = (tiles[6 + j] * inv[6 + j] * gqm_ref[...] * SCALE).astype(BF16)


def _premix(x, tabs, n_tab_tiles, w):
    t = x.shape[0]
    nt = t // TM
    full = lambda shape: pl.BlockSpec(shape, lambda i: (0,) * len(shape))
    tab = pl.BlockSpec((TM, LANE), lambda i: (i % n_tab_tiles, 0))
    row = lambda width: pl.BlockSpec((TM, width), lambda i: (i, 0))
    pair = pl.BlockSpec((3, TM, LANE), lambda i: (0, i, 0))
    return pl.pallas_call(
        _premix_body,
        grid=(nt,),
        in_specs=[row(D_MODEL), full((1, D_MODEL)), full((D_MODEL, IN_WIDTH)), full((1, WIDTH_A)),
                  full((1, LANE)), full((1, LANE)), full((1, LANE)), full((LANE, LANE)), tab, tab, tab],
        out_specs=[row(WIDTH_A), row(WIDTH_A), pair, pair, pair, row(WIDTH_M)],
        out_shape=[jax.ShapeDtypeStruct((t, WIDTH_A), F32), jax.ShapeDtypeStruct((t, WIDTH_A), F32),
                   jax.ShapeDtypeStruct((3, t, LANE), F32), jax.ShapeDtypeStruct((3, t, LANE), F32),
                   jax.ShapeDtypeStruct((3, t, LANE), F32), jax.ShapeDtypeStruct((t, WIDTH_M), BF16)],
        compiler_params=_cparams(("arbitrary",)),
        name="premix",
    )(x, w["g1"], w["w_in"], w["gva"], w["gq"], w["gk"], w["gqm"], w["bd"], *tabs)


def _rope_tables(pos):
    pos = np.asarray(pos, np.float64)
    inv_freq = np.power(ROPE_THETA, -np.arange(ROT_HALF, dtype=np.float64) / ROT_HALF)
    ang = pos[:, None] * inv_freq[None, :]
    cos, sin = np.cos(ang), np.sin(ang)
    t = pos.shape[0]
    rest = HEAD_DIM - 2 * ROT_HALF
    c = np.concatenate([cos, cos, np.ones((t, rest))], axis=1)
    s1 = np.concatenate([-sin, np.zeros((t, HEAD_DIM - ROT_HALF))], axis=1)
    s2 = np.concatenate([np.zeros((t, ROT_HALF)), sin, np.zeros((t, rest))], axis=1)
    two = lambda a: jnp.asarray(np.concatenate([a, a], axis=1), F32)
    return two(c), two(s1), two(s2)


def _attn_body(q_ref, kc_ref, kp_ref, vc_ref, vp_ref, o_ref, kk, vv, m_s, l_s, a_s):
    span_idx = pl.program_id(1)
    p0 = span_idx * SPAN
    lane = lax.broadcasted_iota(jnp.int32, (QB, LANE), 1)
    low = lane < HEAD_DIM
    qi = lax.broadcasted_iota(jnp.int32, (QB, 2 * QB), 0)
    kj = lax.broadcasted_iota(jnp.int32, (QB, 2 * QB), 1)
    band = (kj >= qi) & (kj <= qi + N_SUB)
    kk[0:SPAN, :] = kp_ref[0]
    kk[SPAN:2 * SPAN, :] = kc_ref[0]
    vv[0:SPAN, :] = vp_ref[0]
    vv[SPAN:2 * SPAN, :] = vc_ref[0]

    def rows(ref, start, n, d):
        if d == 1:
            return ref[pl.ds(pl.multiple_of(start, QB), n), :]
        return ref[pl.ds(start, n, stride=d), :]

    def unit(d, qstart, first):
        kpos0 = p0 + qstart - QB * d
        mask = band & (kpos0 + d * kj >= 0)
        qb = rows(q_ref.at[0], qstart, QB, d)
        kb = rows(kk, SPAN + qstart - QB * d, 2 * QB, d).astype(BF16)
        vb = rows(vv, SPAN + qstart - QB * d, 2 * QB, d).astype(BF16)
        stats = []
        for hm in (low, jnp.logical_not(low)):
            qh = jnp.where(hm, qb, 0.0).astype(BF16)
            s = lax.dot_general(qh, kb, (((1,), (1,)), ((), ())), preferred_element_type=F32)
            s = jnp.where(mask, s, NEG)
            m = jnp.max(s, axis=-1, keepdims=True)
            e = jnp.exp(s - m)
            l = jnp.sum(e, axis=-1, keepdims=True)
            acc = jnp.dot(e.astype(BF16), vb, preferred_element_type=F32)
            stats.append((m, l, acc))
        m_new = jnp.where(low, stats[0][0], stats[1][0])
        l_new = jnp.where(low, stats[0][1], stats[1][1])
        a_new = jnp.where(low, stats[0][2], stats[1][2])
        if d == 1:
            sl = (pl.ds(pl.multiple_of(qstart, QB), QB), slice(None))
        else:
            sl = (pl.ds(qstart, QB, stride=d), slice(None))
        if first:
            m_s[sl] = m_new
            l_s[sl] = l_new
            a_s[sl] = a_new
        else:
            m_old, l_old, a_old = m_s[sl], l_s[sl], a_s[sl]
            m_t = jnp.maximum(m_old, m_new)
            wa = jnp.exp(m_old - m_t)
            wb = jnp.exp(m_new - m_t)
            m_s[sl] = m_t
            l_s[sl] = wa * l_old + wb * l_new
            a_s[sl] = wa * a_old + wb * a_new

    nblk = SPAN // QB
    first = True
    for _, d in DILATIONS:
        per_res = nblk // d
        def group(gi, c, d=d, per_res=per_res, first=first):
            for uu in range(ATTN_UNROLL):
                u = gi * ATTN_UNROLL + uu
                unit(d, u // per_res + d * QB * (u % per_res), first)
            return c
        lax.fori_loop(0, nblk // ATTN_UNROLL, group, 0)
        first = False
    o_ref[...] = (a_s[...] / l_s[...]).astype(o_ref.dtype)


def _attn_prompt(q3, k3, v3, batch, seq):
    nspan = seq // SPAN
    cur = pl.BlockSpec((1, SPAN, LANE), lambda b, s, p: (p, b * nspan + s, 0))
    prv = pl.BlockSpec((1, SPAN, LANE), lambda b, s, p: (p, b * nspan + jnp.maximum(s - 1, 0), 0))
    return pl.pallas_call(
        _attn_body,
        grid=(batch, nspan, 3),
        in_specs=[cur, cur, prv, cur, prv],
        out_specs=pl.BlockSpec((SPAN, LANE), lambda b, s, p: (b * nspan + s, p)),
        out_shape=jax.ShapeDtypeStruct((batch * seq, WIDTH_B), BF16),
        scratch_shapes=[pltpu.VMEM((2 * SPAN, LANE), F32)] * 2 + [pltpu.VMEM((SPAN, LANE), F32)] * 3,
        compiler_params=_cparams(("arbitrary", "arbitrary", "arbitrary")),
        name="attn_prompt",
    )(q3, k3, k3, v3, v3)


def _memkv_body(mem_ref, g_ref, w_ref, gk_ref, bd_ref, k_ref, v_ref):
    x = mem_ref[...]
    ms = jnp.mean(x * x, axis=-1, keepdims=True)
    h = (x * lax.rsqrt(ms + EPS) * g_ref[...]).astype(BF16)
    kv = jnp.dot(h, w_ref[...], preferred_element_type=F32)
    n = x.shape[0]
    kt = [kv[:, LANE * j:LANE * (j + 1)] for j in range(2)]
    sq = jnp.concatenate([(t * t).astype(BF16) for t in kt], axis=0)
    ssum = jnp.dot(sq, bd_ref[...], preferred_element_type=F32)
    for j in range(2):
        inv = lax.rsqrt(ssum[j * n:(j + 1) * n] * (1.0 / HEAD_DIM) + EPS)
        k_ref[:, LANE * j:LANE * (j + 1)] = kt[j] * inv * gk_ref[...]
    v_ref[...] = kv[:, WIDTH_M:]


def _memkv(mem, w):
    n = mem.shape[0]
    return pl.pallas_call(
        _memkv_body,
        out_shape=[jax.ShapeDtypeStruct((n, WIDTH_M), F32)] * 2,
        compiler_params=pltpu.CompilerParams(vmem_limit_bytes=VMEM_LIMIT),
        name="memkv",
    )(mem, w["gmem"], w["w_mem_kv"], w["gkm"], w["bd"])


def _memattn_body(q_ref, k_ref, v_ref, o_ref):
    lane = lax.broadcasted_iota(jnp.int32, (q_ref.shape[0], LANE), 1)
    low = lane < HEAD_DIM
    for j in range(2):
        qp = q_ref[:, LANE * j:LANE * (j + 1)].astype(F32)
        kp = k_ref[0, :, LANE * j:LANE * (j + 1)].astype(BF16)
        vp = v_ref[0, :, LANE * j:LANE * (j + 1)].astype(BF16)
        outs = []
        for hm in (low, jnp.logical_not(low)):
            qh = jnp.where(hm, qp, 0.0).astype(BF16)
            s = lax.dot_general(qh, kp, (((1,), (1,)), ((), ())), preferred_element_type=F32)
            m = jnp.max(s, axis=-1, keepdims=True)
            e = jnp.exp(s - m)
            l = jnp.sum(e, axis=-1, keepdims=True)
            outs.append(jnp.dot(e.astype(BF16), vp, preferred_element_type=F32) / l)
        o_ref[:, LANE * j:LANE * (j + 1)] = jnp.where(low, outs[0], outs[1]).astype(o_ref.dtype)


def _memattn_prompt(qm, km, vm, batch, seq):
    tiles_per_b = seq // TM
    kv = pl.BlockSpec((1, N_MEM, WIDTH_M), lambda i: (i // tiles_per_b, 0, 0))
    return pl.pallas_call(
        _memattn_body,
        grid=(batch * tiles_per_b,),
        in_specs=[pl.BlockSpec((TM, WIDTH_M), lambda i: (i, 0)), kv, kv],
        out_specs=pl.BlockSpec((TM, WIDTH_M), lambda i: (i, 0)),
        out_shape=jax.ShapeDtypeStruct((batch * seq, WIDTH_M), BF16),
        compiler_params=_cparams(("arbitrary",)),
        name="memattn_prompt",
    )(qm, km, vm)


def _sample_attn_body(qbd_ref, kt_ref, vt_ref, kn_ref, vn_ref, cnt_ref, cntn_ref, hmask_ref,
                      qmbd_ref, kmt_ref, vmt_ref, hmaskm_ref, ob_ref, om_ref):
    dec = kn_ref.shape[1]
    qbd = qbd_ref[0]
    kt = kt_ref[0].astype(BF16)
    vt = vt_ref[0].astype(BF16)
    s = jnp.dot(qbd, kt, preferred_element_type=F32)
    qf = qbd.astype(F32)
    kn = kn_ref[0]
    vn = vn_ref[0]
    cnt = cnt_ref[...]
    cntn = cntn_ref[...]
    s_new = [jnp.sum(qf * kn[j:j + 1, :], axis=-1, keepdims=True) for j in range(dec)]
    m = jnp.max(jnp.where(cnt > 0, s, NEG), axis=-1, keepdims=True)
    for j in range(dec):
        m = jnp.maximum(m, jnp.where(cntn[:, j:j + 1] > 0, s_new[j], NEG))
    e = cnt * jnp.exp(jnp.where(cnt > 0, s - m, 0.0))
    l = jnp.sum(e, axis=-1, keepdims=True)
    acc = lax.dot_general(e.astype(BF16), vt, (((1,), (1,)), ((), ())), preferred_element_type=F32)
    for j in range(dec):
        w = cntn[:, j:j + 1]
        ej = w * jnp.exp(jnp.where(w > 0, s_new[j] - m, 0.0))
        l = l + ej
        acc = acc + ej * vn[j:j + 1, :]
    r = acc / l * hmask_ref[...]
    out = r[0:8]
    for h in range(1, WIDTH_B // HEAD_DIM):
        out = out + r[8 * h:8 * h + 8]
    ob_ref[0] = out
    qm = qmbd_ref[0]
    sm = jnp.dot(qm, kmt_ref[0].astype(BF16), preferred_element_type=F32)
    mm = jnp.max(sm, axis=-1, keepdims=True)
    em = jnp.exp(sm - mm)
    lm = jnp.sum(em, axis=-1, keepdims=True)
    am = lax.dot_general(em.astype(BF16), vmt_ref[0].astype(BF16), (((1,), (1,)), ((), ())),
                         preferred_element_type=F32)
    rm = am / lm * hmaskm_ref[...]
    outm = rm[0:8]
    for h in range(1, WIDTH_M // HEAD_DIM):
        outm = outm + rm[8 * h:8 * h + 8]
    om_ref[0] = outm


def _sample_counts(dec, w_buf):
    t = np.arange(8)[:, None]
    t = np.where(t < dec, t, 0)
    def mult(dist):
        c = np.zeros(dist.shape, np.float32)
        for window, dil in DILATIONS:
            c += ((dist >= 0) & (dist % dil == 0) & (dist <= window)).astype(np.float32)
        return c
    cache = mult(w_buf + t - np.arange(w_buf)[None, :])
    new = mult(t - np.arange(dec)[None, :])
    nb, nm = WIDTH_B // HEAD_DIM, WIDTH_M // HEAD_DIM
    hmask = (np.arange(8 * nb)[:, None] // 8 == np.arange(WIDTH_B)[None, :] // HEAD_DIM).astype(np.float32)
    hmaskm = (np.arange(8 * nm)[:, None] // 8 == np.arange(WIDTH_M)[None, :] // HEAD_DIM).astype(np.float32)
    return np.tile(cache, (nb, 1)), np.tile(new, (nb, 1)), hmask, hmaskm


def _block_diag_queries(q, dec, hmask):
    width = q.shape[-1]
    nh = width // HEAD_DIM
    qb = q.reshape(-1, 1, dec, width)
    qb = jnp.pad(qb, ((0, 0), (0, 0), (0, 8 - dec), (0, 0)))
    qb = jnp.broadcast_to(qb, (qb.shape[0], nh, 8, width)).reshape(-1, 8 * nh, width)
    return (qb * hmask[None]).astype(BF16)


def _sample_attn(q, kn, vn, qm, kt, vt, kmt, vmt, dec):
    bd = kt.shape[0]
    w_buf = kt.shape[-1]
    cnt, cntn, hmask, hmaskm = _sample_counts(dec, w_buf)
    qbd = _block_diag_queries(q, dec, hmask)
    qmbd = _block_diag_queries(qm.astype(F32), dec, hmaskm)
    nb8, nm8 = qbd.shape[1], qmbd.shape[1]
    per_b = lambda shape: pl.BlockSpec((1,) + shape, lambda b: (b,) + (0,) * len(shape))
    full = lambda shape: pl.BlockSpec(shape, lambda b: (0,) * len(shape))
    ob, om = pl.pallas_call(
        _sample_attn_body,
        grid=(bd,),
        in_specs=[per_b((nb8, WIDTH_B)), per_b((WIDTH_B, w_buf)), per_b((WIDTH_B, w_buf)),
                  per_b((dec, WIDTH_B)), per_b((dec, WIDTH_B)),
                  full((nb8, w_buf)), full((nb8, dec)), full((nb8, WIDTH_B)),
                  per_b((nm8, WIDTH_M)), per_b((WIDTH_M, N_MEM)), per_b((WIDTH_M, N_MEM)), full((nm8, WIDTH_M))],
        out_specs=[per_b((8, WIDTH_B)), per_b((8, WIDTH_M))],
        out_shape=[jax.ShapeDtypeStruct((bd, 8, WIDTH_B), F32), jax.ShapeDtypeStruct((bd, 8, WIDTH_M), F32)],
        compiler_params=_cparams(("arbitrary",)),
        name="sample_attn",
    )(qbd, kt, vt, kn.reshape(bd, dec, WIDTH_B), vn.reshape(bd, dec, WIDTH_B),
      jnp.asarray(cnt), jnp.asarray(cntn), jnp.asarray(hmask),
      qmbd, kmt, vmt, jnp.asarray(hmaskm))
    return (ob[:, :dec].reshape(bd * dec, WIDTH_B).astype(BF16),
            om[:, :dec].reshape(bd * dec, WIDTH_M).astype(BF16))


def _post_body(*refs, tc, aliased):
    (x_ref, u_ref, va_ref, ob_ref, om_ref, wg_ref, bg_ref, wout_ref, g2_ref, wr_ref, br_ref,
     tri_ref, upper_ref) = refs[:13]
    x1_ref, xs_ref, ld_ref, tg_ref, seg_ref = refs[13 + aliased:]
    tm = x_ref.shape[0]
    u = u_ref[...]
    if tc is None:
        vaf = va_ref[...]
        mixed = wg_ref[0] * vaf + bg_ref[...]
        for s in range(1, wg_ref.shape[0]):
            mixed = mixed + wg_ref[s] * pltpu.roll(vaf, s, 0)
        oa = u * mixed
    else:
        lane = lax.broadcasted_iota(jnp.int32, (tc, LANE), 1)
        low = lane < HEAD_DIM
        va = va_ref[...].astype(BF16)
        oa_rows = []
        for c in range(tm // tc):
            r0 = c * tc
            tiles = []
            for p in range(3):
                vp = va[r0:r0 + tc, LANE * p:LANE * (p + 1)]
                r = jnp.dot(wg_ref[p], vp, preferred_element_type=F32)
                tiles.append(jnp.where(low, r[:tc], r[tc:]))
            mixed = jnp.concatenate(tiles, axis=1) + bg_ref[...]
            oa_rows.append(u[r0:r0 + tc] * mixed)
        oa = jnp.concatenate(oa_rows, axis=0)
    mixed_all = jnp.concatenate([oa.astype(BF16), ob_ref[...], om_ref[...]], axis=1)
    x1 = x_ref[...] + jnp.dot(mixed_all, wout_ref[...], preferred_element_type=F32)
    x1_ref[...] = x1
    ms = jnp.mean(x1 * x1, axis=-1, keepdims=True)
    h2 = x1 * lax.rsqrt(ms + EPS) * g2_ref[...]
    h_hi = h2.astype(BF16)
    h_lo = (h2 - h_hi.astype(F32)).astype(BF16)
    logits = (jnp.dot(h_hi, wr_ref[0], preferred_element_type=F32)
              + jnp.dot(h_lo, wr_ref[0], preferred_element_type=F32)
              + jnp.dot(h_hi, wr_ref[1], preferred_element_type=F32)) + br_ref[...]
    lane_i = lax.broadcasted_iota(jnp.int32, (tm, LANE), 1)
    lane_r = lane_i.astype(F32)
    vals = logits
    tops, idxs = [], []
    for _ in range(TOP_K):
        mk = jnp.max(vals, axis=-1, keepdims=True)
        ik = jnp.min(jnp.where(vals == mk, lane_r, float(LANE)), axis=-1, keepdims=True)
        vals = jnp.where(lane_r == ik, -jnp.inf, vals)
        tops.append(mk)
        idxs.append(ik)
    es = [jnp.exp(t - tops[0]) for t in tops]
    den = es[0] + es[1] + es[2] + es[3]
    tg = jnp.zeros((tm, LANE), F32)
    for k in range(TOP_K):
        tg = jnp.where(lane_i == k, es[k] / den, tg)
    tg_ref[...] = tg
    col = lax.broadcasted_iota(jnp.int32, (MOE_TM, SEG_ROWS), 1).astype(F32)
    lane_t = lax.broadcasted_iota(jnp.int32, (MOE_TM, LANE), 1)
    hot_all = [lane_r == idxs[k] for k in range(TOP_K)]
    for hf in range(tm // MOE_TM):
        r0 = hf * MOE_TM
        hot = [h[r0:r0 + MOE_TM] for h in hot_all]
        sel = jnp.zeros((MOE_TM, LANE), F32)
        for k in range(TOP_K):
            sel = sel + jnp.where(hot[k], 1.0, 0.0)
        rank = jnp.dot(tri_ref[...], sel.astype(BF16), preferred_element_type=F32)
        length = jnp.sum(sel, axis=0, keepdims=True)
        plen = jnp.floor((length + (PIECE - 1)) * (1.0 / PIECE)) * PIECE
        loff = jnp.dot(jnp.broadcast_to(plen, (8, LANE)).astype(BF16), upper_ref[...],
                       preferred_element_type=F32)[0:1]
        base = loff + rank
        q = jnp.zeros((MOE_TM, SEG_ROWS), F32)
        ld = jnp.zeros((MOE_TM, LANE), F32)
        for k in range(TOP_K):
            ld_k = jnp.sum(jnp.where(hot[k], base, 0.0), axis=-1, keepdims=True)
            q = q + jnp.where(col == ld_k, 1.0, 0.0)
            ld = jnp.where(lane_t == k, ld_k, ld)
        xs_ref[hf * SEG_ROWS:(hf + 1) * SEG_ROWS, :] = lax.dot_general(
            q.astype(BF16), h_hi[r0:r0 + MOE_TM], (((0,), (0,)), ((), ())), preferred_element_type=F32)
        ld_ref[r0:r0 + MOE_TM, :] = ld.astype(jnp.int32)
        seg_ref[hf * 8:(hf + 1) * 8, :] = jnp.broadcast_to(length, (8, LANE)).astype(jnp.int32)


def _post(x, u, va, ob, om, wg, bg, w, tc, total_tiles, tile0=0, xs_all=None):
    t = x.shape[0]
    nt = t // TM
    full = lambda shape: pl.BlockSpec(shape, lambda i: (0,) * len(shape))
    row = lambda width: pl.BlockSpec((TM, width), lambda i: (i, 0))
    per = TM // MOE_TM
    ix = np.arange(MOE_TM)
    tri = jnp.asarray(ix[:, None] > ix[None, :], BF16)
    ex = np.arange(LANE)
    upper = jnp.asarray(ex[:, None] < ex[None, :], BF16)
    step0 = tile0 // per
    in_specs = [row(D_MODEL), row(WIDTH_A), row(WIDTH_A), row(WIDTH_B), row(WIDTH_M),
                full(wg.shape), full(bg.shape), full((D_MODEL, D_MODEL)), full((1, D_MODEL)),
                full((2, D_MODEL, LANE)), full((1, LANE)), full((MOE_TM, MOE_TM)), full((LANE, LANE))]
    args = [x, u, va, ob, om, wg, bg, w["w_out"], w["g2"], w["w_router"], w["b_router"], tri, upper]
    aliases = {}
    if xs_all is not None:
        in_specs.append(pl.BlockSpec(memory_space=pl.ANY))
        args.append(xs_all)
        aliases = {len(args) - 1: 1}
    return pl.pallas_call(
        functools.partial(_post_body, tc=tc, aliased=int(xs_all is not None)),
        grid=(nt,),
        in_specs=in_specs,
        out_specs=[row(D_MODEL), pl.BlockSpec((per * SEG_ROWS, D_MODEL), lambda i: (i + step0, 0)),
                   row(LANE), row(LANE), pl.BlockSpec((per * 8, LANE), lambda i: (i, 0))],
        out_shape=[jax.ShapeDtypeStruct((t, D_MODEL), F32),
                   jax.ShapeDtypeStruct((total_tiles * SEG_ROWS, D_MODEL), F32),
                   jax.ShapeDtypeStruct((t, LANE), jnp.int32), jax.ShapeDtypeStruct((t, LANE), F32),
                   jax.ShapeDtypeStruct((t // MOE_TM * 8, LANE), jnp.int32)],
        input_output_aliases=aliases,
        compiler_params=_cparams(("arbitrary",)),
        name="post",
    )(*args)


def _experts_body(te_ref, valid_ref, first_ref, next_ref, src_ref, dst_ref,
                  xs_hbm, wgu_hbm, bgu_ref, wd_hbm, bd_ref, ys_hbm,
                  xbuf, ybuf, wgu_f, wd_f, wgu_s, wd_s, gsem, ssem, wsem):
    t = pl.program_id(0)
    nt = pl.num_programs(0)
    slot = t % 2

    def gather(tile, sl):
        for i in range(NPIECE):
            s = src_ref[tile * NPIECE + i]
            pltpu.make_async_copy(xs_hbm.at[pl.ds(pl.multiple_of(s * PIECE, PIECE), PIECE), :],
                                  xbuf.at[sl, pl.ds(i * PIECE, PIECE), :], gsem.at[sl]).start()

    def scatter(tile, sl):
        for i in range(NPIECE):
            d = dst_ref[tile * NPIECE + i]
            pltpu.make_async_copy(ybuf.at[sl, pl.ds(i * PIECE, PIECE), :],
                                  ys_hbm.at[pl.ds(pl.multiple_of(d * PIECE, PIECE), PIECE), :], ssem.at[sl]).start()

    def wait_tile(hbm, buf, sem, sl):
        pltpu.make_async_copy(hbm.at[pl.ds(0, MOE_TM), :], buf.at[sl], sem.at[sl]).wait()

    def weight_copies(e):
        return (pltpu.make_async_copy(wgu_hbm.at[e], wgu_f, wsem.at[0]),
                pltpu.make_async_copy(wd_hbm.at[e], wd_f, wsem.at[1]))

    @pl.when(t == 0)
    def _():
        for c in weight_copies(te_ref[0]):
            c.start()
        gather(0, 0)

    @pl.when(valid_ref[t] > 0)
    def _():
        nxt = jnp.minimum(t + 1, nt - 1)
        has_next = jnp.logical_and(t + 1 < nt, valid_ref[nxt] > 0)

        @pl.when(has_next)
        def _():
            gather(t + 1, 1 - slot)

        @pl.when(first_ref[t] > 0)
        def _():
            for c in weight_copies(te_ref[t]):
                c.wait()
            wgu_s[...] = wgu_f[...].astype(BF16)
            wd_s[...] = wd_f[...].astype(BF16)

            @pl.when(next_ref[t] >= 0)
            def _():
                for c in weight_copies(next_ref[t]):
                    c.start()

        wait_tile(xs_hbm, xbuf, gsem, slot)
        x = xbuf[slot].astype(BF16)
        gu = jnp.dot(x, wgu_s[...], preferred_element_type=F32) + bgu_ref[0]
        gate = jnp.minimum(gu[:, :D_MODEL], SWIGLU_LIMIT)
        up = jnp.clip(gu[:, D_MODEL:], -SWIGLU_LIMIT, SWIGLU_LIMIT)
        act = (up + 1.0) * (gate * (1.0 / (1.0 + jnp.exp(-SWIGLU_ALPHA * gate))))
        y = jnp.dot(act.astype(BF16), wd_s[...], preferred_element_type=F32) + bd_ref[0]

        @pl.when(t >= 2)
        def _():
            wait_tile(ys_hbm, ybuf, ssem, slot)

        ybuf[slot] = y
        scatter(t, slot)

        @pl.when(jnp.logical_not(has_next))
        def _():
            wait_tile(ys_hbm, ybuf, ssem, slot)

            @pl.when(t >= 1)
            def _():
                wait_tile(ys_hbm, ybuf, ssem, 1 - slot)


def _experts(plan, xs, w, ys_rows):
    n_tiles = plan["tile_expert"].shape[0]
    by_expert = lambda shape: pl.BlockSpec((1,) + shape, lambda t, te, *_: (te[t],) + (0,) * len(shape))
    hbm = pl.BlockSpec(memory_space=pl.ANY)
    return pl.pallas_call(
        _experts_body,
        grid_spec=pltpu.PrefetchScalarGridSpec(
            num_scalar_prefetch=6, grid=(n_tiles,),
            in_specs=[hbm, hbm, by_expert((1, 2 * D_MODEL)), hbm, by_expert((1, D_MODEL))],
            out_specs=hbm,
            scratch_shapes=[pltpu.VMEM((2, MOE_TM, D_MODEL), F32), pltpu.VMEM((2, MOE_TM, D_MODEL), F32),
                            pltpu.VMEM((D_MODEL, 2 * D_MODEL), F32), pltpu.VMEM((D_MODEL, D_MODEL), F32),
                            pltpu.VMEM((D_MODEL, 2 * D_MODEL), BF16), pltpu.VMEM((D_MODEL, D_MODEL), BF16),
                            pltpu.SemaphoreType.DMA((2,)), pltpu.SemaphoreType.DMA((2,)),
                            pltpu.SemaphoreType.DMA((2,))]),
        out_shape=jax.ShapeDtypeStruct((ys_rows, D_MODEL), F32),
        compiler_params=_cparams(("arbitrary",)),
        name="moe_experts",
    )(plan["tile_expert"], plan["tile_valid"], plan["tile_first"], plan["tile_next"], plan["src"], plan["dst"],
      xs, w["w_gate_up"], w["b_gate_up"], w["w_down"], w["b_down"])


def _combine_body(used_ref, ld_ref, g_ref, x1_ref, ys_ref, y_ref, *, tile0):
    t = pl.program_id(0)
    row = lax.broadcasted_iota(jnp.int32, (SEG_ROWS, 1), 0)
    ys = jnp.where(row < used_ref[tile0 + t], ys_ref[...], 0.0).astype(BF16)
    col = lax.broadcasted_iota(jnp.int32, (MOE_TM, SEG_ROWS), 1)
    ld = ld_ref[...]
    g = g_ref[...]
    p = jnp.zeros((MOE_TM, SEG_ROWS), F32)
    for k in range(TOP_K):
        p = jnp.where(col == ld[:, k:k + 1], g[:, k:k + 1], p)
    y_ref[...] = x1_ref[...] + jnp.dot(p.astype(BF16), ys, preferred_element_type=F32)


def _combine(used, ld, gates, x1, ys, tile0):
    n = x1.shape[0]
    nt = n // MOE_TM
    rows = lambda width: pl.BlockSpec((MOE_TM, width), lambda t, *_: (t, 0))
    return pl.pallas_call(
        functools.partial(_combine_body, tile0=tile0),
        grid_spec=pltpu.PrefetchScalarGridSpec(
            num_scalar_prefetch=1, grid=(nt,),
            in_specs=[rows(LANE), rows(LANE), rows(D_MODEL),
                      pl.BlockSpec((SEG_ROWS, D_MODEL), lambda t, *_: (t + tile0, 0))],
            out_specs=rows(D_MODEL)),
        out_shape=jax.ShapeDtypeStruct((n, D_MODEL), F32),
        compiler_params=_cparams(("arbitrary",)),
        name="moe_combine",
    )(used, ld, gates, x1, ys)


def _moe(groups, xs, seglen, w):
    i32 = jnp.int32
    nt = seglen.shape[0]
    plen = (seglen + PIECE - 1) // PIECE * PIECE
    loff = jnp.cumsum(plen, axis=1) - plen
    used = jnp.sum(plen, axis=1).astype(i32)
    pp = plen // PIECE
    cp_end = jnp.cumsum(pp, axis=0)
    cp = cp_end - pp
    cnt_e = cp_end[-1]
    tiles_e = (cnt_e + NPIECE - 1) // NPIECE
    tile_end = jnp.cumsum(tiles_e)
    tile_start = tile_end - tiles_e
    n_tiles = (nt * SEG_ROWS + MOE_TM - 1) // MOE_TM + N_EXPERTS
    tix = jnp.arange(n_tiles, dtype=i32)
    total_tiles = tile_end[-1]
    tile_valid = (tix < total_tiles).astype(i32)
    raw_expert = jnp.minimum((tix[:, None] >= tile_end[None, :]).astype(i32).sum(axis=1), N_EXPERTS - 1)
    last_expert = raw_expert[jnp.maximum(total_tiles - 1, 0)]
    tile_expert = jnp.where(tile_valid > 0, raw_expert, last_expert)
    tile_first = ((tix == tile_start[tile_expert]) & (tile_valid > 0)).astype(i32)
    following = tile_end[tile_expert]
    tile_next = jnp.where(following < total_tiles, raw_expert[jnp.minimum(following, n_tiles - 1)], -1)
    q = jnp.arange(n_tiles * NPIECE, dtype=i32)
    qt = q // NPIECE
    e_q = tile_expert[qt]
    j = q - tile_start[e_q] * NPIECE
    ok = (tile_valid[qt] > 0) & (j < cnt_e[e_q])
    t_q = jnp.minimum((jnp.transpose(cp_end)[e_q] <= j[:, None]).astype(i32).sum(axis=1), nt - 1)
    flat = t_q * N_EXPERTS + e_q
    piece = t_q * SEG_PIECES + loff.reshape(-1)[flat] // PIECE + j - cp.reshape(-1)[flat]
    src = jnp.where(ok, piece, 0)
    dump = nt * SEG_PIECES + (qt % 2) * NPIECE + q % NPIECE
    dst = jnp.where(ok, piece, dump)
    plan = dict(tile_expert=tile_expert.astype(i32), tile_valid=tile_valid, tile_first=tile_first,
                tile_next=tile_next.astype(i32), src=src.astype(i32), dst=dst.astype(i32))

    ys = _experts(plan, xs, w, nt * SEG_ROWS + 2 * MOE_TM)
    outs, r0 = [], 0
    for x1, ld, tg in groups:
        outs.append(_combine(used, ld, tg, x1, ys, r0 // MOE_TM))
        r0 += x1.shape[0]
    return outs


def _pair_major_to_rows(a3):
    return jnp.transpose(a3, (1, 0, 2)).reshape(a3.shape[1], 3 * LANE)


def kernel(x_prompt, x_sample, mem_prompt, cache_win_k, cache_win_v, cache_mem_k, cache_mem_v, norm1_g, w_in, gv_a, w_s, b_s, gq_b, gk_b, gq_m, gk_m, mem_norm_g, w_mem_kv, w_out, norm2_g, w_router, b_router, w_gate_up, b_gate_up, w_down, b_down):
    batch, seq, _ = x_prompt.shape
    bd, dec, _ = x_sample.shape
    depth = norm1_g.shape[0]
    assert depth == 1 and seq % SPAN == 0 and (bd * dec) % TM == 0 and PAST_LEN % CHUNK == 0
    w_buf = cache_win_k.shape[2]
    assert w_buf == MAX_WINDOW and dec <= 8
    l = 0
    two = lambda g: jnp.concatenate([g, g])[None, :]
    head = np.arange(LANE) // HEAD_DIM
    wr = jnp.pad(w_router[l], ((0, 0), (0, LANE - N_EXPERTS)))
    wr_hi = wr.astype(BF16)
    wr_lo = (wr - wr_hi.astype(F32)).astype(BF16)
    w = dict(
        g1=norm1_g[l][None], w_in=w_in[l].astype(BF16), gva=gv_a[l][None],
        gq=two(gq_b[l]), gk=two(gk_b[l]), gqm=two(gq_m[l]), gkm=two(gk_m[l]),
        bd=jnp.asarray(head[:, None] == head[None, :], BF16),
        gmem=mem_norm_g[l][None], w_mem_kv=w_mem_kv[l].astype(BF16),
        w_out=w_out[l].astype(BF16), g2=norm2_g[l][None],
        w_router=jnp.stack([wr_hi, wr_lo]),
        b_router=jnp.pad(b_router[l], (0, LANE - N_EXPERTS), constant_values=-jnp.inf)[None],
        w_gate_up=w_gate_up[l], b_gate_up=b_gate_up[l][:, None, :], w_down=w_down[l], b_down=b_down[l][:, None, :],
    )
    ngrp = WIDTH_A // HEAD_DIM
    wtri = jnp.where(jnp.tril(jnp.ones((CHUNK, CHUNK), bool)), w_s[l], 0).astype(BF16)
    wg_p = wtri.reshape(ngrp // 2, 2 * CHUNK, CHUNK)
    bg_p = jnp.repeat(jnp.transpose(b_s[l]), HEAD_DIM, axis=1)
    zero = jnp.zeros((ngrp,), F32)
    lanes = lambda tg_: jnp.tile(jnp.repeat(tg_, HEAD_DIM, axis=1), (bd, 1))
    wg_s = jnp.stack([lanes(jnp.stack([w_s[l][:, t, t - s] if t >= s else zero for t in range(dec)]))
                      for s in range(dec)])
    bg_s = lanes(jnp.transpose(b_s[l][:, :dec]))

    xp = x_prompt.reshape(batch * seq, D_MODEL)
    tabs_p = _rope_tables(np.arange(seq))
    u_p, va_p, q3_p, k3_p, v3_p, qm_p = _premix(xp, tabs_p, seq // TM, w)
    ob_p = _attn_prompt(q3_p, k3_p, v3_p, batch, seq)
    km, vm = _memkv(mem_prompt.reshape(batch * N_MEM, D_MODEL), w)
    om_p = _memattn_prompt(qm_p, km.reshape(batch, N_MEM, WIDTH_M), vm.reshape(batch, N_MEM, WIDTH_M), batch, seq)
    tiles_p = batch * seq // MOE_TM
    tiles_all = tiles_p + bd * dec // MOE_TM
    x1_p, xs_all, ld_p, tg_p, seg_p = _post(xp, u_p, va_p, ob_p, om_p, wg_p, bg_p, w, CHUNK, tiles_all)

    xs = x_sample.reshape(bd * dec, D_MODEL)
    tabs_s = _rope_tables(np.tile(PAST_LEN + np.arange(dec), bd))
    u_s, va_s, q3_s, k3_s, v3_s, qm_s = _premix(xs, tabs_s, 1, w)
    q_s, k_s, v_s = (_pair_major_to_rows(a) for a in (q3_s, k3_s, v3_s))
    nb = WIDTH_B // HEAD_DIM
    nm = WIDTH_M // HEAD_DIM
    kt = jnp.transpose(cache_win_k[l], (0, 2, 3, 1)).reshape(bd, WIDTH_B, w_buf)
    vt = jnp.transpose(cache_win_v[l], (0, 2, 3, 1)).reshape(bd, WIDTH_B, w_buf)
    kmt = jnp.transpose(cache_mem_k[l], (0, 2, 3, 1)).reshape(bd, WIDTH_M, N_MEM)
    vmt = jnp.transpose(cache_mem_v[l], (0, 2, 3, 1)).reshape(bd, WIDTH_M, N_MEM)
    ob_s, om_s = _sample_attn(q_s, k_s, v_s, qm_s, kt, vt, kmt, vmt, dec)
    x1_s, xs_all, ld_s, tg_s, seg_s = _post(xs, u_s, va_s, ob_s, om_s, wg_s, bg_s, w, None, tiles_all,
                                             tile0=tiles_p, xs_all=xs_all)

    seglen = jnp.concatenate([seg_p, seg_s])[::8, :N_EXPERTS]
    y_p, y_s = _moe([(x1_p, ld_p, tg_p), (x1_s, ld_s, tg_s)], xs_all, seglen, w)
    y_prompt = y_p.reshape(batch, seq, D_MODEL)
    y_sample = y_s.reshape(bd, dec, D_MODEL)

    n_keep = min(MAX_WINDOW, seq)

    def window_rows(a3):
        a = a3.reshape(3, batch, seq, LANE)[:, :, seq - n_keep:]
        return jnp.transpose(a, (1, 2, 0, 3)).reshape(1, batch, n_keep, nb, HEAD_DIM)

    return (y_prompt, y_sample,
            window_rows(k3_p), window_rows(v3_p),
            km.reshape(1, batch, N_MEM, nm, HEAD_DIM), vm.reshape(1, batch, N_MEM, nm, HEAD_DIM),
            k_s.reshape(1, bd, dec, nb, HEAD_DIM), v_s.reshape(1, bd, dec, nb, HEAD_DIM),
            va_s.reshape(1, bd, dec, WIDTH_A))
```

```python
import functools

import numpy as np
import jax
import jax.numpy as jnp
from jax import lax
from jax.experimental import pallas as pl
from jax.experimental.pallas import tpu as pltpu

F32 = jnp.float32
BF16 = jnp.bfloat16

D_MODEL = 1024
HEAD_DIM = 64
WIDTH_A = 384
WIDTH_B = 384
WIDTH_M = 256
IN_WIDTH = 2 * WIDTH_A + 3 * WIDTH_B + WIDTH_M
CHUNK = 128
DILATIONS = ((128, 1), (512, 4), (2048, 16))
N_SUB = 128
MAX_WINDOW = 2048
N_MEM = 256
ROPE_THETA = 500000.0
ROT_HALF = 8
SCALE = HEAD_DIM ** -0.5
N_EXPERTS = 32
TOP_K = 4
SWIGLU_LIMIT = 7.0
SWIGLU_ALPHA = 1.702
EPS = 1e-6
PAST_LEN = 8192

LANE = 128
NEG = -1e30
TM = 512
SPAN = 2048
QB = 128
ATTN_UNROLL = 4
MOE_TM = 256
GM_TM = 512
PIECE = 8
NPIECE = GM_TM // PIECE
SEG_ROWS = TOP_K * MOE_TM + N_EXPERTS * PIECE
SEG_PIECES = SEG_ROWS // PIECE
VMEM_LIMIT = 52 * 1024 * 1024


def _cparams(sem):
    return pltpu.CompilerParams(dimension_semantics=sem, vmem_limit_bytes=VMEM_LIMIT)


def _premix_body(x_ref, g1_ref, win_ref, gva_ref, gq_ref, gk_ref, gqm_ref, bd_ref,
                 rc_ref, rs1_ref, rs2_ref,
                 u_ref, va_ref, q3_ref, k3_ref, v3_ref, qm_ref):
    x = x_ref[...]
    tm = x.shape[0]
    ms = jnp.mean(x * x, axis=-1, keepdims=True)
    h = (x * lax.rsqrt(ms + EPS) * g1_ref[...]).astype(BF16)
    z = jnp.dot(h, win_ref[...], preferred_element_type=F32)
    u_ref[...] = z[:, :WIDTH_A]
    va = z[:, WIDTH_A:2 * WIDTH_A]
    va_ms = jnp.mean(va * va, axis=-1, keepdims=True)
    va_ref[...] = va * lax.rsqrt(va_ms + EPS) * gva_ref[...]
    q0, k0, v0, m0 = 2 * WIDTH_A, 2 * WIDTH_A + WIDTH_B, 2 * WIDTH_A + 2 * WIDTH_B, 2 * WIDTH_A + 3 * WIDTH_B
    tiles = ([z[:, q0 + LANE * j:q0 + LANE * (j + 1)] for j in range(3)]
             + [z[:, k0 + LANE * j:k0 + LANE * (j + 1)] for j in range(3)]
             + [z[:, m0 + LANE * j:m0 + LANE * (j + 1)] for j in range(2)])
    sq = jnp.concatenate([(t * t).astype(BF16) for t in tiles], axis=0)
    ssum = jnp.dot(sq, bd_ref[...], preferred_element_type=F32)
    inv = [lax.rsqrt(ssum[i * tm:(i + 1) * tm] * (1.0 / HEAD_DIM) + EPS) for i in range(8)]
    rc, rs1, rs2 = rc_ref[...], rs1_ref[...], rs2_ref[...]

    def rope(t):
        return t * rc + pltpu.roll(t, LANE - ROT_HALF, 1) * rs1 + pltpu.roll(t, ROT_HALF, 1) * rs2

    for j in range(3):
        q3_ref[j] = rope(tiles[j] * inv[j] * gq_ref[...]) * SCALE
        k3_ref[j] = rope(tiles[3 + j] * inv[3 + j] * gk_ref[...])
        v3_ref[j] = z[:, v0 + LANE * j:v0 + LANE * (j + 1)]
    for j in range(2):
        qm_ref[:, LANE * j:LANE * (j + 1)] = (tiles[6 + j] * inv[6 + j] * gqm_ref[...] * SCALE).astype(BF16)


def _premix(x, tabs, n_tab_tiles, w):
    t = x.shape[0]
    nt = t // TM
    full = lambda shape: pl.BlockSpec(shape, lambda i: (0,) * len(shape))
    tab = pl.BlockSpec((TM, LANE), lambda i: (i % n_tab_tiles, 0))
    row = lambda width: pl.BlockSpec((TM, width), lambda i: (i, 0))
    pair = pl.BlockSpec((3, TM, LANE), lambda i: (0, i, 0))
    return pl.pallas_call(
        _premix_body,
        grid=(nt,),
        in_specs=[row(D_MODEL), full((1, D_MODEL)), full((D_MODEL, IN_WIDTH)), full((1, WIDTH_A)),
                  full((1, LANE)), full((1, LANE)), full((1, LANE)), full((LANE, LANE)), tab, tab, tab],
        out_specs=[row(WIDTH_A), row(WIDTH_A), pair, pair, pair, row(WIDTH_M)],
        out_shape=[jax.ShapeDtypeStruct((t, WIDTH_A), F32), jax.ShapeDtypeStruct((t, WIDTH_A), F32),
                   jax.ShapeDtypeStruct((3, t, LANE), F32), jax.ShapeDtypeStruct((3, t, LANE), F32),
                   jax.ShapeDtypeStruct((3, t, LANE), F32), jax.ShapeDtypeStruct((t, WIDTH_M), BF16)],
        compiler_params=_cparams(("arbitrary",)),
        name="premix",
    )(x, w["g1"], w["w_in"], w["gva"], w["gq"], w["gk"], w["gqm"], w["bd"], *tabs)


def _rope_tables(pos):
    pos = np.asarray(pos, np.float64)
    inv_freq = np.power(ROPE_THETA, -np.arange(ROT_HALF, dtype=np.float64) / ROT_HALF)
    ang = pos[:, None] * inv_freq[None, :]
    cos, sin = np.cos(ang), np.sin(ang)
    t = pos.shape[0]
    rest = HEAD_DIM - 2 * ROT_HALF
    c = np.concatenate([cos, cos, np.ones((t, rest))], axis=1)
    s1 = np.concatenate([-sin, np.zeros((t, HEAD_DIM - ROT_HALF))], axis=1)
    s2 = np.concatenate([np.zeros((t, ROT_HALF)), sin, np.zeros((t, rest))], axis=1)
    two = lambda a: jnp.asarray(np.concatenate([a, a], axis=1), F32)
    return two(c), two(s1), two(s2)


def _attn_body(q_ref, kc_ref, kp_ref, vc_ref, vp_ref, o_ref, kk, vv, m_s, l_s, a_s):
    span_idx = pl.program_id(1)
    p0 = span_idx * SPAN
    lane = lax.broadcasted_iota(jnp.int32, (QB, LANE), 1)
    low = lane < HEAD_DIM
    qi = lax.broadcasted_iota(jnp.int32, (QB, 2 * QB), 0)
    kj = lax.broadcasted_iota(jnp.int32, (QB, 2 * QB), 1)
    band = (kj >= qi) & (kj <= qi + N_SUB)
    kk[0:SPAN, :] = kp_ref[0]
    kk[SPAN:2 * SPAN, :] = kc_ref[0]
    vv[0:SPAN, :] = vp_ref[0]
    vv[SPAN:2 * SPAN, :] = vc_ref[0]

    def rows(ref, start, n, d):
        if d == 1:
            return ref[pl.ds(pl.multiple_of(start, QB), n), :]
        return ref[pl.ds(start, n, stride=d), :]

    def unit(d, qstart, first):
        kpos0 = p0 + qstart - QB * d
        mask = band & (kpos0 + d * kj >= 0)
        qb = rows(q_ref.at[0], qstart, QB, d)
        kb = rows(kk, SPAN + qstart - QB * d, 2 * QB, d).astype(BF16)
        vb = rows(vv, SPAN + qstart - QB * d, 2 * QB, d).astype(BF16)
        stats = []
        for hm in (low, jnp.logical_not(low)):
            qh = jnp.where(hm, qb, 0.0).astype(BF16)
            s = lax.dot_general(qh, kb, (((1,), (1,)), ((), ())), preferred_element_type=F32)
            s = jnp.where(mask, s, NEG)
            m = jnp.max(s, axis=-1, keepdims=True)
            e = jnp.exp(s - m)
            l = jnp.sum(e, axis=-1, keepdims=True)
            acc = jnp.dot(e.astype(BF16), vb, preferred_element_type=F32)
            stats.append((m, l, acc))
        m_new = jnp.where(low, stats[0][0], stats[1][0])
        l_new = jnp.where(low, stats[0][1], stats[1][1])
        a_new = jnp.where(low, stats[0][2], stats[1][2])
        if d == 1:
            sl = (pl.ds(pl.multiple_of(qstart, QB), QB), slice(None))
        else:
            sl = (pl.ds(qstart, QB, stride=d), slice(None))
        if first:
            m_s[sl] = m_new
            l_s[sl] = l_new
            a_s[sl] = a_new
        else:
            m_old, l_old, a_old = m_s[sl], l_s[sl], a_s[sl]
            m_t = jnp.maximum(m_old, m_new)
            wa = jnp.exp(m_old - m_t)
            wb = jnp.exp(m_new - m_t)
            m_s[sl] = m_t
            l_s[sl] = wa * l_old + wb * l_new
            a_s[sl] = wa * a_old + wb * a_new

    nblk = SPAN // QB
    first = True
    for _, d in DILATIONS:
        per_res = nblk // d
        def group(gi, c, d=d, per_res=per_res, first=first):
            for uu in range(ATTN_UNROLL):
                u = gi * ATTN_UNROLL + uu
                unit(d, u // per_res + d * QB * (u % per_res), first)
            return c
        lax.fori_loop(0, nblk // ATTN_UNROLL, group, 0)
        first = False
    o_ref[...] = (a_s[...] / l_s[...]).astype(o_ref.dtype)


def _attn_prompt(q3, k3, v3, batch, seq):
    nspan = seq // SPAN
    cur = pl.BlockSpec((1, SPAN, LANE), lambda b, s, p: (p, b * nspan + s, 0))
    prv = pl.BlockSpec((1, SPAN, LANE), lambda b, s, p: (p, b * nspan + jnp.maximum(s - 1, 0), 0))
    return pl.pallas_call(
        _attn_body,
        grid=(batch, nspan, 3),
        in_specs=[cur, cur, prv, cur, prv],
        out_specs=pl.BlockSpec((SPAN, LANE), lambda b, s, p: (b * nspan + s, p)),
        out_shape=jax.ShapeDtypeStruct((batch * seq, WIDTH_B), BF16),
        scratch_shapes=[pltpu.VMEM((2 * SPAN, LANE), F32)] * 2 + [pltpu.VMEM((SPAN, LANE), F32)] * 3,
        compiler_params=_cparams(("arbitrary", "arbitrary", "arbitrary")),
        name="attn_prompt",
    )(q3, k3, k3, v3, v3)


def _memkv_body(mem_ref, g_ref, w_ref, gk_ref, bd_ref, k_ref, v_ref):
    x = mem_ref[...]
    ms = jnp.mean(x * x, axis=-1, keepdims=True)
    h = (x * lax.rsqrt(ms + EPS) * g_ref[...]).astype(BF16)
    kv = jnp.dot(h, w_ref[...], preferred_element_type=F32)
    n = x.shape[0]
    kt = [kv[:, LANE * j:LANE * (j + 1)] for j in range(2)]
    sq = jnp.concatenate([(t * t).astype(BF16) for t in kt], axis=0)
    ssum = jnp.dot(sq, bd_ref[...], preferred_element_type=F32)
    for j in range(2):
        inv = lax.rsqrt(ssum[j * n:(j + 1) * n] * (1.0 / HEAD_DIM) + EPS)
        k_ref[:, LANE * j:LANE * (j + 1)] = kt[j] * inv * gk_ref[...]
    v_ref[...] = kv[:, WIDTH_M:]


def _memkv(mem, w):
    n = mem.shape[0]
    return pl.pallas_call(
        _memkv_body,
        out_shape=[jax.ShapeDtypeStruct((n, WIDTH_M), F32)] * 2,
        compiler_params=pltpu.CompilerParams(vmem_limit_bytes=VMEM_LIMIT),
        name="memkv",
    )(mem, w["gmem"], w["w_mem_kv"], w["gkm"], w["bd"])


def _memattn_body(q_ref, k_ref, v_ref, o_ref):
    lane = lax.broadcasted_iota(jnp.int32, (q_ref.shape[0], LANE), 1)
    low = lane < HEAD_DIM
    for j in range(2):
        qp = q_ref[:, LANE * j:LANE * (j + 1)].astype(F32)
        kp = k_ref[0, :, LANE * j:LANE * (j + 1)].astype(BF16)
        vp = v_ref[0, :, LANE * j:LANE * (j + 1)].astype(BF16)
        outs = []
        for hm in (low, jnp.logical_not(low)):
            qh = jnp.where(hm, qp, 0.0).astype(BF16)
            s = lax.dot_general(qh, kp, (((1,), (1,)), ((), ())), preferred_element_type=F32)
            m = jnp.max(s, axis=-1, keepdims=True)
            e = jnp.exp(s - m)
            l = jnp.sum(e, axis=-1, keepdims=True)
            outs.append(jnp.dot(e.astype(BF16), vp, preferred_element_type=F32) / l)
        o_ref[:, LANE * j:LANE * (j + 1)] = jnp.where(low, outs[0], outs[1]).astype(o_ref.dtype)


def _memattn_prompt(qm, km, vm, batch, seq):
    tiles_per_b = seq // TM
    kv = pl.BlockSpec((1, N_MEM, WIDTH_M), lambda i: (i // tiles_per_b, 0, 0))
    return pl.pallas_call(
        _memattn_body,
        grid=(batch * tiles_per_b,),
        in_specs=[pl.BlockSpec((TM, WIDTH_M), lambda i: (i, 0)), kv, kv],
        out_specs=pl.BlockSpec((TM, WIDTH_M), lambda i: (i, 0)),
        out_shape=jax.ShapeDtypeStruct((batch * seq, WIDTH_M), BF16),
        compiler_params=_cparams(("arbitrary",)),
        name="memattn_prompt",
    )(qm, km, vm)


def _sample_attn_body(qbd_ref, kt_ref, vt_ref, kn_ref, vn_ref, cnt_ref, cntn_ref, hmask_ref,
                      qmbd_ref, kmt_ref, vmt_ref, hmaskm_ref, ob_ref, om_ref):
    dec = kn_ref.shape[1]
    qbd = qbd_ref[0]
    kt = kt_ref[0].astype(BF16)
    vt = vt_ref[0].astype(BF16)
    s = jnp.dot(qbd, kt, preferred_element_type=F32)
    qf = qbd.astype(F32)
    kn = kn_ref[0]
    vn = vn_ref[0]
    cnt = cnt_ref[...]
    cntn = cntn_ref[...]
    s_new = [jnp.sum(qf * kn[j:j + 1, :], axis=-1, keepdims=True) for j in range(dec)]
    m = jnp.max(jnp.where(cnt > 0, s, NEG), axis=-1, keepdims=True)
    for j in range(dec):
        m = jnp.maximum(m, jnp.where(cntn[:, j:j + 1] > 0, s_new[j], NEG))
    e = cnt * jnp.exp(jnp.where(cnt > 0, s - m, 0.0))
    l = jnp.sum(e, axis=-1, keepdims=True)
    acc = lax.dot_general(e.astype(BF16), vt, (((1,), (1,)), ((), ())), preferred_element_type=F32)
    for j in range(dec):
        w = cntn[:, j:j + 1]
        ej = w * jnp.exp(jnp.where(w > 0, s_new[j] - m, 0.0))
        l = l + ej
        acc = acc + ej * vn[j:j + 1, :]
    r = acc / l * hmask_ref[...]
    out = r[0:8]
    for h in range(1, WIDTH_B // HEAD_DIM):
        out = out + r[8 * h:8 * h + 8]
    ob_ref[0] = out
    qm = qmbd_ref[0]
    sm = jnp.dot(qm, kmt_ref[0].astype(BF16), preferred_element_type=F32)
    mm = jnp.max(sm, axis=-1, keepdims=True)
    em = jnp.exp(sm - mm)
    lm = jnp.sum(em, axis=-1, keepdims=True)
    am = lax.dot_general(em.astype(BF16), vmt_ref[0].astype(BF16), (((1,), (1,)), ((), ())),
                         preferred_element_type=F32)
    rm = am / lm * hmaskm_ref[...]
    outm = rm[0:8]
    for h in range(1, WIDTH_M // HEAD_DIM):
        outm = outm + rm[8 * h:8 * h + 8]
    om_ref[0] = outm


def _sample_counts(dec, w_buf):
    t = np.arange(8)[:, None]
    t = np.where(t < dec, t, 0)
    def mult(dist):
        c = np.zeros(dist.shape, np.float32)
        for window, dil in DILATIONS:
            c += ((dist >= 0) & (dist % dil == 0) & (dist <= window)).astype(np.float32)
        return c
    cache = mult(w_buf + t - np.arange(w_buf)[None, :])
    new = mult(t - np.arange(dec)[None, :])
    nb, nm = WIDTH_B // HEAD_DIM, WIDTH_M // HEAD_DIM
    hmask = (np.arange(8 * nb)[:, None] // 8 == np.arange(WIDTH_B)[None, :] // HEAD_DIM).astype(np.float32)
    hmaskm = (np.arange(8 * nm)[:, None] // 8 == np.arange(WIDTH_M)[None, :] // HEAD_DIM).astype(np.float32)
    return np.tile(cache, (nb, 1)), np.tile(new, (nb, 1)), hmask, hmaskm


def _block_diag_queries(q, dec, hmask):
    width = q.shape[-1]
    nh = width // HEAD_DIM
    qb = q.reshape(-1, 1, dec, width)
    qb = jnp.pad(qb, ((0, 0), (0, 0), (0, 8 - dec), (0, 0)))
    qb = jnp.broadcast_to(qb, (qb.shape[0], nh, 8, width)).reshape(-1, 8 * nh, width)
    return (qb * hmask[None]).astype(BF16)


def _sample_attn(q, kn, vn, qm, kt, vt, kmt, vmt, dec):
    bd = kt.shape[0]
    w_buf = kt.shape[-1]
    cnt, cntn, hmask, hmaskm = _sample_counts(dec, w_buf)
    qbd = _block_diag_queries(q, dec, hmask)
    qmbd = _block_diag_queries(qm.astype(F32), dec, hmaskm)
    nb8, nm8 = qbd.shape[1], qmbd.shape[1]
    per_b = lambda shape: pl.BlockSpec((1,) + shape, lambda b: (b,) + (0,) * len(shape))
    full = lambda shape: pl.BlockSpec(shape, lambda b: (0,) * len(shape))
    ob, om = pl.pallas_call(
        _sample_attn_body,
        grid=(bd,),
        in_specs=[per_b((nb8, WIDTH_B)), per_b((WIDTH_B, w_buf)), per_b((WIDTH_B, w_buf)),
                  per_b((dec, WIDTH_B)), per_b((dec, WIDTH_B)),
                  full((nb8, w_buf)), full((nb8, dec)), full((nb8, WIDTH_B)),
                  per_b((nm8, WIDTH_M)), per_b((WIDTH_M, N_MEM)), per_b((WIDTH_M, N_MEM)), full((nm8, WIDTH_M))],
        out_specs=[per_b((8, WIDTH_B)), per_b((8, WIDTH_M))],
        out_shape=[jax.ShapeDtypeStruct((bd, 8, WIDTH_B), F32), jax.ShapeDtypeStruct((bd, 8, WIDTH_M), F32)],
        compiler_params=_cparams(("arbitrary",)),
        name="sample_attn",
    )(qbd, kt, vt, kn.reshape(bd, dec, WIDTH_B), vn.reshape(bd, dec, WIDTH_B),
      jnp.asarray(cnt), jnp.asarray(cntn), jnp.asarray(hmask),
      qmbd, kmt, vmt, jnp.asarray(hmaskm))
    return (ob[:, :dec].reshape(bd * dec, WIDTH_B).astype(BF16),
            om[:, :dec].reshape(bd * dec, WIDTH_M).astype(BF16))


def _post_body(*refs, tc, aliased):
    (x_ref, u_ref, va_ref, ob_ref, om_ref, wg_ref, bg_ref, wout_ref, g2_ref, wr_ref, br_ref,
     tri_ref, upper_ref) = refs[:13]
    x1_ref, xs_ref, ld_ref, tg_ref, seg_ref = refs[13 + aliased:]
    tm = x_ref.shape[0]
    u = u_ref[...]
    if tc is None:
        vaf = va_ref[...]
        mixed = wg_ref[0] * vaf + bg_ref[...]
        for s in range(1, wg_ref.shape[0]):
            mixed = mixed + wg_ref[s] * pltpu.roll(vaf, s, 0)
        oa = u * mixed
    else:
        lane = lax.broadcasted_iota(jnp.int32, (tc, LANE), 1)
        low = lane < HEAD_DIM
        va = va_ref[...].astype(BF16)
        oa_rows = []
        for c in range(tm // tc):
            r0 = c * tc
            tiles = []
            for p in range(3):
                vp = va[r0:r0 + tc, LANE * p:LANE * (p + 1)]
                r = jnp.dot(wg_ref[p], vp, preferred_element_type=F32)
                tiles.append(jnp.where(low, r[:tc], r[tc:]))
            mixed = jnp.concatenate(tiles, axis=1) + bg_ref[...]
            oa_rows.append(u[r0:r0 + tc] * mixed)
        oa = jnp.concatenate(oa_rows, axis=0)
    mixed_all = jnp.concatenate([oa.astype(BF16), ob_ref[...], om_ref[...]], axis=1)
    x1 = x_ref[...] + jnp.dot(mixed_all, wout_ref[...], preferred_element_type=F32)
    x1_ref[...] = x1
    ms = jnp.mean(x1 * x1, axis=-1, keepdims=True)
    h2 = x1 * lax.rsqrt(ms + EPS) * g2_ref[...]
    h_hi = h2.astype(BF16)
    h_lo = (h2 - h_hi.astype(F32)).astype(BF16)
    logits = (jnp.dot(h_hi, wr_ref[0], preferred_element_type=F32)
              + jnp.dot(h_lo, wr_ref[0], preferred_element_type=F32)
              + jnp.dot(h_hi, wr_ref[1], preferred_element_type=F32)) + br_ref[...]
    lane_i = lax.broadcasted_iota(jnp.int32, (tm, LANE), 1)
    lane_r = lane_i.astype(F32)
    vals = logits
    tops, idxs = [], []
    for _ in range(TOP_K):
        mk = jnp.max(vals, axis=-1, keepdims=True)
        ik = jnp.min(jnp.where(vals == mk, lane_r, float(LANE)), axis=-1, keepdims=True)
        vals = jnp.where(lane_r == ik, -jnp.inf, vals)
        tops.append(mk)
        idxs.append(ik)
    es = [jnp.exp(t - tops[0]) for t in tops]
    den = es[0] + es[1] + es[2] + es[3]
    tg = jnp.zeros((tm, LANE), F32)
    for k in range(TOP_K):
        tg = jnp.where(lane_i == k, es[k] / den, tg)
    tg_ref[...] = tg
    col = lax.broadcasted_iota(jnp.int32, (MOE_TM, SEG_ROWS), 1).astype(F32)
    lane_t = lax.broadcasted_iota(jnp.int32, (MOE_TM, LANE), 1)
    hot_all = [lane_r == idxs[k] for k in range(TOP_K)]
    for hf in range(tm // MOE_TM):
        r0 = hf * MOE_TM
        hot = [h[r0:r0 + MOE_TM] for h in hot_all]
        sel = jnp.zeros((MOE_TM, LANE), F32)
        for k in range(TOP_K):
            sel = sel + jnp.where(hot[k], 1.0, 0.0)
        rank = jnp.dot(tri_ref[...], sel.astype(BF16), preferred_element_type=F32)
        length = jnp.sum(sel, axis=0, keepdims=True)
        plen = jnp.floor((length + (PIECE - 1)) * (1.0 / PIECE)) * PIECE
        loff = jnp.dot(jnp.broadcast_to(plen, (8, LANE)).astype(BF16), upper_ref[...],
                       preferred_element_type=F32)[0:1]
        base = loff + rank
        q = jnp.zeros((MOE_TM, SEG_ROWS), F32)
        ld = jnp.zeros((MOE_TM, LANE), F32)
        for k in range(TOP_K):
            ld_k = jnp.sum(jnp.where(hot[k], base, 0.0), axis=-1, keepdims=True)
            q = q + jnp.where(col == ld_k, 1.0, 0.0)
            ld = jnp.where(lane_t == k, ld_k, ld)
        xs_ref[hf * SEG_ROWS:(hf + 1) * SEG_ROWS, :] = lax.dot_general(
            q.astype(BF16), h_hi[r0:r0 + MOE_TM], (((0,), (0,)), ((), ())), preferred_element_type=F32)
        ld_ref[r0:r0 + MOE_TM, :] = ld.astype(jnp.int32)
        seg_ref[hf * 8:(hf + 1) * 8, :] = jnp.broadcast_to(length, (8, LANE)).astype(jnp.int32)


def _post(x, u, va, ob, om, wg, bg, w, tc, total_tiles, tile0=0, xs_all=None):
    t = x.shape[0]
    nt = t // TM
    full = lambda shape: pl.BlockSpec(shape, lambda i: (0,) * len(shape))
    row = lambda width: pl.BlockSpec((TM, width), lambda i: (i, 0))
    per = TM // MOE_TM
    ix = np.arange(MOE_TM)
    tri = jnp.asarray(ix[:, None] > ix[None, :], BF16)
    ex = np.arange(LANE)
    upper = jnp.asarray(ex[:, None] < ex[None, :], BF16)
    step0 = tile0 // per
    in_specs = [row(D_MODEL), row(WIDTH_A), row(WIDTH_A), row(WIDTH_B), row(WIDTH_M),
                full(wg.shape), full(bg.shape), full((D_MODEL, D_MODEL)), full((1, D_MODEL)),
                full((2, D_MODEL, LANE)), full((1, LANE)), full((MOE_TM, MOE_TM)), full((LANE, LANE))]
    args = [x, u, va, ob, om, wg, bg, w["w_out"], w["g2"], w["w_router"], w["b_router"], tri, upper]
    aliases = {}
    if xs_all is not None:
        in_specs.append(pl.BlockSpec(memory_space=pl.ANY))
        args.append(xs_all)
        aliases = {len(args) - 1: 1}
    return pl.pallas_call(
        functools.partial(_post_body, tc=tc, aliased=int(xs_all is not None)),
        grid=(nt,),
        in_specs=in_specs,
        out_specs=[row(D_MODEL), pl.BlockSpec((per * SEG_ROWS, D_MODEL), lambda i: (i + step0, 0)),
                   row(LANE), row(LANE), pl.BlockSpec((per * 8, LANE), lambda i: (i, 0))],
        out_shape=[jax.ShapeDtypeStruct((t, D_MODEL), F32),
                   jax.ShapeDtypeStruct((total_tiles * SEG_ROWS, D_MODEL), F32),
                   jax.ShapeDtypeStruct((t, LANE), jnp.int32), jax.ShapeDtypeStruct((t, LANE), F32),
                   jax.ShapeDtypeStruct((t // MOE_TM * 8, LANE), jnp.int32)],
        input_output_aliases=aliases,
        compiler_params=_cparams(("arbitrary",)),
        name="post",
    )(*args)


def _experts_body(te_ref, valid_ref, first_ref, next_ref, src_ref, dst_ref,
                  xs_hbm, wgu_hbm, bgu_ref, wd_hbm, bd_ref, ys_hbm,
                  xbuf, ybuf, wgu_f, wd_f, wgu_s, wd_s, gsem, ssem, wsem):
    t = pl.program_id(0)
    nt = pl.num_programs(0)
    slot = t % 2

    def gather(tile, sl):
        for i in range(NPIECE):
            s = src_ref[tile * NPIECE + i]
            pltpu.make_async_copy(xs_hbm.at[pl.ds(pl.multiple_of(s * PIECE, PIECE), PIECE), :],
                                  xbuf.at[sl, pl.ds(i * PIECE, PIECE), :], gsem.at[sl]).start()

    def scatter(tile, sl):
        for i in range(NPIECE):
            d = dst_ref[tile * NPIECE + i]
            pltpu.make_async_copy(ybuf.at[sl, pl.ds(i * PIECE, PIECE), :],
                                  ys_hbm.at[pl.ds(pl.multiple_of(d * PIECE, PIECE), PIECE), :], ssem.at[sl]).start()

    def wait_tile(hbm, buf, sem, sl):
        pltpu.make_async_copy(hbm.at[pl.ds(0, GM_TM), :], buf.at[sl], sem.at[sl]).wait()

    def weight_copies(e):
        return (pltpu.make_async_copy(wgu_hbm.at[e], wgu_f, wsem.at[0]),
                pltpu.make_async_copy(wd_hbm.at[e], wd_f, wsem.at[1]))

    @pl.when(t == 0)
    def _():
        for c in weight_copies(te_ref[0]):
            c.start()
        gather(0, 0)

    @pl.when(valid_ref[t] > 0)
    def _():
        nxt = jnp.minimum(t + 1, nt - 1)
        has_next = jnp.logical_and(t + 1 < nt, valid_ref[nxt] > 0)

        @pl.when(first_ref[t] > 0)
        def _():
            for c in weight_copies(te_ref[t]):
                c.wait()
            wgu_s[...] = wgu_f[...].astype(BF16)
            wd_s[...] = wd_f[...].astype(BF16)

            @pl.when(next_ref[t] >= 0)
            def _():
                for c in weight_copies(next_ref[t]):
                    c.start()

        @pl.when(t >= 2)
        def _():
            wait_tile(ys_hbm, ybuf, ssem, slot)

        gather(jnp.where(has_next, t + 1, t), 1 - slot)
        wait_tile(xs_hbm, xbuf, gsem, slot)
        x = xbuf[slot].astype(BF16)
        gu = jnp.dot(x, wgu_s[...], preferred_element_type=F32) + bgu_ref[0]
        gate = jnp.minimum(gu[:, :D_MODEL], SWIGLU_LIMIT)
        up = jnp.clip(gu[:, D_MODEL:], -SWIGLU_LIMIT, SWIGLU_LIMIT)
        act = (up + 1.0) * (gate * (1.0 / (1.0 + jnp.exp(-SWIGLU_ALPHA * gate))))
        ybuf[slot] = jnp.dot(act.astype(BF16), wd_s[...], preferred_element_type=F32) + bd_ref[0]
        scatter(t, slot)

        @pl.when(jnp.logical_not(has_next))
        def _():
            wait_tile(xs_hbm, xbuf, gsem, 1 - slot)
            wait_tile(ys_hbm, ybuf, ssem, slot)

            @pl.when(t >= 1)
            def _():
                wait_tile(ys_hbm, ybuf, ssem, 1 - slot)


def _experts(plan, xs, w, ys_rows):
    n_tiles = plan["tile_expert"].shape[0]
    by_expert = lambda shape: pl.BlockSpec((1,) + shape, lambda t, te, *_: (te[t],) + (0,) * len(shape))
    hbm = pl.BlockSpec(memory_space=pl.ANY)
    return pl.pallas_call(
        _experts_body,
        grid_spec=pltpu.PrefetchScalarGridSpec(
            num_scalar_prefetch=6, grid=(n_tiles,),
            in_specs=[hbm, hbm, by_expert((1, 2 * D_MODEL)), hbm, by_expert((1, D_MODEL))],
            out_specs=hbm,
            scratch_shapes=[pltpu.VMEM((2, GM_TM, D_MODEL), F32), pltpu.VMEM((2, GM_TM, D_MODEL), F32),
                            pltpu.VMEM((D_MODEL, 2 * D_MODEL), F32), pltpu.VMEM((D_MODEL, D_MODEL), F32),
                            pltpu.VMEM((D_MODEL, 2 * D_MODEL), BF16), pltpu.VMEM((D_MODEL, D_MODEL), BF16),
                            pltpu.SemaphoreType.DMA((2,)), pltpu.SemaphoreType.DMA((2,)),
                            pltpu.SemaphoreType.DMA((2,))]),
        out_shape=jax.ShapeDtypeStruct((ys_rows, D_MODEL), F32),
        compiler_params=_cparams(("arbitrary",)),
        name="moe_experts",
    )(plan["tile_expert"], plan["tile_valid"], plan["tile_first"], plan["tile_next"], plan["src"], plan["dst"],
      xs, w["w_gate_up"], w["b_gate_up"], w["w_down"], w["b_down"])


def _combine_body(used_ref, ld_ref, g_ref, x1_ref, ys_ref, y_ref, *, tile0):
    t = pl.program_id(0)
    row = lax.broadcasted_iota(jnp.int32, (SEG_ROWS, 1), 0)
    ys = jnp.where(row < used_ref[tile0 + t], ys_ref[...], 0.0).astype(BF16)
    col = lax.broadcasted_iota(jnp.int32, (MOE_TM, SEG_ROWS), 1)
    ld = ld_ref[...]
    g = g_ref[...]
    p = jnp.zeros((MOE_TM, SEG_ROWS), F32)
    for k in range(TOP_K):
        p = jnp.where(col == ld[:, k:k + 1], g[:, k:k + 1], p)
    y_ref[...] = x1_ref[...] + jnp.dot(p.astype(BF16), ys, preferred_element_type=F32)


def _combine(used, ld, gates, x1, ys, tile0):
    n = x1.shape[0]
    nt = n // MOE_TM
    rows = lambda width: pl.BlockSpec((MOE_TM, width), lambda t, *_: (t, 0))
    return pl.pallas_call(
        functools.partial(_combine_body, tile0=tile0),
        grid_spec=pltpu.PrefetchScalarGridSpec(
            num_scalar_prefetch=1, grid=(nt,),
            in_specs=[rows(LANE), rows(LANE), rows(D_MODEL),
                      pl.BlockSpec((SEG_ROWS, D_MODEL), lambda t, *_: (t + tile0, 0))],
            out_specs=rows(D_MODEL)),
        out_shape=jax.ShapeDtypeStruct((n, D_MODEL), F32),
        compiler_params=_cparams(("arbitrary",)),
        name="moe_combine",
    )(used, ld, gates, x1, ys)


def _moe(groups, xs, seglen, w):
    i32 = jnp.int32
    nt = seglen.shape[0]
    plen = (seglen + PIECE - 1) // PIECE * PIECE
    loff = jnp.cumsum(plen, axis=1) - plen
    used = jnp.sum(plen, axis=1).astype(i32)
    pp = plen // PIECE
    cp_end = jnp.cumsum(pp, axis=0)
    cp = cp_end - pp
    cnt_e = cp_end[-1]
    tiles_e = (cnt_e + NPIECE - 1) // NPIECE
    tile_end = jnp.cumsum(tiles_e)
    tile_start = tile_end - tiles_e
    n_tiles = (nt * SEG_ROWS + GM_TM - 1) // GM_TM + N_EXPERTS
    tix = jnp.arange(n_tiles, dtype=i32)
    total_tiles = tile_end[-1]
    tile_valid = (tix < total_tiles).astype(i32)
    expert_at = lambda tile: jnp.minimum((tile[:, None] >= tile_end[None, :]).astype(i32).sum(axis=1), N_EXPERTS - 1)
    last_expert = expert_at(jnp.maximum(total_tiles - 1, 0)[None])[0]
    tile_expert = jnp.where(tile_valid > 0, expert_at(tix), last_expert)
    hot_e = tile_expert[:, None] == jnp.arange(N_EXPERTS, dtype=i32)[None, :]
    per_tile = lambda v: jnp.sum(jnp.where(hot_e, v[None, :], 0), axis=1)
    per_tile_rows = lambda m: jnp.sum(jnp.where(hot_e[:, :, None], jnp.transpose(m)[None], 0), axis=1)
    start_t = per_tile(tile_start)
    tile_first = ((tix == start_t) & (tile_valid > 0)).astype(i32)
    following = per_tile(tile_end)
    tile_next = jnp.where(following < total_tiles, expert_at(following), -1)
    j = (tix - start_t)[:, None] * NPIECE + jnp.arange(NPIECE, dtype=i32)[None, :]
    ok = (tile_valid[:, None] > 0) & (j < per_tile(cnt_e)[:, None])
    ends_t, cp_t, loff_t = per_tile_rows(cp_end), per_tile_rows(cp), per_tile_rows(loff)
    t_q = jnp.minimum((ends_t[:, None, :] <= j[:, :, None]).astype(i32).sum(axis=2), nt - 1)
    hot_t = t_q[:, :, None] == jnp.arange(nt, dtype=i32)[None, None, :]
    at_t = lambda m: jnp.sum(jnp.where(hot_t, m[:, None, :], 0), axis=2)
    piece = t_q * SEG_PIECES + at_t(loff_t) // PIECE + j - at_t(cp_t)
    src = jnp.where(ok, piece, 0)
    dump = nt * SEG_PIECES + (tix % 2)[:, None] * NPIECE + jnp.arange(NPIECE, dtype=i32)[None, :]
    dst = jnp.where(ok, piece, dump)
    plan = dict(tile_expert=tile_expert.astype(i32), tile_valid=tile_valid, tile_first=tile_first,
                tile_next=tile_next.astype(i32), src=src.reshape(-1).astype(i32), dst=dst.reshape(-1).astype(i32))

    ys = _experts(plan, xs, w, nt * SEG_ROWS + 2 * GM_TM)
    outs, r0 = [], 0
    for x1, ld, tg in groups:
        outs.append(_combine(used, ld, tg, x1, ys, r0 // MOE_TM))
        r0 += x1.shape[0]
    return outs


def _pair_major_to_rows(a3):
    return jnp.transpose(a3, (1, 0, 2)).reshape(a3.shape[1], 3 * LANE)


def kernel(x_prompt, x_sample, mem_prompt, cache_win_k, cache_win_v, cache_mem_k, cache_mem_v, norm1_g, w_in, gv_a, w_s, b_s, gq_b, gk_b, gq_m, gk_m, mem_norm_g, w_mem_kv, w_out, norm2_g, w_router, b_router, w_gate_up, b_gate_up, w_down, b_down):
    batch, seq, _ = x_prompt.shape
    bd, dec, _ = x_sample.shape
    depth = norm1_g.shape[0]
    assert depth == 1 and seq % SPAN == 0 and (bd * dec) % TM == 0 and PAST_LEN % CHUNK == 0
    w_buf = cache_win_k.shape[2]
    assert w_buf == MAX_WINDOW and dec <= 8
    l = 0
    two = lambda g: jnp.concatenate([g, g])[None, :]
    head = np.arange(LANE) // HEAD_DIM
    wr = jnp.pad(w_router[l], ((0, 0), (0, LANE - N_EXPERTS)))
    wr_hi = wr.astype(BF16)
    wr_lo = (wr - wr_hi.astype(F32)).astype(BF16)
    w = dict(
        g1=norm1_g[l][None], w_in=w_in[l].astype(BF16), gva=gv_a[l][None],
        gq=two(gq_b[l]), gk=two(gk_b[l]), gqm=two(gq_m[l]), gkm=two(gk_m[l]),
        bd=jnp.asarray(head[:, None] == head[None, :], BF16),
        gmem=mem_norm_g[l][None], w_mem_kv=w_mem_kv[l].astype(BF16),
        w_out=w_out[l].astype(BF16), g2=norm2_g[l][None],
        w_router=jnp.stack([wr_hi, wr_lo]),
        b_router=jnp.pad(b_router[l], (0, LANE - N_EXPERTS), constant_values=-jnp.inf)[None],
        w_gate_up=w_gate_up[l], b_gate_up=b_gate_up[l][:, None, :], w_down=w_down[l], b_down=b_down[l][:, None, :],
    )
    ngrp = WIDTH_A // HEAD_DIM
    wtri = jnp.where(jnp.tril(jnp.ones((CHUNK, CHUNK), bool)), w_s[l], 0).astype(BF16)
    wg_p = wtri.reshape(ngrp // 2, 2 * CHUNK, CHUNK)
    bg_p = jnp.repeat(jnp.transpose(b_s[l]), HEAD_DIM, axis=1)
    zero = jnp.zeros((ngrp,), F32)
    lanes = lambda tg_: jnp.tile(jnp.repeat(tg_, HEAD_DIM, axis=1), (bd, 1))
    wg_s = jnp.stack([lanes(jnp.stack([w_s[l][:, t, t - s] if t >= s else zero for t in range(dec)]))
                      for s in range(dec)])
    bg_s = lanes(jnp.transpose(b_s[l][:, :dec]))

    xp = x_prompt.reshape(batch * seq, D_MODEL)
    tabs_p = _rope_tables(np.arange(seq))
    u_p, va_p, q3_p, k3_p, v3_p, qm_p = _premix(xp, tabs_p, seq // TM, w)
    ob_p = _attn_prompt(q3_p, k3_p, v3_p, batch, seq)
    km, vm = _memkv(mem_prompt.reshape(batch * N_MEM, D_MODEL), w)
    om_p = _memattn_prompt(qm_p, km.reshape(batch, N_MEM, WIDTH_M), vm.reshape(batch, N_MEM, WIDTH_M), batch, seq)
    tiles_p = batch * seq // MOE_TM
    tiles_all = tiles_p + bd * dec // MOE_TM
    x1_p, xs_all, ld_p, tg_p, seg_p = _post(xp, u_p, va_p, ob_p, om_p, wg_p, bg_p, w, CHUNK, tiles_all)

    xs = x_sample.reshape(bd * dec, D_MODEL)
    tabs_s = _rope_tables(np.tile(PAST_LEN + np.arange(dec), bd))
    u_s, va_s, q3_s, k3_s, v3_s, qm_s = _premix(xs, tabs_s, 1, w)
    q_s, k_s, v_s = (_pair_major_to_rows(a) for a in (q3_s, k3_s, v3_s))
    nb = WIDTH_B // HEAD_DIM
    nm = WIDTH_M // HEAD_DIM
    kt = jnp.transpose(cache_win_k[l], (0, 2, 3, 1)).reshape(bd, WIDTH_B, w_buf)
    vt = jnp.transpose(cache_win_v[l], (0, 2, 3, 1)).reshape(bd, WIDTH_B, w_buf)
    kmt = jnp.transpose(cache_mem_k[l], (0, 2, 3, 1)).reshape(bd, WIDTH_M, N_MEM)
    vmt = jnp.transpose(cache_mem_v[l], (0, 2, 3, 1)).reshape(bd, WIDTH_M, N_MEM)
    ob_s, om_s = _sample_attn(q_s, k_s, v_s, qm_s, kt, vt, kmt, vmt, dec)
    x1_s, xs_all, ld_s, tg_s, seg_s = _post(xs, u_s, va_s, ob_s, om_s, wg_s, bg_s, w, None, tiles_all,
                                             tile0=tiles_p, xs_all=xs_all)

    seglen = jnp.concatenate([seg_p, seg_s])[::8, :N_EXPERTS]
    y_p, y_s = _moe([(x1_p, ld_p, tg_p), (x1_s, ld_s, tg_s)], xs_all, seglen, w)
    y_prompt = y_p.reshape(batch, seq, D_MODEL)
    y_sample = y_s.reshape(bd, dec, D_MODEL)

    n_keep = min(MAX_WINDOW, seq)

    def window_rows(a3):
        a = a3.reshape(3, batch, seq, LANE)[:, :, seq - n_keep:]
        return jnp.transpose(a, (1, 2, 0, 3)).reshape(1, batch, n_keep, nb, HEAD_DIM)

    return (y_prompt, y_sample,
            window_rows(k3_p), window_rows(v3_p),
            km.reshape(1, batch, N_MEM, nm, HEAD_DIM), vm.reshape(1, batch, N_MEM, nm, HEAD_DIM),
            k_s.reshape(1, bd, dec, nb, HEAD_DIM), v_s.reshape(1, bd, dec, nb, HEAD_DIM),
            va_s.reshape(1, bd, dec, WIDTH_A))
```

```python
import functools

import numpy as np
import jax
import jax.numpy as jnp
from jax import lax
from jax.experimental import pallas as pl
from jax.experimental.pallas import tpu as pltpu

F32 = jnp.float32
BF16 = jnp.bfloat16

D_MODEL = 1024
HEAD_DIM = 64
WIDTH_A = 384
WIDTH_B = 384
WIDTH_M = 256
IN_WIDTH = 2 * WIDTH_A + 3 * WIDTH_B + WIDTH_M
CHUNK = 128
DILATIONS = ((128, 1), (512, 4), (2048, 16))
N_SUB = 128
MAX_WINDOW = 2048
N_MEM = 256
ROPE_THETA = 500000.0
ROT_HALF = 8
SCALE = HEAD_DIM ** -0.5
N_EXPERTS = 32
TOP_K = 4
SWIGLU_LIMIT = 7.0
SWIGLU_ALPHA = 1.702
EPS = 1e-6
PAST_LEN = 8192

LANE = 128
NEG = -1e30
TM = 512
SPAN = 2048
QB = 128
ATTN_UNROLL = 16
SAMPLE_SEQS = 2
MOE_TM = 256
GM_TM = 512
PIECE = 8
NPIECE = GM_TM // PIECE
SEG_ROWS = TOP_K * MOE_TM + N_EXPERTS * PIECE
SEG_PIECES = SEG_ROWS // PIECE
VMEM_LIMIT = 52 * 1024 * 1024


def _cparams(sem):
    return pltpu.CompilerParams(dimension_semantics=sem, vmem_limit_bytes=VMEM_LIMIT)


def _premix_body(x_ref, g1_ref, win_ref, gva_ref, gq_ref, gk_ref, gqm_ref, bd_ref,
                 rc_ref, rs1_ref, rs2_ref,
                 u_ref, va_ref, q3_ref, k3_ref, v3_ref, qm_ref):
    x = x_ref[...]
    tm = x.shape[0]
    ms = jnp.mean(x * x, axis=-1, keepdims=True)
    h = (x * lax.rsqrt(ms + EPS) * g1_ref[...]).astype(BF16)
    z = jnp.dot(h, win_ref[...], preferred_element_type=F32)
    u_ref[...] = z[:, :WIDTH_A]
    va = z[:, WIDTH_A:2 * WIDTH_A]
    va_ms = jnp.mean(va * va, axis=-1, keepdims=True)
    va_ref[...] = va * lax.rsqrt(va_ms + EPS) * gva_ref[...]
    q0, k0, v0, m0 = 2 * WIDTH_A, 2 * WIDTH_A + WIDTH_B, 2 * WIDTH_A + 2 * WIDTH_B, 2 * WIDTH_A + 3 * WIDTH_B
    tiles = ([z[:, q0 + LANE * j:q0 + LANE * (j + 1)] for j in range(3)]
             + [z[:, k0 + LANE * j:k0 + LANE * (j + 1)] for j in range(3)]
             + [z[:, m0 + LANE * j:m0 + LANE * (j + 1)] for j in range(2)])
    sq = jnp.concatenate([(t * t).astype(BF16) for t in tiles], axis=0)
    ssum = jnp.dot(sq, bd_ref[...], preferred_element_type=F32)
    inv = [lax.rsqrt(ssum[i * tm:(i + 1) * tm] * (1.0 / HEAD_DIM) + EPS) for i in range(8)]
    rc, rs1, rs2 = rc_ref[...], rs1_ref[...], rs2_ref[...]

    def rope(t):
        return t * rc + pltpu.roll(t, LANE - ROT_HALF, 1) * rs1 + pltpu.roll(t, ROT_HALF, 1) * rs2

    for j in range(3):
        q3_ref[j] = rope(tiles[j] * inv[j] * gq_ref[...]) * SCALE
        k3_ref[j] = rope(tiles[3 + j] * inv[3 + j] * gk_ref[...])
        v3_ref[j] = z[:, v0 + LANE * j:v0 + LANE * (j + 1)]
    for j in range(2):
        qm_ref[:, LANE * j:LANE * (j + 1)] = (tiles[6 + j] * inv[6 + j] * gqm_ref[...] * SCALE).astype(BF16)


def _premix(x, tabs, n_tab_tiles, w):
    t = x.shape[0]
    nt = t // TM
    full = lambda shape: pl.BlockSpec(shape, lambda i: (0,) * len(shape))
    tab = pl.BlockSpec((TM, LANE), lambda i: (i % n_tab_tiles, 0))
    row = lambda width: pl.BlockSpec((TM, width), lambda i: (i, 0))
    pair = pl.BlockSpec((3, TM, LANE), lambda i: (0, i, 0))
    return pl.pallas_call(
        _premix_body,
        grid=(nt,),
        in_specs=[row(D_MODEL), full((1, D_MODEL)), full((D_MODEL, IN_WIDTH)), full((1, WIDTH_A)),
                  full((1, LANE)), full((1, LANE)), full((1, LANE)), full((LANE, LANE)), tab, tab, tab],
        out_specs=[row(WIDTH_A), row(WIDTH_A), pair, pair, pair, row(WIDTH_M)],
        out_shape=[jax.ShapeDtypeStruct((t, WIDTH_A), F32), jax.ShapeDtypeStruct((t, WIDTH_A), F32),
                   jax.ShapeDtypeStruct((3, t, LANE), F32), jax.ShapeDtypeStruct((3, t, LANE), F32),
                   jax.ShapeDtypeStruct((3, t, LANE), F32), jax.ShapeDtypeStruct((t, WIDTH_M), BF16)],
        compiler_params=_cparams(("arbitrary",)),
        name="premix",
    )(x, w["g1"], w["w_in"], w["gva"], w["gq"], w["gk"], w["gqm"], w["bd"], *tabs)


def _rope_tables(pos):
    pos = np.asarray(pos, np.float64)
    inv_freq = np.power(ROPE_THETA, -np.arange(ROT_HALF, dtype=np.float64) / ROT_HALF)
    ang = pos[:, None] * inv_freq[None, :]
    cos, sin = np.cos(ang), np.sin(ang)
    t = pos.shape[0]
    rest = HEAD_DIM - 2 * ROT_HALF
    c = np.concatenate([cos, cos, np.ones((t, rest))], axis=1)
    s1 = np.concatenate([-sin, np.zeros((t, HEAD_DIM - ROT_HALF))], axis=1)
    s2 = np.concatenate([np.zeros((t, ROT_HALF)), sin, np.zeros((t, rest))], axis=1)
    two = lambda a: jnp.asarray(np.concatenate([a, a], axis=1), F32)
    return two(c), two(s1), two(s2)


def _attn_body(q_ref, kc_ref, kp_ref, vc_ref, vp_ref, o_ref, kk, vv, m_s, l_s, a_s):
    span_idx = pl.program_id(1)
    p0 = span_idx * SPAN
    lane = lax.broadcasted_iota(jnp.int32, (QB, LANE), 1)
    low = lane < HEAD_DIM
    qi = lax.broadcasted_iota(jnp.int32, (QB, 2 * QB), 0)
    kj = lax.broadcasted_iota(jnp.int32, (QB, 2 * QB), 1)
    band = (kj >= qi) & (kj <= qi + N_SUB)
    kk[0:SPAN, :] = kp_ref[0]
    kk[SPAN:2 * SPAN, :] = kc_ref[0]
    vv[0:SPAN, :] = vp_ref[0]
    vv[SPAN:2 * SPAN, :] = vc_ref[0]

    def rows(ref, start, n, d):
        if d == 1:
            return ref[pl.ds(pl.multiple_of(start, QB), n), :]
        return ref[pl.ds(start, n, stride=d), :]

    def unit(d, qstart, first):
        first_key = jnp.maximum((QB * d - p0 - qstart + d - 1) // d, 0)
        mask = band & (kj >= first_key)
        qb = rows(q_ref.at[0], qstart, QB, d)
        kb = rows(kk, SPAN + qstart - QB * d, 2 * QB, d).astype(BF16)
        vb = rows(vv, SPAN + qstart - QB * d, 2 * QB, d).astype(BF16)
        stats = []
        for hm in (low, jnp.logical_not(low)):
            qh = jnp.where(hm, qb, 0.0).astype(BF16)
            s = lax.dot_general(qh, kb, (((1,), (1,)), ((), ())), preferred_element_type=F32)
            s = jnp.where(mask, s, NEG)
            m = jnp.max(s, axis=-1, keepdims=True)
            e = jnp.exp(s - m)
            l = jnp.sum(e, axis=-1, keepdims=True)
            acc = jnp.dot(e.astype(BF16), vb, preferred_element_type=F32)
            stats.append((m, l, acc))
        m_new = jnp.where(low, stats[0][0], stats[1][0])
        l_new = jnp.where(low, stats[0][1], stats[1][1])
        a_new = jnp.where(low, stats[0][2], stats[1][2])
        if d == 1:
            sl = (pl.ds(pl.multiple_of(qstart, QB), QB), slice(None))
        else:
            sl = (pl.ds(qstart, QB, stride=d), slice(None))
        if first:
            m_s[sl] = m_new
            l_s[sl] = l_new
            a_s[sl] = a_new
        else:
            m_old, l_old, a_old = m_s[sl], l_s[sl], a_s[sl]
            m_t = jnp.maximum(m_old, m_new)
            wa = jnp.exp(m_old - m_t)
            wb = jnp.exp(m_new - m_t)
            m_s[sl] = m_t
            l_s[sl] = wa * l_old + wb * l_new
            a_s[sl] = wa * a_old + wb * a_new

    nblk = SPAN // QB
    first = True
    for _, d in DILATIONS:
        per_res = nblk // d
        def group(gi, c, d=d, per_res=per_res, first=first):
            for uu in range(ATTN_UNROLL):
                u = gi * ATTN_UNROLL + uu
                unit(d, u // per_res + d * QB * (u % per_res), first)
            return c
        lax.fori_loop(0, nblk // ATTN_UNROLL, group, 0)
        first = False
    o_ref[...] = (a_s[...] / l_s[...]).astype(o_ref.dtype)


def _attn_prompt(q3, k3, v3, batch, seq):
    nspan = seq // SPAN
    cur = pl.BlockSpec((1, SPAN, LANE), lambda b, s, p: (p, b * nspan + s, 0))
    prv = pl.BlockSpec((1, SPAN, LANE), lambda b, s, p: (p, b * nspan + jnp.maximum(s - 1, 0), 0))
    return pl.pallas_call(
        _attn_body,
        grid=(batch, nspan, 3),
        in_specs=[cur, cur, prv, cur, prv],
        out_specs=pl.BlockSpec((SPAN, LANE), lambda b, s, p: (b * nspan + s, p)),
        out_shape=jax.ShapeDtypeStruct((batch * seq, WIDTH_B), BF16),
        scratch_shapes=[pltpu.VMEM((2 * SPAN, LANE), F32)] * 2 + [pltpu.VMEM((SPAN, LANE), F32)] * 3,
        compiler_params=_cparams(("arbitrary", "arbitrary", "arbitrary")),
        name="attn_prompt",
    )(q3, k3, k3, v3, v3)


def _memkv_body(mem_ref, g_ref, w_ref, gk_ref, bd_ref, k_ref, v_ref):
    x = mem_ref[...]
    ms = jnp.mean(x * x, axis=-1, keepdims=True)
    h = (x * lax.rsqrt(ms + EPS) * g_ref[...]).astype(BF16)
    kv = jnp.dot(h, w_ref[...], preferred_element_type=F32)
    n = x.shape[0]
    kt = [kv[:, LANE * j:LANE * (j + 1)] for j in range(2)]
    sq = jnp.concatenate([(t * t).astype(BF16) for t in kt], axis=0)
    ssum = jnp.dot(sq, bd_ref[...], preferred_element_type=F32)
    for j in range(2):
        inv = lax.rsqrt(ssum[j * n:(j + 1) * n] * (1.0 / HEAD_DIM) + EPS)
        k_ref[:, LANE * j:LANE * (j + 1)] = kt[j] * inv * gk_ref[...]
    v_ref[...] = kv[:, WIDTH_M:]


def _memkv(mem, w):
    n = mem.shape[0]
    return pl.pallas_call(
        _memkv_body,
        out_shape=[jax.ShapeDtypeStruct((n, WIDTH_M), F32)] * 2,
        compiler_params=pltpu.CompilerParams(vmem_limit_bytes=VMEM_LIMIT),
        name="memkv",
    )(mem, w["gmem"], w["w_mem_kv"], w["gkm"], w["bd"])


def _memattn_body(q_ref, k_ref, v_ref, o_ref):
    lane = lax.broadcasted_iota(jnp.int32, (q_ref.shape[0], LANE), 1)
    low = lane < HEAD_DIM
    for j in range(2):
        qp = q_ref[:, LANE * j:LANE * (j + 1)].astype(F32)
        kp = k_ref[0, :, LANE * j:LANE * (j + 1)].astype(BF16)
        vp = v_ref[0, :, LANE * j:LANE * (j + 1)].astype(BF16)
        outs = []
        for hm in (low, jnp.logical_not(low)):
            qh = jnp.where(hm, qp, 0.0).astype(BF16)
            s = lax.dot_general(qh, kp, (((1,), (1,)), ((), ())), preferred_element_type=F32)
            m = jnp.max(s, axis=-1, keepdims=True)
            e = jnp.exp(s - m)
            l = jnp.sum(e, axis=-1, keepdims=True)
            outs.append(jnp.dot(e.astype(BF16), vp, preferred_element_type=F32) / l)
        o_ref[:, LANE * j:LANE * (j + 1)] = jnp.where(low, outs[0], outs[1]).astype(o_ref.dtype)


def _memattn_prompt(qm, km, vm, batch, seq):
    tiles_per_b = seq // TM
    kv = pl.BlockSpec((1, N_MEM, WIDTH_M), lambda i: (i // tiles_per_b, 0, 0))
    return pl.pallas_call(
        _memattn_body,
        grid=(batch * tiles_per_b,),
        in_specs=[pl.BlockSpec((TM, WIDTH_M), lambda i: (i, 0)), kv, kv],
        out_specs=pl.BlockSpec((TM, WIDTH_M), lambda i: (i, 0)),
        out_shape=jax.ShapeDtypeStruct((batch * seq, WIDTH_M), BF16),
        compiler_params=_cparams(("arbitrary",)),
        name="memattn_prompt",
    )(qm, km, vm)


def _sample_attn_body(*refs):
    for i in range(refs[0].shape[0]):
        _sample_attn_one(i, *refs)


def _sample_attn_one(i, qbd_ref, kt_ref, vt_ref, kn_ref, vn_ref, cnt_ref, cntn_ref, hmask_ref,
                     qmbd_ref, kmt_ref, vmt_ref, hmaskm_ref, ob_ref, om_ref):
    dec = kn_ref.shape[1]
    qbd = qbd_ref[i]
    kt = kt_ref[i].astype(BF16)
    vt = vt_ref[i].astype(BF16)
    s = jnp.dot(qbd, kt, preferred_element_type=F32)
    qf = qbd.astype(F32)
    kn = kn_ref[i]
    vn = vn_ref[i]
    cnt = cnt_ref[...]
    cntn = cntn_ref[...]
    s_new = [jnp.sum(qf * kn[j:j + 1, :], axis=-1, keepdims=True) for j in range(dec)]
    m = jnp.max(jnp.where(cnt > 0, s, NEG), axis=-1, keepdims=True)
    for j in range(dec):
        m = jnp.maximum(m, jnp.where(cntn[:, j:j + 1] > 0, s_new[j], NEG))
    e = cnt * jnp.exp(jnp.where(cnt > 0, s - m, 0.0))
    l = jnp.sum(e, axis=-1, keepdims=True)
    acc = lax.dot_general(e.astype(BF16), vt, (((1,), (1,)), ((), ())), preferred_element_type=F32)
    for j in range(dec):
        w = cntn[:, j:j + 1]
        ej = w * jnp.exp(jnp.where(w > 0, s_new[j] - m, 0.0))
        l = l + ej
        acc = acc + ej * vn[j:j + 1, :]
    r = acc / l * hmask_ref[...]
    out = r[0:8]
    for h in range(1, WIDTH_B // HEAD_DIM):
        out = out + r[8 * h:8 * h + 8]
    ob_ref[i] = out
    qm = qmbd_ref[i]
    sm = jnp.dot(qm, kmt_ref[i].astype(BF16), preferred_element_type=F32)
    mm = jnp.max(sm, axis=-1, keepdims=True)
    em = jnp.exp(sm - mm)
    lm = jnp.sum(em, axis=-1, keepdims=True)
    am = lax.dot_general(em.astype(BF16), vmt_ref[i].astype(BF16), (((1,), (1,)), ((), ())),
                         preferred_element_type=F32)
    rm = am / lm * hmaskm_ref[...]
    outm = rm[0:8]
    for h in range(1, WIDTH_M // HEAD_DIM):
        outm = outm + rm[8 * h:8 * h + 8]
    om_ref[i] = outm


def _sample_counts(dec, w_buf):
    t = np.arange(8)[:, None]
    t = np.where(t < dec, t, 0)
    def mult(dist):
        c = np.zeros(dist.shape, np.float32)
        for window, dil in DILATIONS:
            c += ((dist >= 0) & (dist % dil == 0) & (dist <= window)).astype(np.float32)
        return c
    cache = mult(w_buf + t - np.arange(w_buf)[None, :])
    new = mult(t - np.arange(dec)[None, :])
    nb, nm = WIDTH_B // HEAD_DIM, WIDTH_M // HEAD_DIM
    hmask = (np.arange(8 * nb)[:, None] // 8 == np.arange(WIDTH_B)[None, :] // HEAD_DIM).astype(np.float32)
    hmaskm = (np.arange(8 * nm)[:, None] // 8 == np.arange(WIDTH_M)[None, :] // HEAD_DIM).astype(np.float32)
    return np.tile(cache, (nb, 1)), np.tile(new, (nb, 1)), hmask, hmaskm


def _block_diag_queries(q, dec, hmask):
    width = q.shape[-1]
    nh = width // HEAD_DIM
    qb = q.reshape(-1, 1, dec, width)
    qb = jnp.pad(qb, ((0, 0), (0, 0), (0, 8 - dec), (0, 0)))
    qb = jnp.broadcast_to(qb, (qb.shape[0], nh, 8, width)).reshape(-1, 8 * nh, width)
    return (qb * hmask[None]).astype(BF16)


def _sample_attn(q, kn, vn, qm, kt, vt, kmt, vmt, dec):
    bd = kt.shape[0]
    w_buf = kt.shape[-1]
    cnt, cntn, hmask, hmaskm = _sample_counts(dec, w_buf)
    qbd = _block_diag_queries(q, dec, hmask)
    qmbd = _block_diag_queries(qm.astype(F32), dec, hmaskm)
    nb8, nm8 = qbd.shape[1], qmbd.shape[1]
    per_b = lambda shape: pl.BlockSpec((SAMPLE_SEQS,) + shape, lambda b: (b,) + (0,) * len(shape))
    full = lambda shape: pl.BlockSpec(shape, lambda b: (0,) * len(shape))
    ob, om = pl.pallas_call(
        _sample_attn_body,
        grid=(bd // SAMPLE_SEQS,),
        in_specs=[per_b((nb8, WIDTH_B)), per_b((WIDTH_B, w_buf)), per_b((WIDTH_B, w_buf)),
                  per_b((dec, WIDTH_B)), per_b((dec, WIDTH_B)),
                  full((nb8, w_buf)), full((nb8, dec)), full((nb8, WIDTH_B)),
                  per_b((nm8, WIDTH_M)), per_b((WIDTH_M, N_MEM)), per_b((WIDTH_M, N_MEM)), full((nm8, WIDTH_M))],
        out_specs=[per_b((8, WIDTH_B)), per_b((8, WIDTH_M))],
        out_shape=[jax.ShapeDtypeStruct((bd, 8, WIDTH_B), F32), jax.ShapeDtypeStruct((bd, 8, WIDTH_M), F32)],
        compiler_params=_cparams(("arbitrary",)),
        name="sample_attn",
    )(qbd, kt, vt, kn.reshape(bd, dec, WIDTH_B), vn.reshape(bd, dec, WIDTH_B),
      jnp.asarray(cnt), jnp.asarray(cntn), jnp.asarray(hmask),
      qmbd, kmt, vmt, jnp.asarray(hmaskm))
    return (ob[:, :dec].reshape(bd * dec, WIDTH_B).astype(BF16),
            om[:, :dec].reshape(bd * dec, WIDTH_M).astype(BF16))


def _post_body(*refs, tc, aliased):
    (x_ref, u_ref, va_ref, ob_ref, om_ref, wg_ref, bg_ref, wout_ref, g2_ref, wr_ref, br_ref,
     tri_ref, upper_ref) = refs[:13]
    x1_ref, xs_ref, ld_ref, tg_ref, seg_ref = refs[13 + aliased:]
    tm = x_ref.shape[0]
    u = u_ref[...]
    if tc is None:
        vaf = va_ref[...]
        mixed = wg_ref[0] * vaf + bg_ref[...]
        for s in range(1, wg_ref.shape[0]):
            mixed = mixed + wg_ref[s] * pltpu.roll(vaf, s, 0)
        oa = u * mixed
    else:
        lane = lax.broadcasted_iota(jnp.int32, (tc, LANE), 1)
        low = lane < HEAD_DIM
        va = va_ref[...].astype(BF16)
        oa_rows = []
        for c in range(tm // tc):
            r0 = c * tc
            tiles = []
            for p in range(3):
                vp = va[r0:r0 + tc, LANE * p:LANE * (p + 1)]
                r = jnp.dot(wg_ref[p], vp, preferred_element_type=F32)
                tiles.append(jnp.where(low, r[:tc], r[tc:]))
            mixed = jnp.concatenate(tiles, axis=1) + bg_ref[...]
            oa_rows.append(u[r0:r0 + tc] * mixed)
        oa = jnp.concatenate(oa_rows, axis=0)
    mixed_all = jnp.concatenate([oa.astype(BF16), ob_ref[...], om_ref[...]], axis=1)
    x1 = x_ref[...] + jnp.dot(mixed_all, wout_ref[...], preferred_element_type=F32)
    x1_ref[...] = x1
    ms = jnp.mean(x1 * x1, axis=-1, keepdims=True)
    h2 = x1 * lax.rsqrt(ms + EPS) * g2_ref[...]
    h_hi = h2.astype(BF16)
    h_lo = (h2 - h_hi.astype(F32)).astype(BF16)
    hw = jnp.dot(h_hi, wr_ref[...], preferred_element_type=F32)
    logits = (hw[:, :LANE] + hw[:, LANE:]
              + jnp.dot(h_lo, wr_ref[:, :LANE], preferred_element_type=F32)) + br_ref[...]
    lane_i = lax.broadcasted_iota(jnp.int32, (tm, LANE), 1)
    lane_r = lane_i.astype(F32)
    vals = logits
    tops, idxs = [], []
    for _ in range(TOP_K):
        mk = jnp.max(vals, axis=-1, keepdims=True)
        ik = jnp.min(jnp.where(vals == mk, lane_r, float(LANE)), axis=-1, keepdims=True)
        vals = jnp.where(lane_r == ik, -jnp.inf, vals)
        tops.append(mk)
        idxs.append(ik)
    es = [jnp.exp(t - tops[0]) for t in tops]
    den = es[0] + es[1] + es[2] + es[3]
    tg = jnp.zeros((tm, LANE), F32)
    for k in range(TOP_K):
        tg = jnp.where(lane_i == k, es[k] / den, tg)
    tg_ref[...] = tg
    col = lax.broadcasted_iota(jnp.int32, (MOE_TM, SEG_ROWS), 1).astype(F32)
    lane_t = lax.broadcasted_iota(jnp.int32, (MOE_TM, LANE), 1)
    hot_all = [lane_r == idxs[k] for k in range(TOP_K)]
    for hf in range(tm // MOE_TM):
        r0 = hf * MOE_TM
        hot = [h[r0:r0 + MOE_TM] for h in hot_all]
        sel = jnp.zeros((MOE_TM, LANE), F32)
        for k in range(TOP_K):
            sel = sel + jnp.where(hot[k], 1.0, 0.0)
        rank = jnp.dot(tri_ref[...], sel.astype(BF16), preferred_element_type=F32)
        length = jnp.sum(sel, axis=0, keepdims=True)
        plen = jnp.floor((length + (PIECE - 1)) * (1.0 / PIECE)) * PIECE
        loff = jnp.dot(jnp.broadcast_to(plen, (8, LANE)).astype(BF16), upper_ref[...],
                       preferred_element_type=F32)[0:1]
        base = loff + rank
        q = jnp.zeros((MOE_TM, SEG_ROWS), F32)
        ld = jnp.zeros((MOE_TM, LANE), F32)
        for k in range(TOP_K):
            ld_k = jnp.sum(jnp.where(hot[k], base, 0.0), axis=-1, keepdims=True)
            q = q + jnp.where(col == ld_k, 1.0, 0.0)
            ld = jnp.where(lane_t == k, ld_k, ld)
        xs_ref[hf * SEG_ROWS:(hf + 1) * SEG_ROWS, :] = lax.dot_general(
            q.astype(BF16), h_hi[r0:r0 + MOE_TM], (((0,), (0,)), ((), ())), preferred_element_type=F32)
        ld_ref[r0:r0 + MOE_TM, :] = ld.astype(jnp.int32)
        seg_ref[hf * 8:(hf + 1) * 8, :] = jnp.broadcast_to(length, (8, LANE)).astype(jnp.int32)


def _post(x, u, va, ob, om, wg, bg, w, tc, total_tiles, tile0=0, xs_all=None):
    t = x.shape[0]
    nt = t // TM
    full = lambda shape: pl.BlockSpec(shape, lambda i: (0,) * len(shape))
    row = lambda width: pl.BlockSpec((TM, width), lambda i: (i, 0))
    per = TM // MOE_TM
    ix = np.arange(MOE_TM)
    tri = jnp.asarray(ix[:, None] > ix[None, :], BF16)
    ex = np.arange(LANE)
    upper = jnp.asarray(ex[:, None] < ex[None, :], BF16)
    step0 = tile0 // per
    in_specs = [row(D_MODEL), row(WIDTH_A), row(WIDTH_A), row(WIDTH_B), row(WIDTH_M),
                full(wg.shape), full(bg.shape), full((D_MODEL, D_MODEL)), full((1, D_MODEL)),
                full((D_MODEL, 2 * LANE)), full((1, LANE)), full((MOE_TM, MOE_TM)), full((LANE, LANE))]
    args = [x, u, va, ob, om, wg, bg, w["w_out"], w["g2"], w["w_router"], w["b_router"], tri, upper]
    aliases = {}
    if xs_all is not None:
        in_specs.append(pl.BlockSpec(memory_space=pl.ANY))
        args.append(xs_all)
        aliases = {len(args) - 1: 1}
    return pl.pallas_call(
        functools.partial(_post_body, tc=tc, aliased=int(xs_all is not None)),
        grid=(nt,),
        in_specs=in_specs,
        out_specs=[row(D_MODEL), pl.BlockSpec((per * SEG_ROWS, D_MODEL), lambda i: (i + step0, 0)),
                   row(LANE), row(LANE), pl.BlockSpec((per * 8, LANE), lambda i: (i, 0))],
        out_shape=[jax.ShapeDtypeStruct((t, D_MODEL), F32),
                   jax.ShapeDtypeStruct((total_tiles * SEG_ROWS, D_MODEL), F32),
                   jax.ShapeDtypeStruct((t, LANE), jnp.int32), jax.ShapeDtypeStruct((t, LANE), F32),
                   jax.ShapeDtypeStruct((t // MOE_TM * 8, LANE), jnp.int32)],
        input_output_aliases=aliases,
        compiler_params=_cparams(("arbitrary",)),
        name="post",
    )(*args)


def _experts_body(te_ref, valid_ref, first_ref, next_ref, src_ref, dst_ref,
                  xs_hbm, wgu_hbm, bgu_ref, wd_hbm, bd_ref, ys_hbm,
                  xbuf, ybuf, wgu_f, wd_f, wgu_s, wd_s, gsem, ssem, wsem):
    t = pl.program_id(0)
    nt = pl.num_programs(0)
    slot = t % 2

    def gather(tile, sl):
        for i in range(NPIECE):
            s = src_ref[tile * NPIECE + i]
            pltpu.make_async_copy(xs_hbm.at[pl.ds(pl.multiple_of(s * PIECE, PIECE), PIECE), :],
                                  xbuf.at[sl, pl.ds(i * PIECE, PIECE), :], gsem.at[sl]).start()

    def scatter(tile, sl):
        for i in range(NPIECE):
            d = dst_ref[tile * NPIECE + i]
            pltpu.make_async_copy(ybuf.at[sl, pl.ds(i * PIECE, PIECE), :],
                                  ys_hbm.at[pl.ds(pl.multiple_of(d * PIECE, PIECE), PIECE), :], ssem.at[sl]).start()

    def wait_tile(hbm, buf, sem, sl):
        pltpu.make_async_copy(hbm.at[pl.ds(0, GM_TM), :], buf.at[sl], sem.at[sl]).wait()

    def weight_copies(e):
        return (pltpu.make_async_copy(wgu_hbm.at[e], wgu_f, wsem.at[0]),
                pltpu.make_async_copy(wd_hbm.at[e], wd_f, wsem.at[1]))

    @pl.when(t == 0)
    def _():
        for c in weight_copies(te_ref[0]):
            c.start()
        gather(0, 0)

    @pl.when(valid_ref[t] > 0)
    def _():
        nxt = jnp.minimum(t + 1, nt - 1)
        has_next = jnp.logical_and(t + 1 < nt, valid_ref[nxt] > 0)

        @pl.when(first_ref[t] > 0)
        def _():
            for c in weight_copies(te_ref[t]):
                c.wait()
            wgu_s[...] = wgu_f[...].astype(BF16)
            wd_s[...] = wd_f[...].astype(BF16)

            @pl.when(next_ref[t] >= 0)
            def _():
                for c in weight_copies(next_ref[t]):
                    c.start()

        @pl.when(t >= 2)
        def _():
            wait_tile(ys_hbm, ybuf, ssem, slot)

        gather(jnp.where(has_next, t + 1, t), 1 - slot)
        wait_tile(xs_hbm, xbuf, gsem, slot)
        x = xbuf[slot].astype(BF16)
        gu = jnp.dot(x, wgu_s[...], preferred_element_type=F32) + bgu_ref[0]
        gate = jnp.minimum(gu[:, :D_MODEL], SWIGLU_LIMIT)
        up = jnp.clip(gu[:, D_MODEL:], -SWIGLU_LIMIT, SWIGLU_LIMIT)
        act = (up + 1.0) * (gate * (1.0 / (1.0 + jnp.exp(-SWIGLU_ALPHA * gate))))
        ybuf[slot] = jnp.dot(act.astype(BF16), wd_s[...], preferred_element_type=F32) + bd_ref[0]
        scatter(t, slot)

        @pl.when(jnp.logical_not(has_next))
        def _():
            wait_tile(xs_hbm, xbuf, gsem, 1 - slot)
            wait_tile(ys_hbm, ybuf, ssem, slot)

            @pl.when(t >= 1)
            def _():
                wait_tile(ys_hbm, ybuf, ssem, 1 - slot)


def _experts(plan, xs, w, ys_rows):
    n_tiles = plan["tile_expert"].shape[0]
    by_expert = lambda shape: pl.BlockSpec((1,) + shape, lambda t, te, *_: (te[t],) + (0,) * len(shape))
    hbm = pl.BlockSpec(memory_space=pl.ANY)
    return pl.pallas_call(
        _experts_body,
        grid_spec=pltpu.PrefetchScalarGridSpec(
            num_scalar_prefetch=6, grid=(n_tiles,),
            in_specs=[hbm, hbm, by_expert((1, 2 * D_MODEL)), hbm, by_expert((1, D_MODEL))],
            out_specs=hbm,
            scratch_shapes=[pltpu.VMEM((2, GM_TM, D_MODEL), F32), pltpu.VMEM((2, GM_TM, D_MODEL), F32),
                            pltpu.VMEM((D_MODEL, 2 * D_MODEL), F32), pltpu.VMEM((D_MODEL, D_MODEL), F32),
                            pltpu.VMEM((D_MODEL, 2 * D_MODEL), BF16), pltpu.VMEM((D_MODEL, D_MODEL), BF16),
                            pltpu.SemaphoreType.DMA((2,)), pltpu.SemaphoreType.DMA((2,)),
                            pltpu.SemaphoreType.DMA((2,))]),
        out_shape=jax.ShapeDtypeStruct((ys_rows, D_MODEL), F32),
        compiler_params=_cparams(("arbitrary",)),
        name="moe_experts",
    )(plan["tile_expert"], plan["tile_valid"], plan["tile_first"], plan["tile_next"], plan["src"], plan["dst"],
      xs, w["w_gate_up"], w["b_gate_up"], w["w_down"], w["b_down"])


def _combine_body(used_ref, ld_ref, g_ref, x1_ref, ys_ref, y_ref, *, tile0):
    t = pl.program_id(0)
    row = lax.broadcasted_iota(jnp.int32, (SEG_ROWS, 1), 0)
    ys = jnp.where(row < used_ref[tile0 + t], ys_ref[...], 0.0).astype(BF16)
    col = lax.broadcasted_iota(jnp.int32, (MOE_TM, SEG_ROWS), 1)
    ld = ld_ref[...]
    g = g_ref[...]
    p = jnp.zeros((MOE_TM, SEG_ROWS), F32)
    for k in range(TOP_K):
        p = jnp.where(col == ld[:, k:k + 1], g[:, k:k + 1], p)
    y_ref[...] = x1_ref[...] + jnp.dot(p.astype(BF16), ys, preferred_element_type=F32)


def _combine(used, ld, gates, x1, ys, tile0):
    n = x1.shape[0]
    nt = n // MOE_TM
    rows = lambda width: pl.BlockSpec((MOE_TM, width), lambda t, *_: (t, 0))
    return pl.pallas_call(
        functools.partial(_combine_body, tile0=tile0),
        grid_spec=pltpu.PrefetchScalarGridSpec(
            num_scalar_prefetch=1, grid=(nt,),
            in_specs=[rows(LANE), rows(LANE), rows(D_MODEL),
                      pl.BlockSpec((SEG_ROWS, D_MODEL), lambda t, *_: (t + tile0, 0))],
            out_specs=rows(D_MODEL)),
        out_shape=jax.ShapeDtypeStruct((n, D_MODEL), F32),
        compiler_params=_cparams(("arbitrary",)),
        name="moe_combine",
    )(used, ld, gates, x1, ys)


def _moe(groups, xs, seglen, w):
    i32 = jnp.int32
    nt = seglen.shape[0]
    plen = (seglen + PIECE - 1) // PIECE * PIECE
    loff = jnp.cumsum(plen, axis=1) - plen
    used = jnp.sum(plen, axis=1).astype(i32)
    pp = plen // PIECE
    cp_end = jnp.cumsum(pp, axis=0)
    cp = cp_end - pp
    cnt_e = cp_end[-1]
    tiles_e = (cnt_e + NPIECE - 1) // NPIECE
    tile_end = jnp.cumsum(tiles_e)
    tile_start = tile_end - tiles_e
    n_tiles = (nt * SEG_ROWS + GM_TM - 1) // GM_TM + N_EXPERTS
    tix = jnp.arange(n_tiles, dtype=i32)
    total_tiles = tile_end[-1]
    tile_valid = (tix < total_tiles).astype(i32)
    expert_at = lambda tile: jnp.minimum((tile[:, None] >= tile_end[None, :]).astype(i32).sum(axis=1), N_EXPERTS - 1)
    last_expert = expert_at(jnp.maximum(total_tiles - 1, 0)[None])[0]
    tile_expert = jnp.where(tile_valid > 0, expert_at(tix), last_expert)
    hot_e = tile_expert[:, None] == jnp.arange(N_EXPERTS, dtype=i32)[None, :]
    per_tile = lambda v: jnp.sum(jnp.where(hot_e, v[None, :], 0), axis=1)
    per_tile_rows = lambda m: jnp.sum(jnp.where(hot_e[:, :, None], jnp.transpose(m)[None], 0), axis=1)
    start_t = per_tile(tile_start)
    tile_first = ((tix == start_t) & (tile_valid > 0)).astype(i32)
    following = per_tile(tile_end)
    tile_next = jnp.where(following < total_tiles, expert_at(following), -1)
    j = (tix - start_t)[:, None] * NPIECE + jnp.arange(NPIECE, dtype=i32)[None, :]
    ok = (tile_valid[:, None] > 0) & (j < per_tile(cnt_e)[:, None])
    ends_t, cp_t, loff_t = per_tile_rows(cp_end), per_tile_rows(cp), per_tile_rows(loff)
    t_q = jnp.minimum((ends_t[:, None, :] <= j[:, :, None]).astype(i32).sum(axis=2), nt - 1)
    hot_t = t_q[:, :, None] == jnp.arange(nt, dtype=i32)[None, None, :]
    at_t = lambda m: jnp.sum(jnp.where(hot_t, m[:, None, :], 0), axis=2)
    piece = t_q * SEG_PIECES + at_t(loff_t) // PIECE + j - at_t(cp_t)
    src = jnp.where(ok, piece, 0)
    dump = nt * SEG_PIECES + (tix % 2)[:, None] * NPIECE + jnp.arange(NPIECE, dtype=i32)[None, :]
    dst = jnp.where(ok, piece, dump)
    plan = dict(tile_expert=tile_expert.astype(i32), tile_valid=tile_valid, tile_first=tile_first,
                tile_next=tile_next.astype(i32), src=src.reshape(-1).astype(i32), dst=dst.reshape(-1).astype(i32))

    ys = _experts(plan, xs, w, nt * SEG_ROWS + 2 * GM_TM)
    outs, r0 = [], 0
    for x1, ld, tg in groups:
        outs.append(_combine(used, ld, tg, x1, ys, r0 // MOE_TM))
        r0 += x1.shape[0]
    return outs


def _pair_major_to_rows(a3):
    return jnp.transpose(a3, (1, 0, 2)).reshape(a3.shape[1], 3 * LANE)


def kernel(x_prompt, x_sample, mem_prompt, cache_win_k, cache_win_v, cache_mem_k, cache_mem_v, norm1_g, w_in, gv_a, w_s, b_s, gq_b, gk_b, gq_m, gk_m, mem_norm_g, w_mem_kv, w_out, norm2_g, w_router, b_router, w_gate_up, b_gate_up, w_down, b_down):
    batch, seq, _ = x_prompt.shape
    bd, dec, _ = x_sample.shape
    depth = norm1_g.shape[0]
    assert depth == 1 and seq % SPAN == 0 and (bd * dec) % TM == 0 and PAST_LEN % CHUNK == 0
    w_buf = cache_win_k.shape[2]
    assert w_buf == MAX_WINDOW and dec <= 8
    l = 0
    two = lambda g: jnp.concatenate([g, g])[None, :]
    head = np.arange(LANE) // HEAD_DIM
    wr = jnp.pad(w_router[l], ((0, 0), (0, LANE - N_EXPERTS)))
    wr_hi = wr.astype(BF16)
    wr_lo = (wr - wr_hi.astype(F32)).astype(BF16)
    w = dict(
        g1=norm1_g[l][None], w_in=w_in[l].astype(BF16), gva=gv_a[l][None],
        gq=two(gq_b[l]), gk=two(gk_b[l]), gqm=two(gq_m[l]), gkm=two(gk_m[l]),
        bd=jnp.asarray(head[:, None] == head[None, :], BF16),
        gmem=mem_norm_g[l][None], w_mem_kv=w_mem_kv[l].astype(BF16),
        w_out=w_out[l].astype(BF16), g2=norm2_g[l][None],
        w_router=jnp.concatenate([wr_hi, wr_lo], axis=1),
        b_router=jnp.pad(b_router[l], (0, LANE - N_EXPERTS), constant_values=-jnp.inf)[None],
        w_gate_up=w_gate_up[l], b_gate_up=b_gate_up[l][:, None, :], w_down=w_down[l], b_down=b_down[l][:, None, :],
    )
    ngrp = WIDTH_A // HEAD_DIM
    wtri = jnp.where(jnp.tril(jnp.ones((CHUNK, CHUNK), bool)), w_s[l], 0).astype(BF16)
    wg_p = wtri.reshape(ngrp // 2, 2 * CHUNK, CHUNK)
    bg_p = jnp.repeat(jnp.transpose(b_s[l]), HEAD_DIM, axis=1)
    zero = jnp.zeros((ngrp,), F32)
    lanes = lambda tg_: jnp.tile(jnp.repeat(tg_, HEAD_DIM, axis=1), (bd, 1))
    wg_s = jnp.stack([lanes(jnp.stack([w_s[l][:, t, t - s] if t >= s else zero for t in range(dec)]))
                      for s in range(dec)])
    bg_s = lanes(jnp.transpose(b_s[l][:, :dec]))

    xp = x_prompt.reshape(batch * seq, D_MODEL)
    tabs_p = _rope_tables(np.arange(seq))
    u_p, va_p, q3_p, k3_p, v3_p, qm_p = _premix(xp, tabs_p, seq // TM, w)
    ob_p = _attn_prompt(q3_p, k3_p, v3_p, batch, seq)
    km, vm = _memkv(mem_prompt.reshape(batch * N_MEM, D_MODEL), w)
    om_p = _memattn_prompt(qm_p, km.reshape(batch, N_MEM, WIDTH_M), vm.reshape(batch, N_MEM, WIDTH_M), batch, seq)
    tiles_p = batch * seq // MOE_TM
    tiles_all = tiles_p + bd * dec // MOE_TM
    x1_p, xs_all, ld_p, tg_p, seg_p = _post(xp, u_p, va_p, ob_p, om_p, wg_p, bg_p, w, CHUNK, tiles_all)

    xs = x_sample.reshape(bd * dec, D_MODEL)
    tabs_s = _rope_tables(np.tile(PAST_LEN + np.arange(dec), bd))
    u_s, va_s, q3_s, k3_s, v3_s, qm_s = _premix(xs, tabs_s, 1, w)
    q_s, k_s, v_s = (_pair_major_to_rows(a) for a in (q3_s, k3_s, v3_s))
    nb = WIDTH_B // HEAD_DIM
    nm = WIDTH_M // HEAD_DIM
    kt = jnp.transpose(cache_win_k[l], (0, 2, 3, 1)).reshape(bd, WIDTH_B, w_buf)
    vt = jnp.transpose(cache_win_v[l], (0, 2, 3, 1)).reshape(bd, WIDTH_B, w_buf)
    kmt = jnp.transpose(cache_mem_k[l], (0, 2, 3, 1)).reshape(bd, WIDTH_M, N_MEM)
    vmt = jnp.transpose(cache_mem_v[l], (0, 2, 3, 1)).reshape(bd, WIDTH_M, N_MEM)
    ob_s, om_s = _sample_attn(q_s, k_s, v_s, qm_s, kt, vt, kmt, vmt, dec)
    x1_s, xs_all, ld_s, tg_s, seg_s = _post(xs, u_s, va_s, ob_s, om_s, wg_s, bg_s, w, None, tiles_all,
                                             tile0=tiles_p, xs_all=xs_all)

    seglen = jnp.concatenate([seg_p, seg_s])[::8, :N_EXPERTS]
    y_p, y_s = _moe([(x1_p, ld_p, tg_p), (x1_s, ld_s, tg_s)], xs_all, seglen, w)
    y_prompt = y_p.reshape(batch, seq, D_MODEL)
    y_sample = y_s.reshape(bd, dec, D_MODEL)

    n_keep = min(MAX_WINDOW, seq)

    def window_rows(a3):
        a = a3.reshape(3, batch, seq, LANE)[:, :, seq - n_keep:]
        return jnp.transpose(a, (1, 2, 0, 3)).reshape(1, batch, n_keep, nb, HEAD_DIM)

    return (y_prompt, y_sample,
            window_rows(k3_p), window_rows(v3_p),
            km.reshape(1, batch, N_MEM, nm, HEAD_DIM), vm.reshape(1, batch, N_MEM, nm, HEAD_DIM),
            k_s.reshape(1, bd, dec, nb, HEAD_DIM), v_s.reshape(1, bd, dec, nb, HEAD_DIM),
            va_s.reshape(1, bd, dec, WIDTH_A))
```

```python
import functools

import numpy as np
import jax
import jax.numpy as jnp
from jax import lax
from jax.experimental import pallas as pl
from jax.experimental.pallas import tpu as pltpu

F32 = jnp.float32
BF16 = jnp.bfloat16

D_MODEL = 1024
HEAD_DIM = 64
WIDTH_A = 384
WIDTH_B = 384
WIDTH_M = 256
IN_WIDTH = 2 * WIDTH_A + 3 * WIDTH_B + WIDTH_M
CHUNK = 128
DILATIONS = ((128, 1), (512, 4), (2048, 16))
N_SUB = 128
MAX_WINDOW = 2048
N_MEM = 256
ROPE_THETA = 500000.0
ROT_HALF = 8
SCALE = HEAD_DIM ** -0.5
N_EXPERTS = 32
TOP_K = 4
SWIGLU_LIMIT = 7.0
SWIGLU_ALPHA = 1.702
EPS = 1e-6
PAST_LEN = 8192

LANE = 128
NEG = -1e30
TM = 512
SPAN = 2048
QB = 128
ATTN_UNROLL = 16
SAMPLE_SEQS = 2
MOE_TM = 256
GM_TM = 512
PIECE = 8
NPIECE = GM_TM // PIECE
SEG_ROWS = TOP_K * MOE_TM + N_EXPERTS * PIECE
SEG_PIECES = SEG_ROWS // PIECE
VMEM_LIMIT = 52 * 1024 * 1024


def _cparams(sem):
    return pltpu.CompilerParams(dimension_semantics=sem, vmem_limit_bytes=VMEM_LIMIT)


def _premix_body(x_ref, g1_ref, win_ref, gva_ref, gq_ref, gk_ref, gqm_ref, bd_ref,
                 rc_ref, rs1_ref, rs2_ref,
                 u_ref, va_ref, q3_ref, k3_ref, v3_ref, qm_ref):
    x = x_ref[...]
    tm = x.shape[0]
    ms = jnp.mean(x * x, axis=-1, keepdims=True)
    h = (x * lax.rsqrt(ms + EPS) * g1_ref[...]).astype(BF16)
    z = jnp.dot(h, win_ref[...], preferred_element_type=F32)
    u_ref[...] = z[:, :WIDTH_A]
    va = z[:, WIDTH_A:2 * WIDTH_A]
    va_ms = jnp.mean(va * va, axis=-1, keepdims=True)
    va_ref[...] = va * lax.rsqrt(va_ms + EPS) * gva_ref[...]
    q0, k0, v0, m0 = 2 * WIDTH_A, 2 * WIDTH_A + WIDTH_B, 2 * WIDTH_A + 2 * WIDTH_B, 2 * WIDTH_A + 3 * WIDTH_B
    tiles = ([z[:, q0 + LANE * j:q0 + LANE * (j + 1)] for j in range(3)]
             + [z[:, k0 + LANE * j:k0 + LANE * (j + 1)] for j in range(3)]
             + [z[:, m0 + LANE * j:m0 + LANE * (j + 1)] for j in range(2)])
    sq = [(t * t).astype(BF16) for t in tiles]
    sq = jnp.concatenate([jnp.concatenate(sq[i:i + 2], axis=1) for i in range(0, 8, 2)], axis=0)
    ssum = jnp.dot(sq, bd_ref[...], preferred_element_type=F32)
    inv = [lax.rsqrt(ssum[(i // 2) * tm:(i // 2 + 1) * tm, (i % 2) * LANE:(i % 2 + 1) * LANE] * (1.0 / HEAD_DIM) + EPS)
           for i in range(8)]
    rc, rs1, rs2 = rc_ref[...], rs1_ref[...], rs2_ref[...]

    def rope(t):
        return t * rc + pltpu.roll(t, LANE - ROT_HALF, 1) * rs1 + pltpu.roll(t, ROT_HALF, 1) * rs2

    for j in range(3):
        q3_ref[j] = rope(tiles[j] * inv[j] * gq_ref[...]) * SCALE
        k3_ref[j] = rope(tiles[3 + j] * inv[3 + j] * gk_ref[...])
        v3_ref[j] = z[:, v0 + LANE * j:v0 + LANE * (j + 1)]
    for j in range(2):
        qm_ref[:, LANE * j:LANE * (j + 1)] = (tiles[6 + j] * inv[6 + j] * gqm_ref[...] * SCALE).astype(BF16)


def _premix(x, tabs, n_tab_tiles, w):
    t = x.shape[0]
    nt = t // TM
    full = lambda shape: pl.BlockSpec(shape, lambda i: (0,) * len(shape))
    tab = pl.BlockSpec((TM, LANE), lambda i: (i % n_tab_tiles, 0))
    row = lambda width: pl.BlockSpec((TM, width), lambda i: (i, 0))
    pair = pl.BlockSpec((3, TM, LANE), lambda i: (0, i, 0))
    return pl.pallas_call(
        _premix_body,
        grid=(nt,),
        in_specs=[row(D_MODEL), full((1, D_MODEL)), full((D_MODEL, IN_WIDTH)), full((1, WIDTH_A)),
                  full((1, LANE)), full((1, LANE)), full((1, LANE)), full((2 * LANE, 2 * LANE)), tab, tab, tab],
        out_specs=[row(WIDTH_A), row(WIDTH_A), pair, pair, pair, row(WIDTH_M)],
        out_shape=[jax.ShapeDtypeStruct((t, WIDTH_A), F32), jax.ShapeDtypeStruct((t, WIDTH_A), F32),
                   jax.ShapeDtypeStruct((3, t, LANE), F32), jax.ShapeDtypeStruct((3, t, LANE), F32),
                   jax.ShapeDtypeStruct((3, t, LANE), F32), jax.ShapeDtypeStruct((t, WIDTH_M), BF16)],
        compiler_params=_cparams(("arbitrary",)),
        name="premix",
    )(x, w["g1"], w["w_in"], w["gva"], w["gq"], w["gk"], w["gqm"], w["bd2"], *tabs)


def _rope_tables(pos):
    pos = np.asarray(pos, np.float64)
    inv_freq = np.power(ROPE_THETA, -np.arange(ROT_HALF, dtype=np.float64) / ROT_HALF)
    ang = pos[:, None] * inv_freq[None, :]
    cos, sin = np.cos(ang), np.sin(ang)
    t = pos.shape[0]
    rest = HEAD_DIM - 2 * ROT_HALF
    c = np.concatenate([cos, cos, np.ones((t, rest))], axis=1)
    s1 = np.concatenate([-sin, np.zeros((t, HEAD_DIM - ROT_HALF))], axis=1)
    s2 = np.concatenate([np.zeros((t, ROT_HALF)), sin, np.zeros((t, rest))], axis=1)
    two = lambda a: jnp.asarray(np.concatenate([a, a], axis=1), F32)
    return two(c), two(s1), two(s2)


def _attn_body(q_ref, kc_ref, kp_ref, vc_ref, vp_ref, o_ref, kk, vv, m_s, l_s, a_s):
    span_idx = pl.program_id(1)
    p0 = span_idx * SPAN
    lane = lax.broadcasted_iota(jnp.int32, (QB, LANE), 1)
    low = lane < HEAD_DIM
    qi = lax.broadcasted_iota(jnp.int32, (QB, 2 * QB), 0)
    kj = lax.broadcasted_iota(jnp.int32, (QB, 2 * QB), 1)
    band = (kj >= qi) & (kj <= qi + N_SUB)
    kk[0:SPAN, :] = kp_ref[0]
    kk[SPAN:2 * SPAN, :] = kc_ref[0]
    vv[0:SPAN, :] = vp_ref[0]
    vv[SPAN:2 * SPAN, :] = vc_ref[0]

    def rows(ref, start, n, d):
        if d == 1:
            return ref[pl.ds(pl.multiple_of(start, QB), n), :]
        return ref[pl.ds(start, n, stride=d), :]

    def unit(d, qstart, first):
        first_key = jnp.maximum((QB * d - p0 - qstart + d - 1) // d, 0)
        mask = band & (kj >= first_key)
        qb = rows(q_ref.at[0], qstart, QB, d)
        kb = rows(kk, SPAN + qstart - QB * d, 2 * QB, d).astype(BF16)
        vb = rows(vv, SPAN + qstart - QB * d, 2 * QB, d).astype(BF16)
        stats = []
        for hm in (low, jnp.logical_not(low)):
            qh = jnp.where(hm, qb, 0.0).astype(BF16)
            s = lax.dot_general(qh, kb, (((1,), (1,)), ((), ())), preferred_element_type=F32)
            s = jnp.where(mask, s, NEG)
            m = jnp.max(s, axis=-1, keepdims=True)
            e = jnp.exp(s - m)
            l = jnp.sum(e, axis=-1, keepdims=True)
            acc = jnp.dot(e.astype(BF16), vb, preferred_element_type=F32)
            stats.append((m, l, acc))
        m_new = jnp.where(low, stats[0][0], stats[1][0])
        l_new = jnp.where(low, stats[0][1], stats[1][1])
        a_new = jnp.where(low, stats[0][2], stats[1][2])
        if d == 1:
            sl = (pl.ds(pl.multiple_of(qstart, QB), QB), slice(None))
        else:
            sl = (pl.ds(qstart, QB, stride=d), slice(None))
        if first:
            m_s[sl] = m_new
            l_s[sl] = l_new
            a_s[sl] = a_new
        else:
            m_old, l_old, a_old = m_s[sl], l_s[sl], a_s[sl]
            m_t = jnp.maximum(m_old, m_new)
            wa = jnp.exp(m_old - m_t)
            wb = jnp.exp(m_new - m_t)
            m_s[sl] = m_t
            l_s[sl] = wa * l_old + wb * l_new
            a_s[sl] = wa * a_old + wb * a_new

    nblk = SPAN // QB
    first = True
    for _, d in DILATIONS:
        per_res = nblk // d
        def group(gi, c, d=d, per_res=per_res, first=first):
            for uu in range(ATTN_UNROLL):
                u = gi * ATTN_UNROLL + uu
                unit(d, u // per_res + d * QB * (u % per_res), first)
            return c
        lax.fori_loop(0, nblk // ATTN_UNROLL, group, 0)
        first = False
    o_ref[...] = (a_s[...] / l_s[...]).astype(o_ref.dtype)


def _attn_prompt(q3, k3, v3, batch, seq):
    nspan = seq // SPAN
    cur = pl.BlockSpec((1, SPAN, LANE), lambda b, s, p: (p, b * nspan + s, 0))
    prv = pl.BlockSpec((1, SPAN, LANE), lambda b, s, p: (p, b * nspan + jnp.maximum(s - 1, 0), 0))
    return pl.pallas_call(
        _attn_body,
        grid=(batch, nspan, 3),
        in_specs=[cur, cur, prv, cur, prv],
        out_specs=pl.BlockSpec((SPAN, LANE), lambda b, s, p: (b * nspan + s, p)),
        out_shape=jax.ShapeDtypeStruct((batch * seq, WIDTH_B), BF16),
        scratch_shapes=[pltpu.VMEM((2 * SPAN, LANE), F32)] * 2 + [pltpu.VMEM((SPAN, LANE), F32)] * 3,
        compiler_params=_cparams(("arbitrary", "arbitrary", "arbitrary")),
        name="attn_prompt",
    )(q3, k3, k3, v3, v3)


def _memkv_body(mem_ref, g_ref, w_ref, gk_ref, bd_ref, k_ref, v_ref):
    x = mem_ref[...]
    ms = jnp.mean(x * x, axis=-1, keepdims=True)
    h = (x * lax.rsqrt(ms + EPS) * g_ref[...]).astype(BF16)
    kv = jnp.dot(h, w_ref[...], preferred_element_type=F32)
    n = x.shape[0]
    kt = [kv[:, LANE * j:LANE * (j + 1)] for j in range(2)]
    sq = jnp.concatenate([(t * t).astype(BF16) for t in kt], axis=0)
    ssum = jnp.dot(sq, bd_ref[...], preferred_element_type=F32)
    for j in range(2):
        inv = lax.rsqrt(ssum[j * n:(j + 1) * n] * (1.0 / HEAD_DIM) + EPS)
        k_ref[:, LANE * j:LANE * (j + 1)] = kt[j] * inv * gk_ref[...]
    v_ref[...] = kv[:, WIDTH_M:]


def _memkv(mem, w):
    n = mem.shape[0]
    return pl.pallas_call(
        _memkv_body,
        out_shape=[jax.ShapeDtypeStruct((n, WIDTH_M), F32)] * 2,
        compiler_params=pltpu.CompilerParams(vmem_limit_bytes=VMEM_LIMIT),
        name="memkv",
    )(mem, w["gmem"], w["w_mem_kv"], w["gkm"], w["bd"])


def _memattn_body(q_ref, k_ref, v_ref, o_ref):
    lane = lax.broadcasted_iota(jnp.int32, (q_ref.shape[0], LANE), 1)
    low = lane < HEAD_DIM
    for j in range(2):
        qp = q_ref[:, LANE * j:LANE * (j + 1)].astype(F32)
        kp = k_ref[0, :, LANE * j:LANE * (j + 1)].astype(BF16)
        vp = v_ref[0, :, LANE * j:LANE * (j + 1)].astype(BF16)
        outs = []
        for hm in (low, jnp.logical_not(low)):
            qh = jnp.where(hm, qp, 0.0).astype(BF16)
            s = lax.dot_general(qh, kp, (((1,), (1,)), ((), ())), preferred_element_type=F32)
            m = jnp.max(s, axis=-1, keepdims=True)
            e = jnp.exp(s - m)
            l = jnp.sum(e, axis=-1, keepdims=True)
            outs.append(jnp.dot(e.astype(BF16), vp, preferred_element_type=F32) / l)
        o_ref[:, LANE * j:LANE * (j + 1)] = jnp.where(low, outs[0], outs[1]).astype(o_ref.dtype)


def _memattn_prompt(qm, km, vm, batch, seq):
    tiles_per_b = seq // TM
    kv = pl.BlockSpec((1, N_MEM, WIDTH_M), lambda i: (i // tiles_per_b, 0, 0))
    return pl.pallas_call(
        _memattn_body,
        grid=(batch * tiles_per_b,),
        in_specs=[pl.BlockSpec((TM, WIDTH_M), lambda i: (i, 0)), kv, kv],
        out_specs=pl.BlockSpec((TM, WIDTH_M), lambda i: (i, 0)),
        out_shape=jax.ShapeDtypeStruct((batch * seq, WIDTH_M), BF16),
        compiler_params=_cparams(("arbitrary",)),
        name="memattn_prompt",
    )(qm, km, vm)


def _sample_attn_body(*refs):
    for i in range(refs[0].shape[0]):
        _sample_attn_one(i, *refs)


def _sample_attn_one(i, qbd_ref, kt_ref, vt_ref, kn_ref, vn_ref, cnt_ref, cntn_ref, hmask_ref,
                     qmbd_ref, kmt_ref, vmt_ref, hmaskm_ref, ob_ref, om_ref):
    dec = kn_ref.shape[1]
    qbd = qbd_ref[i]
    kt = kt_ref[i].astype(BF16)
    vt = vt_ref[i].astype(BF16)
    s = jnp.dot(qbd, kt, preferred_element_type=F32)
    qf = qbd.astype(F32)
    kn = kn_ref[i]
    vn = vn_ref[i]
    cnt = cnt_ref[...]
    cntn = cntn_ref[...]
    s_new = [jnp.sum(qf * kn[j:j + 1, :], axis=-1, keepdims=True) for j in range(dec)]
    m = jnp.max(jnp.where(cnt > 0, s, NEG), axis=-1, keepdims=True)
    for j in range(dec):
        m = jnp.maximum(m, jnp.where(cntn[:, j:j + 1] > 0, s_new[j], NEG))
    e = cnt * jnp.exp(jnp.where(cnt > 0, s - m, 0.0))
    l = jnp.sum(e, axis=-1, keepdims=True)
    acc = lax.dot_general(e.astype(BF16), vt, (((1,), (1,)), ((), ())), preferred_element_type=F32)
    for j in range(dec):
        w = cntn[:, j:j + 1]
        ej = w * jnp.exp(jnp.where(w > 0, s_new[j] - m, 0.0))
        l = l + ej
        acc = acc + ej * vn[j:j + 1, :]
    r = acc / l * hmask_ref[...]
    out = r[0:8]
    for h in range(1, WIDTH_B // HEAD_DIM):
        out = out + r[8 * h:8 * h + 8]
    ob_ref[i] = out
    qm = qmbd_ref[i]
    sm = jnp.dot(qm, kmt_ref[i].astype(BF16), preferred_element_type=F32)
    mm = jnp.max(sm, axis=-1, keepdims=True)
    em = jnp.exp(sm - mm)
    lm = jnp.sum(em, axis=-1, keepdims=True)
    am = lax.dot_general(em.astype(BF16), vmt_ref[i].astype(BF16), (((1,), (1,)), ((), ())),
                         preferred_element_type=F32)
    rm = am / lm * hmaskm_ref[...]
    outm = rm[0:8]
    for h in range(1, WIDTH_M // HEAD_DIM):
        outm = outm + rm[8 * h:8 * h + 8]
    om_ref[i] = outm


def _sample_counts(dec, w_buf):
    t = np.arange(8)[:, None]
    t = np.where(t < dec, t, 0)
    def mult(dist):
        c = np.zeros(dist.shape, np.float32)
        for window, dil in DILATIONS:
            c += ((dist >= 0) & (dist % dil == 0) & (dist <= window)).astype(np.float32)
        return c
    cache = mult(w_buf + t - np.arange(w_buf)[None, :])
    new = mult(t - np.arange(dec)[None, :])
    nb, nm = WIDTH_B // HEAD_DIM, WIDTH_M // HEAD_DIM
    hmask = (np.arange(8 * nb)[:, None] // 8 == np.arange(WIDTH_B)[None, :] // HEAD_DIM).astype(np.float32)
    hmaskm = (np.arange(8 * nm)[:, None] // 8 == np.arange(WIDTH_M)[None, :] // HEAD_DIM).astype(np.float32)
    return np.tile(cache, (nb, 1)), np.tile(new, (nb, 1)), hmask, hmaskm


def _block_diag_queries(q, dec, hmask):
    width = q.shape[-1]
    nh = width // HEAD_DIM
    qb = q.reshape(-1, 1, dec, width)
    qb = jnp.pad(qb, ((0, 0), (0, 0), (0, 8 - dec), (0, 0)))
    qb = jnp.broadcast_to(qb, (qb.shape[0], nh, 8, width)).reshape(-1, 8 * nh, width)
    return (qb * hmask[None]).astype(BF16)


def _sample_attn(q, kn, vn, qm, kt, vt, kmt, vmt, dec):
    bd = kt.shape[0]
    w_buf = kt.shape[-1]
    cnt, cntn, hmask, hmaskm = _sample_counts(dec, w_buf)
    qbd = _block_diag_queries(q, dec, hmask)
    qmbd = _block_diag_queries(qm.astype(F32), dec, hmaskm)
    nb8, nm8 = qbd.shape[1], qmbd.shape[1]
    per_b = lambda shape: pl.BlockSpec((SAMPLE_SEQS,) + shape, lambda b: (b,) + (0,) * len(shape))
    full = lambda shape: pl.BlockSpec(shape, lambda b: (0,) * len(shape))
    ob, om = pl.pallas_call(
        _sample_attn_body,
        grid=(bd // SAMPLE_SEQS,),
        in_specs=[per_b((nb8, WIDTH_B)), per_b((WIDTH_B, w_buf)), per_b((WIDTH_B, w_buf)),
                  per_b((dec, WIDTH_B)), per_b((dec, WIDTH_B)),
                  full((nb8, w_buf)), full((nb8, dec)), full((nb8, WIDTH_B)),
                  per_b((nm8, WIDTH_M)), per_b((WIDTH_M, N_MEM)), per_b((WIDTH_M, N_MEM)), full((nm8, WIDTH_M))],
        out_specs=[per_b((8, WIDTH_B)), per_b((8, WIDTH_M))],
        out_shape=[jax.ShapeDtypeStruct((bd, 8, WIDTH_B), F32), jax.ShapeDtypeStruct((bd, 8, WIDTH_M), F32)],
        compiler_params=_cparams(("arbitrary",)),
        name="sample_attn",
    )(qbd, kt, vt, kn.reshape(bd, dec, WIDTH_B), vn.reshape(bd, dec, WIDTH_B),
      jnp.asarray(cnt), jnp.asarray(cntn), jnp.asarray(hmask),
      qmbd, kmt, vmt, jnp.asarray(hmaskm))
    return (ob[:, :dec].reshape(bd * dec, WIDTH_B).astype(BF16),
            om[:, :dec].reshape(bd * dec, WIDTH_M).astype(BF16))


def _post_body(*refs, tc, aliased):
    (x_ref, u_ref, va_ref, ob_ref, om_ref, wg_ref, bg_ref, wout_ref, g2_ref, wr_ref, br_ref,
     tri_ref, upper_ref) = refs[:13]
    x1_ref, xs_ref, ld_ref, tg_ref, seg_ref = refs[13 + aliased:]
    tm = x_ref.shape[0]
    u = u_ref[...]
    if tc is None:
        vaf = va_ref[...]
        mixed = wg_ref[0] * vaf + bg_ref[...]
        for s in range(1, wg_ref.shape[0]):
            mixed = mixed + wg_ref[s] * pltpu.roll(vaf, s, 0)
        oa = u * mixed
    else:
        lane = lax.broadcasted_iota(jnp.int32, (tc, LANE), 1)
        low = lane < HEAD_DIM
        va = va_ref[...].astype(BF16)
        oa_rows = []
        for c in range(tm // tc):
            r0 = c * tc
            tiles = []
            for p in range(3):
                vp = va[r0:r0 + tc, LANE * p:LANE * (p + 1)]
                r = jnp.dot(wg_ref[p], vp, preferred_element_type=F32)
                tiles.append(jnp.where(low, r[:tc], r[tc:]))
            mixed = jnp.concatenate(tiles, axis=1) + bg_ref[...]
            oa_rows.append(u[r0:r0 + tc] * mixed)
        oa = jnp.concatenate(oa_rows, axis=0)
    mixed_all = jnp.concatenate([oa.astype(BF16), ob_ref[...], om_ref[...]], axis=1)
    x1 = x_ref[...] + jnp.dot(mixed_all, wout_ref[...], preferred_element_type=F32)
    x1_ref[...] = x1
    ms = jnp.mean(x1 * x1, axis=-1, keepdims=True)
    h2 = x1 * lax.rsqrt(ms + EPS) * g2_ref[...]
    h_hi = h2.astype(BF16)
    h_lo = (h2 - h_hi.astype(F32)).astype(BF16)
    hw = jnp.dot(h_hi, wr_ref[...], preferred_element_type=F32)
    logits = (hw[:, :LANE] + hw[:, LANE:]
              + jnp.dot(h_lo, wr_ref[:, :LANE], preferred_element_type=F32)) + br_ref[...]
    lane_i = lax.broadcasted_iota(jnp.int32, (tm, LANE), 1)
    lane_r = lane_i.astype(F32)
    vals = logits
    tops, idxs = [], []
    for _ in range(TOP_K):
        mk = jnp.max(vals, axis=-1, keepdims=True)
        ik = jnp.min(jnp.where(vals == mk, lane_r, float(LANE)), axis=-1, keepdims=True)
        vals = jnp.where(lane_r == ik, -jnp.inf, vals)
        tops.append(mk)
        idxs.append(ik)
    es = [jnp.exp(t - tops[0]) for t in tops]
    den = es[0] + es[1] + es[2] + es[3]
    tg = jnp.zeros((tm, LANE), F32)
    for k in range(TOP_K):
        tg = jnp.where(lane_i == k, es[k] / den, tg)
    tg_ref[...] = tg
    col = lax.broadcasted_iota(jnp.int32, (MOE_TM, SEG_ROWS), 1).astype(F32)
    lane_t = lax.broadcasted_iota(jnp.int32, (MOE_TM, LANE), 1)
    hot_all = [lane_r == idxs[k] for k in range(TOP_K)]
    for hf in range(tm // MOE_TM):
        r0 = hf * MOE_TM
        hot = [h[r0:r0 + MOE_TM] for h in hot_all]
        sel = jnp.zeros((MOE_TM, LANE), F32)
        for k in range(TOP_K):
            sel = sel + jnp.where(hot[k], 1.0, 0.0)
        rank = jnp.dot(tri_ref[...], sel.astype(BF16), preferred_element_type=F32)
        length = jnp.sum(sel, axis=0, keepdims=True)
        plen = jnp.floor((length + (PIECE - 1)) * (1.0 / PIECE)) * PIECE
        loff = jnp.dot(jnp.broadcast_to(plen, (8, LANE)).astype(BF16), upper_ref[...],
                       preferred_element_type=F32)[0:1]
        base = loff + rank
        q = jnp.zeros((MOE_TM, SEG_ROWS), F32)
        ld = jnp.zeros((MOE_TM, LANE), F32)
        for k in range(TOP_K):
            ld_k = jnp.sum(jnp.where(hot[k], base, 0.0), axis=-1, keepdims=True)
            q = q + jnp.where(col == ld_k, 1.0, 0.0)
            ld = jnp.where(lane_t == k, ld_k, ld)
        xs_ref[hf * SEG_ROWS:(hf + 1) * SEG_ROWS, :] = lax.dot_general(
            q.astype(BF16), h_hi[r0:r0 + MOE_TM], (((0,), (0,)), ((), ())),
            preferred_element_type=F32).astype(xs_ref.dtype)
        ld_ref[r0:r0 + MOE_TM, :] = ld.astype(jnp.int32)
        seg_ref[hf * 8:(hf + 1) * 8, :] = jnp.broadcast_to(length, (8, LANE)).astype(jnp.int32)


def _post(x, u, va, ob, om, wg, bg, w, tc, total_tiles, tile0=0, xs_all=None):
    t = x.shape[0]
    nt = t // TM
    full = lambda shape: pl.BlockSpec(shape, lambda i: (0,) * len(shape))
    row = lambda width: pl.BlockSpec((TM, width), lambda i: (i, 0))
    per = TM // MOE_TM
    ix = np.arange(MOE_TM)
    tri = jnp.asarray(ix[:, None] > ix[None, :], BF16)
    ex = np.arange(LANE)
    upper = jnp.asarray(ex[:, None] < ex[None, :], BF16)
    step0 = tile0 // per
    in_specs = [row(D_MODEL), row(WIDTH_A), row(WIDTH_A), row(WIDTH_B), row(WIDTH_M),
                full(wg.shape), full(bg.shape), full((D_MODEL, D_MODEL)), full((1, D_MODEL)),
                full((D_MODEL, 2 * LANE)), full((1, LANE)), full((MOE_TM, MOE_TM)), full((LANE, LANE))]
    args = [x, u, va, ob, om, wg, bg, w["w_out"], w["g2"], w["w_router"], w["b_router"], tri, upper]
    aliases = {}
    if xs_all is not None:
        in_specs.append(pl.BlockSpec(memory_space=pl.ANY))
        args.append(xs_all)
        aliases = {len(args) - 1: 1}
    return pl.pallas_call(
        functools.partial(_post_body, tc=tc, aliased=int(xs_all is not None)),
        grid=(nt,),
        in_specs=in_specs,
        out_specs=[row(D_MODEL), pl.BlockSpec((per * SEG_ROWS, D_MODEL), lambda i: (i + step0, 0)),
                   row(LANE), row(LANE), pl.BlockSpec((per * 8, LANE), lambda i: (i, 0))],
        out_shape=[jax.ShapeDtypeStruct((t, D_MODEL), F32),
                   jax.ShapeDtypeStruct((total_tiles * SEG_ROWS, D_MODEL), F32),
                   jax.ShapeDtypeStruct((t, LANE), jnp.int32), jax.ShapeDtypeStruct((t, LANE), F32),
                   jax.ShapeDtypeStruct((t // MOE_TM * 8, LANE), jnp.int32)],
        input_output_aliases=aliases,
        compiler_params=_cparams(("arbitrary",)),
        name="post",
    )(*args)


def _experts_body(te_ref, valid_ref, first_ref, next_ref, src_ref, dst_ref,
                  xs_hbm, wgu_hbm, bgu_ref, wd_hbm, bd_ref, ys_hbm,
                  xbuf, ybuf, wgu_f, wd_f, wgu_s, wd_s, gsem, ssem, wsem):
    t = pl.program_id(0)
    nt = pl.num_programs(0)
    slot = t % 2

    def gather(tile, sl):
        for i in range(NPIECE):
            s = src_ref[tile * NPIECE + i]
            pltpu.make_async_copy(xs_hbm.at[pl.ds(pl.multiple_of(s * PIECE, PIECE), PIECE), :],
                                  xbuf.at[sl, pl.ds(i * PIECE, PIECE), :], gsem.at[sl]).start()

    def scatter(tile, sl):
        for i in range(NPIECE):
            d = dst_ref[tile * NPIECE + i]
            pltpu.make_async_copy(ybuf.at[sl, pl.ds(i * PIECE, PIECE), :],
                                  ys_hbm.at[pl.ds(pl.multiple_of(d * PIECE, PIECE), PIECE), :], ssem.at[sl]).start()

    def wait_tile(hbm, buf, sem, sl):
        pltpu.make_async_copy(hbm.at[pl.ds(0, GM_TM), :], buf.at[sl], sem.at[sl]).wait()

    def weight_copies(e):
        return (pltpu.make_async_copy(wgu_hbm.at[e], wgu_f, wsem.at[0]),
                pltpu.make_async_copy(wd_hbm.at[e], wd_f, wsem.at[1]))

    @pl.when(t == 0)
    def _():
        for c in weight_copies(te_ref[0]):
            c.start()
        gather(0, 0)

    @pl.when(valid_ref[t] > 0)
    def _():
        nxt = jnp.minimum(t + 1, nt - 1)
        has_next = jnp.logical_and(t + 1 < nt, valid_ref[nxt] > 0)

        @pl.when(first_ref[t] > 0)
        def _():
            for c in weight_copies(te_ref[t]):
                c.wait()
            wgu_s[...] = wgu_f[...].astype(BF16)
            wd_s[...] = wd_f[...].astype(BF16)

            @pl.when(next_ref[t] >= 0)
            def _():
                for c in weight_copies(next_ref[t]):
                    c.start()

        @pl.when(t >= 2)
        def _():
            wait_tile(ys_hbm, ybuf, ssem, slot)

        gather(jnp.where(has_next, t + 1, t), 1 - slot)
        wait_tile(xs_hbm, xbuf, gsem, slot)
        x = xbuf[slot].astype(BF16)
        gu = jnp.dot(x, wgu_s[...], preferred_element_type=F32) + bgu_ref[0]
        gate = jnp.minimum(gu[:, :D_MODEL], SWIGLU_LIMIT)
        up = jnp.clip(gu[:, D_MODEL:], -SWIGLU_LIMIT, SWIGLU_LIMIT)
        act = (up + 1.0) * (gate * (1.0 / (1.0 + jnp.exp(-SWIGLU_ALPHA * gate))))
        y = jnp.dot(act.astype(BF16), wd_s[...], preferred_element_type=F32) + bd_ref[0]
        ybuf[slot] = y.astype(ybuf.dtype)
        scatter(t, slot)

        @pl.when(jnp.logical_not(has_next))
        def _():
            wait_tile(xs_hbm, xbuf, gsem, 1 - slot)
            wait_tile(ys_hbm, ybuf, ssem, slot)

            @pl.when(t >= 1)
            def _():
                wait_tile(ys_hbm, ybuf, ssem, 1 - slot)


def _experts(plan, xs, w, ys_rows):
    n_tiles = plan["tile_expert"].shape[0]
    by_expert = lambda shape: pl.BlockSpec((1,) + shape, lambda t, te, *_: (te[t],) + (0,) * len(shape))
    hbm = pl.BlockSpec(memory_space=pl.ANY)
    return pl.pallas_call(
        _experts_body,
        grid_spec=pltpu.PrefetchScalarGridSpec(
            num_scalar_prefetch=6, grid=(n_tiles,),
            in_specs=[hbm, hbm, by_expert((1, 2 * D_MODEL)), hbm, by_expert((1, D_MODEL))],
            out_specs=hbm,
            scratch_shapes=[pltpu.VMEM((2, GM_TM, D_MODEL), F32), pltpu.VMEM((2, GM_TM, D_MODEL), BF16),
                            pltpu.VMEM((D_MODEL, 2 * D_MODEL), F32), pltpu.VMEM((D_MODEL, D_MODEL), F32),
                            pltpu.VMEM((D_MODEL, 2 * D_MODEL), BF16), pltpu.VMEM((D_MODEL, D_MODEL), BF16),
                            pltpu.SemaphoreType.DMA((2,)), pltpu.SemaphoreType.DMA((2,)),
                            pltpu.SemaphoreType.DMA((2,))]),
        out_shape=jax.ShapeDtypeStruct((ys_rows, D_MODEL), BF16),
        compiler_params=_cparams(("arbitrary",)),
        name="moe_experts",
    )(plan["tile_expert"], plan["tile_valid"], plan["tile_first"], plan["tile_next"], plan["src"], plan["dst"],
      xs, w["w_gate_up"], w["b_gate_up"], w["w_down"], w["b_down"])


def _combine_body(used_ref, ld_ref, g_ref, x1_ref, ys_ref, y_ref, *, tile0):
    t = pl.program_id(0)
    row = lax.broadcasted_iota(jnp.int32, (SEG_ROWS, 1), 0)
    ys = jnp.where(row < used_ref[tile0 + t], ys_ref[...].astype(F32), 0.0).astype(BF16)
    col = lax.broadcasted_iota(jnp.int32, (MOE_TM, SEG_ROWS), 1)
    ld = ld_ref[...]
    g = g_ref[...]
    p = jnp.zeros((MOE_TM, SEG_ROWS), F32)
    for k in range(TOP_K):
        p = jnp.where(col == ld[:, k:k + 1], g[:, k:k + 1], p)
    y_ref[...] = x1_ref[...] + jnp.dot(p.astype(BF16), ys, preferred_element_type=F32)


def _combine(used, ld, gates, x1, ys, tile0):
    n = x1.shape[0]
    nt = n // MOE_TM
    rows = lambda width: pl.BlockSpec((MOE_TM, width), lambda t, *_: (t, 0))
    return pl.pallas_call(
        functools.partial(_combine_body, tile0=tile0),
        grid_spec=pltpu.PrefetchScalarGridSpec(
            num_scalar_prefetch=1, grid=(nt,),
            in_specs=[rows(LANE), rows(LANE), rows(D_MODEL),
                      pl.BlockSpec((SEG_ROWS, D_MODEL), lambda t, *_: (t + tile0, 0))],
            out_specs=rows(D_MODEL)),
        out_shape=jax.ShapeDtypeStruct((n, D_MODEL), F32),
        compiler_params=_cparams(("arbitrary",)),
        name="moe_combine",
    )(used, ld, gates, x1, ys)


def _moe(groups, xs, seglen, w):
    i32 = jnp.int32
    nt = seglen.shape[0]
    plen = (seglen + PIECE - 1) // PIECE * PIECE
    loff = jnp.cumsum(plen, axis=1) - plen
    used = jnp.sum(plen, axis=1).astype(i32)
    pp = plen // PIECE
    cp_end = jnp.cumsum(pp, axis=0)
    cp = cp_end - pp
    cnt_e = cp_end[-1]
    tiles_e = (cnt_e + NPIECE - 1) // NPIECE
    tile_end = jnp.cumsum(tiles_e)
    tile_start = tile_end - tiles_e
    n_tiles = (nt * SEG_ROWS + GM_TM - 1) // GM_TM + N_EXPERTS
    tix = jnp.arange(n_tiles, dtype=i32)
    total_tiles = tile_end[-1]
    tile_valid = (tix < total_tiles).astype(i32)
    expert_at = lambda tile: jnp.minimum((tile[:, None] >= tile_end[None, :]).astype(i32).sum(axis=1), N_EXPERTS - 1)
    last_expert = expert_at(jnp.maximum(total_tiles - 1, 0)[None])[0]
    tile_expert = jnp.where(tile_valid > 0, expert_at(tix), last_expert)
    hot_e = tile_expert[:, None] == jnp.arange(N_EXPERTS, dtype=i32)[None, :]
    per_tile = lambda v: jnp.sum(jnp.where(hot_e, v[None, :], 0), axis=1)
    per_tile_rows = lambda m: jnp.sum(jnp.where(hot_e[:, :, None], jnp.transpose(m)[None], 0), axis=1)
    start_t = per_tile(tile_start)
    tile_first = ((tix == start_t) & (tile_valid > 0)).astype(i32)
    following = per_tile(tile_end)
    tile_next = jnp.where(following < total_tiles, expert_at(following), -1)
    j = (tix - start_t)[:, None] * NPIECE + jnp.arange(NPIECE, dtype=i32)[None, :]
    ok = (tile_valid[:, None] > 0) & (j < per_tile(cnt_e)[:, None])
    ends_t, cp_t, loff_t = per_tile_rows(cp_end), per_tile_rows(cp), per_tile_rows(loff)
    t_q = jnp.minimum((ends_t[:, None, :] <= j[:, :, None]).astype(i32).sum(axis=2), nt - 1)
    hot_t = t_q[:, :, None] == jnp.arange(nt, dtype=i32)[None, None, :]
    at_t = lambda m: jnp.sum(jnp.where(hot_t, m[:, None, :], 0), axis=2)
    piece = t_q * SEG_PIECES + at_t(loff_t) // PIECE + j - at_t(cp_t)
    src = jnp.where(ok, piece, 0)
    dump = nt * SEG_PIECES + (tix % 2)[:, None] * NPIECE + jnp.arange(NPIECE, dtype=i32)[None, :]
    dst = jnp.where(ok, piece, dump)
    plan = dict(tile_expert=tile_expert.astype(i32), tile_valid=tile_valid, tile_first=tile_first,
                tile_next=tile_next.astype(i32), src=src.reshape(-1).astype(i32), dst=dst.reshape(-1).astype(i32))

    ys = _experts(plan, xs, w, nt * SEG_ROWS + 2 * GM_TM)
    outs, r0 = [], 0
    for x1, ld, tg in groups:
        outs.append(_combine(used, ld, tg, x1, ys, r0 // MOE_TM))
        r0 += x1.shape[0]
    return outs


def _pair_major_to_rows(a3):
    return jnp.transpose(a3, (1, 0, 2)).reshape(a3.shape[1], 3 * LANE)


def kernel(x_prompt, x_sample, mem_prompt, cache_win_k, cache_win_v, cache_mem_k, cache_mem_v, norm1_g, w_in, gv_a, w_s, b_s, gq_b, gk_b, gq_m, gk_m, mem_norm_g, w_mem_kv, w_out, norm2_g, w_router, b_router, w_gate_up, b_gate_up, w_down, b_down):
    batch, seq, _ = x_prompt.shape
    bd, dec, _ = x_sample.shape
    depth = norm1_g.shape[0]
    assert depth == 1 and seq % SPAN == 0 and (bd * dec) % TM == 0 and PAST_LEN % CHUNK == 0
    w_buf = cache_win_k.shape[2]
    assert w_buf == MAX_WINDOW and dec <= 8
    l = 0
    two = lambda g: jnp.concatenate([g, g])[None, :]
    head = np.arange(LANE) // HEAD_DIM
    head2 = np.arange(2 * LANE) // HEAD_DIM
    wr = jnp.pad(w_router[l], ((0, 0), (0, LANE - N_EXPERTS)))
    wr_hi = wr.astype(BF16)
    wr_lo = (wr - wr_hi.astype(F32)).astype(BF16)
    w = dict(
        g1=norm1_g[l][None], w_in=w_in[l].astype(BF16), gva=gv_a[l][None],
        gq=two(gq_b[l]), gk=two(gk_b[l]), gqm=two(gq_m[l]), gkm=two(gk_m[l]),
        bd=jnp.asarray(head[:, None] == head[None, :], BF16),
        bd2=jnp.asarray(head2[:, None] == head2[None, :], BF16),
        gmem=mem_norm_g[l][None], w_mem_kv=w_mem_kv[l].astype(BF16),
        w_out=w_out[l].astype(BF16), g2=norm2_g[l][None],
        w_router=jnp.concatenate([wr_hi, wr_lo], axis=1),
        b_router=jnp.pad(b_router[l], (0, LANE - N_EXPERTS), constant_values=-jnp.inf)[None],
        w_gate_up=w_gate_up[l], b_gate_up=b_gate_up[l][:, None, :], w_down=w_down[l], b_down=b_down[l][:, None, :],
    )
    ngrp = WIDTH_A // HEAD_DIM
    wtri = jnp.where(jnp.tril(jnp.ones((CHUNK, CHUNK), bool)), w_s[l], 0).astype(BF16)
    wg_p = wtri.reshape(ngrp // 2, 2 * CHUNK, CHUNK)
    bg_p = jnp.repeat(jnp.transpose(b_s[l]), HEAD_DIM, axis=1)
    zero = jnp.zeros((ngrp,), F32)
    lanes = lambda tg_: jnp.tile(jnp.repeat(tg_, HEAD_DIM, axis=1), (bd, 1))
    wg_s = jnp.stack([lanes(jnp.stack([w_s[l][:, t, t - s] if t >= s else zero for t in range(dec)]))
                      for s in range(dec)])
    bg_s = lanes(jnp.transpose(b_s[l][:, :dec]))

    xp = x_prompt.reshape(batch * seq, D_MODEL)
    tabs_p = _rope_tables(np.arange(seq))
    u_p, va_p, q3_p, k3_p, v3_p, qm_p = _premix(xp, tabs_p, seq // TM, w)
    ob_p = _attn_prompt(q3_p, k3_p, v3_p, batch, seq)
    km, vm = _memkv(mem_prompt.reshape(batch * N_MEM, D_MODEL), w)
    om_p = _memattn_prompt(qm_p, km.reshape(batch, N_MEM, WIDTH_M), vm.reshape(batch, N_MEM, WIDTH_M), batch, seq)
    tiles_p = batch * seq // MOE_TM
    tiles_all = tiles_p + bd * dec // MOE_TM
    x1_p, xs_all, ld_p, tg_p, seg_p = _post(xp, u_p, va_p, ob_p, om_p, wg_p, bg_p, w, CHUNK, tiles_all)

    xs = x_sample.reshape(bd * dec, D_MODEL)
    tabs_s = _rope_tables(np.tile(PAST_LEN + np.arange(dec), bd))
    u_s, va_s, q3_s, k3_s, v3_s, qm_s = _premix(xs, tabs_s, 1, w)
    q_s, k_s, v_s = (_pair_major_to_rows(a) for a in (q3_s, k3_s, v3_s))
    nb = WIDTH_B // HEAD_DIM
    nm = WIDTH_M // HEAD_DIM
    kt = jnp.transpose(cache_win_k[l], (0, 2, 3, 1)).reshape(bd, WIDTH_B, w_buf)
    vt = jnp.transpose(cache_win_v[l], (0, 2, 3, 1)).reshape(bd, WIDTH_B, w_buf)
    kmt = jnp.transpose(cache_mem_k[l], (0, 2, 3, 1)).reshape(bd, WIDTH_M, N_MEM)
    vmt = jnp.transpose(cache_mem_v[l], (0, 2, 3, 1)).reshape(bd, WIDTH_M, N_MEM)
    ob_s, om_s = _sample_attn(q_s, k_s, v_s, qm_s, kt, vt, kmt, vmt, dec)
    x1_s, xs_all, ld_s, tg_s, seg_s = _post(xs, u_s, va_s, ob_s, om_s, wg_s, bg_s, w, None, tiles_all,
                                             tile0=tiles_p, xs_all=xs_all)

    seglen = jnp.concatenate([seg_p, seg_s])[::8, :N_EXPERTS]
    y_p, y_s = _moe([(x1_p, ld_p, tg_p), (x1_s, ld_s, tg_s)], xs_all, seglen, w)
    y_prompt = y_p.reshape(batch, seq, D_MODEL)
    y_sample = y_s.reshape(bd, dec, D_MODEL)

    n_keep = min(MAX_WINDOW, seq)

    def window_rows(a3):
        a = a3.reshape(3, batch, seq, LANE)[:, :, seq - n_keep:]
        return jnp.transpose(a, (1, 2, 0, 3)).reshape(1, batch, n_keep, nb, HEAD_DIM)

    return (y_prompt, y_sample,
            window_rows(k3_p), window_rows(v3_p),
            km.reshape(1, batch, N_MEM, nm, HEAD_DIM), vm.reshape(1, batch, N_MEM, nm, HEAD_DIM),
            k_s.reshape(1, bd, dec, nb, HEAD_DIM), v_s.reshape(1, bd, dec, nb, HEAD_DIM),
            va_s.reshape(1, bd, dec, WIDTH_A))
```

```python
import functools

import numpy as np
import jax
import jax.numpy as jnp
from jax import lax
from jax.experimental import pallas as pl
from jax.experimental.pallas import tpu as pltpu

F32 = jnp.float32
BF16 = jnp.bfloat16

D_MODEL = 1024
HEAD_DIM = 64
WIDTH_A = 384
WIDTH_B = 384
WIDTH_M = 256
IN_WIDTH = 2 * WIDTH_A + 3 * WIDTH_B + WIDTH_M
CHUNK = 128
DILATIONS = ((128, 1), (512, 4), (2048, 16))
N_SUB = 128
MAX_WINDOW = 2048
N_MEM = 256
ROPE_THETA = 500000.0
ROT_HALF = 8
SCALE = HEAD_DIM ** -0.5
N_EXPERTS = 32
TOP_K = 4
SWIGLU_LIMIT = 7.0
SWIGLU_ALPHA = 1.702
EPS = 1e-6
PAST_LEN = 8192

LANE = 128
NEG = -1e30
TM = 512
SPAN = 2048
QB = 128
SAMPLE_SEQS = 2
MOE_TM = 256
GM_TM = 512
PIECE = 8
NPIECE = GM_TM // PIECE
SEG_ROWS = TOP_K * MOE_TM + N_EXPERTS * PIECE
SEG_PIECES = SEG_ROWS // PIECE
VMEM_LIMIT = 52 * 1024 * 1024


def _cparams(sem):
    return pltpu.CompilerParams(dimension_semantics=sem, vmem_limit_bytes=VMEM_LIMIT)


def _premix_body(x_ref, g1_ref, win_ref, gva_ref, gq_ref, gk_ref, gqm_ref, bd_ref,
                 rc_ref, rs1_ref, rs2_ref,
                 u_ref, va_ref, q3_ref, k3_ref, v3_ref, qm_ref):
    x = x_ref[...]
    tm = x.shape[0]
    ms = jnp.mean(x * x, axis=-1, keepdims=True)
    h = (x * lax.rsqrt(ms + EPS) * g1_ref[...]).astype(BF16)
    z = jnp.dot(h, win_ref[...], preferred_element_type=F32)
    u_ref[...] = z[:, :WIDTH_A]
    va = z[:, WIDTH_A:2 * WIDTH_A]
    va_ms = jnp.mean(va * va, axis=-1, keepdims=True)
    va_ref[...] = va * lax.rsqrt(va_ms + EPS) * gva_ref[...]
    q0, k0, v0, m0 = 2 * WIDTH_A, 2 * WIDTH_A + WIDTH_B, 2 * WIDTH_A + 2 * WIDTH_B, 2 * WIDTH_A + 3 * WIDTH_B
    tiles = ([z[:, q0 + LANE * j:q0 + LANE * (j + 1)] for j in range(3)]
             + [z[:, k0 + LANE * j:k0 + LANE * (j + 1)] for j in range(3)]
             + [z[:, m0 + LANE * j:m0 + LANE * (j + 1)] for j in range(2)])
    sq = [(t * t).astype(BF16) for t in tiles]
    sq = jnp.concatenate([jnp.concatenate(sq[i:i + 2], axis=1) for i in range(0, 8, 2)], axis=0)
    ssum = jnp.dot(sq, bd_ref[...], preferred_element_type=F32)
    inv = [lax.rsqrt(ssum[(i // 2) * tm:(i // 2 + 1) * tm, (i % 2) * LANE:(i % 2 + 1) * LANE] * (1.0 / HEAD_DIM) + EPS)
           for i in range(8)]
    rc, rs1, rs2 = rc_ref[...], rs1_ref[...], rs2_ref[...]

    def rope(t):
        return t * rc + pltpu.roll(t, LANE - ROT_HALF, 1) * rs1 + pltpu.roll(t, ROT_HALF, 1) * rs2

    for j in range(3):
        q3_ref[j] = rope(tiles[j] * inv[j] * gq_ref[...]) * SCALE
        k3_ref[j] = rope(tiles[3 + j] * inv[3 + j] * gk_ref[...])
        v3_ref[j] = z[:, v0 + LANE * j:v0 + LANE * (j + 1)]
    for j in range(2):
        qm_ref[:, LANE * j:LANE * (j + 1)] = (tiles[6 + j] * inv[6 + j] * gqm_ref[...] * SCALE).astype(BF16)


def _premix(x, tabs, n_tab_tiles, w):
    t = x.shape[0]
    nt = t // TM
    full = lambda shape: pl.BlockSpec(shape, lambda i: (0,) * len(shape))
    tab = pl.BlockSpec((TM, LANE), lambda i: (i % n_tab_tiles, 0))
    row = lambda width: pl.BlockSpec((TM, width), lambda i: (i, 0))
    pair = pl.BlockSpec((3, TM, LANE), lambda i: (0, i, 0))
    return pl.pallas_call(
        _premix_body,
        grid=(nt,),
        in_specs=[row(D_MODEL), full((1, D_MODEL)), full((D_MODEL, IN_WIDTH)), full((1, WIDTH_A)),
                  full((1, LANE)), full((1, LANE)), full((1, LANE)), full((2 * LANE, 2 * LANE)), tab, tab, tab],
        out_specs=[row(WIDTH_A), row(WIDTH_A), pair, pair, pair, row(WIDTH_M)],
        out_shape=[jax.ShapeDtypeStruct((t, WIDTH_A), F32), jax.ShapeDtypeStruct((t, WIDTH_A), F32),
                   jax.ShapeDtypeStruct((3, t, LANE), F32), jax.ShapeDtypeStruct((3, t, LANE), F32),
                   jax.ShapeDtypeStruct((3, t, LANE), F32), jax.ShapeDtypeStruct((t, WIDTH_M), BF16)],
        compiler_params=_cparams(("arbitrary",)),
        name="premix",
    )(x, w["g1"], w["w_in"], w["gva"], w["gq"], w["gk"], w["gqm"], w["bd2"], *tabs)


def _rope_tables(pos):
    pos = np.asarray(pos, np.float64)
    inv_freq = np.power(ROPE_THETA, -np.arange(ROT_HALF, dtype=np.float64) / ROT_HALF)
    ang = pos[:, None] * inv_freq[None, :]
    cos, sin = np.cos(ang), np.sin(ang)
    t = pos.shape[0]
    rest = HEAD_DIM - 2 * ROT_HALF
    c = np.concatenate([cos, cos, np.ones((t, rest))], axis=1)
    s1 = np.concatenate([-sin, np.zeros((t, HEAD_DIM - ROT_HALF))], axis=1)
    s2 = np.concatenate([np.zeros((t, ROT_HALF)), sin, np.zeros((t, rest))], axis=1)
    two = lambda a: jnp.asarray(np.concatenate([a, a], axis=1), F32)
    return two(c), two(s1), two(s2)


def _attn_body(q_ref, kc_ref, kp_ref, vc_ref, vp_ref, o_ref, m_s, l_s, a_s):
    span_idx = pl.program_id(1)
    p0 = span_idx * SPAN
    lane = lax.broadcasted_iota(jnp.int32, (QB, LANE), 1)
    low = lane < HEAD_DIM
    qi = lax.broadcasted_iota(jnp.int32, (QB, 2 * QB), 0)
    kj = lax.broadcasted_iota(jnp.int32, (QB, 2 * QB), 1)
    band = (kj >= qi) & (kj <= qi + N_SUB)

    def rows(ref, start, d):
        if d == 1:
            return ref[0, pl.ds(start, QB), :]
        return ref[0, pl.ds(start, QB, stride=d), :]

    def unit(d, res, blk, first):
        qstart = res + d * QB * blk
        if blk == 0:
            older = (kp_ref, vp_ref, SPAN - QB * d + res)
            first_key = jnp.maximum((QB * d - p0 - res + d - 1) // d, 0)
            mask = band & (kj >= first_key)
        else:
            older = (kc_ref, vc_ref, qstart - QB * d)
            mask = band
        qb = rows(q_ref, qstart, d)
        kb = jnp.concatenate([rows(older[0], older[2], d), rows(kc_ref, qstart, d)], axis=0).astype(BF16)
        vb = jnp.concatenate([rows(older[1], older[2], d), rows(vc_ref, qstart, d)], axis=0).astype(BF16)
        stats = []
        for hm in (low, jnp.logical_not(low)):
            qh = jnp.where(hm, qb, 0.0).astype(BF16)
            s = lax.dot_general(qh, kb, (((1,), (1,)), ((), ())), preferred_element_type=F32)
            s = jnp.where(mask, s, NEG)
            m = jnp.max(s, axis=-1, keepdims=True)
            e = jnp.exp(s - m)
            l = jnp.sum(e, axis=-1, keepdims=True)
            acc = jnp.dot(e.astype(BF16), vb, preferred_element_type=F32)
            stats.append((m, l, acc))
        m_new = jnp.where(low, stats[0][0], stats[1][0])
        l_new = jnp.where(low, stats[0][1], stats[1][1])
        a_new = jnp.where(low, stats[0][2], stats[1][2])
        sl = (pl.ds(qstart, QB) if d == 1 else pl.ds(qstart, QB, stride=d), slice(None))
        if first:
            m_s[sl] = m_new
            l_s[sl] = l_new
            a_s[sl] = a_new
        else:
            m_old, l_old, a_old = m_s[sl], l_s[sl], a_s[sl]
            m_t = jnp.maximum(m_old, m_new)
            wa = jnp.exp(m_old - m_t)
            wb = jnp.exp(m_new - m_t)
            m_s[sl] = m_t
            l_s[sl] = wa * l_old + wb * l_new
            a_s[sl] = wa * a_old + wb * a_new

    nblk = SPAN // QB
    first = True
    for _, d in DILATIONS:
        per_res = nblk // d
        for res in range(d):
            for blk in range(per_res):
                unit(d, res, blk, first)
        first = False
    o_ref[...] = (a_s[...] / l_s[...]).astype(o_ref.dtype)


def _attn_prompt(q3, k3, v3, batch, seq):
    nspan = seq // SPAN
    cur = pl.BlockSpec((1, SPAN, LANE), lambda b, s, p: (p, b * nspan + s, 0))
    prv = pl.BlockSpec((1, SPAN, LANE), lambda b, s, p: (p, b * nspan + jnp.maximum(s - 1, 0), 0))
    return pl.pallas_call(
        _attn_body,
        grid=(batch, nspan, 3),
        in_specs=[cur, cur, prv, cur, prv],
        out_specs=pl.BlockSpec((SPAN, LANE), lambda b, s, p: (b * nspan + s, p)),
        out_shape=jax.ShapeDtypeStruct((batch * seq, WIDTH_B), BF16),
        scratch_shapes=[pltpu.VMEM((SPAN, LANE), F32)] * 3,
        compiler_params=_cparams(("arbitrary", "arbitrary", "arbitrary")),
        name="attn_prompt",
    )(q3, k3, k3, v3, v3)


def _memkv_body(mem_ref, g_ref, w_ref, gk_ref, bd_ref, k_ref, v_ref):
    x = mem_ref[...]
    ms = jnp.mean(x * x, axis=-1, keepdims=True)
    h = (x * lax.rsqrt(ms + EPS) * g_ref[...]).astype(BF16)
    kv = jnp.dot(h, w_ref[...], preferred_element_type=F32)
    n = x.shape[0]
    kt = [kv[:, LANE * j:LANE * (j + 1)] for j in range(2)]
    sq = jnp.concatenate([(t * t).astype(BF16) for t in kt], axis=0)
    ssum = jnp.dot(sq, bd_ref[...], preferred_element_type=F32)
    for j in range(2):
        inv = lax.rsqrt(ssum[j * n:(j + 1) * n] * (1.0 / HEAD_DIM) + EPS)
        k_ref[:, LANE * j:LANE * (j + 1)] = kt[j] * inv * gk_ref[...]
    v_ref[...] = kv[:, WIDTH_M:]


def _memkv(mem, w):
    n = mem.shape[0]
    return pl.pallas_call(
        _memkv_body,
        out_shape=[jax.ShapeDtypeStruct((n, WIDTH_M), F32)] * 2,
        compiler_params=pltpu.CompilerParams(vmem_limit_bytes=VMEM_LIMIT),
        name="memkv",
    )(mem, w["gmem"], w["w_mem_kv"], w["gkm"], w["bd"])


def _memattn_body(q_ref, k_ref, v_ref, o_ref):
    lane = lax.broadcasted_iota(jnp.int32, (q_ref.shape[0], LANE), 1)
    low = lane < HEAD_DIM
    for j in range(2):
        qp = q_ref[:, LANE * j:LANE * (j + 1)].astype(F32)
        kp = k_ref[0, :, LANE * j:LANE * (j + 1)].astype(BF16)
        vp = v_ref[0, :, LANE * j:LANE * (j + 1)].astype(BF16)
        outs = []
        for hm in (low, jnp.logical_not(low)):
            qh = jnp.where(hm, qp, 0.0).astype(BF16)
            s = lax.dot_general(qh, kp, (((1,), (1,)), ((), ())), preferred_element_type=F32)
            m = jnp.max(s, axis=-1, keepdims=True)
            e = jnp.exp(s - m)
            l = jnp.sum(e, axis=-1, keepdims=True)
            outs.append(jnp.dot(e.astype(BF16), vp, preferred_element_type=F32) / l)
        o_ref[:, LANE * j:LANE * (j + 1)] = jnp.where(low, outs[0], outs[1]).astype(o_ref.dtype)


def _memattn_prompt(qm, km, vm, batch, seq):
    tiles_per_b = seq // TM
    kv = pl.BlockSpec((1, N_MEM, WIDTH_M), lambda i: (i // tiles_per_b, 0, 0))
    return pl.pallas_call(
        _memattn_body,
        grid=(batch * tiles_per_b,),
        in_specs=[pl.BlockSpec((TM, WIDTH_M), lambda i: (i, 0)), kv, kv],
        out_specs=pl.BlockSpec((TM, WIDTH_M), lambda i: (i, 0)),
        out_shape=jax.ShapeDtypeStruct((batch * seq, WIDTH_M), BF16),
        compiler_params=_cparams(("arbitrary",)),
        name="memattn_prompt",
    )(qm, km, vm)


def _sample_attn_body(*refs):
    for i in range(refs[0].shape[0]):
        _sample_attn_one(i, *refs)


def _sample_attn_one(i, qbd_ref, kt_ref, vt_ref, kn_ref, vn_ref, cnt_ref, cntn_ref, hmask_ref,
                     qmbd_ref, kmt_ref, vmt_ref, hmaskm_ref, ob_ref, om_ref):
    dec = kn_ref.shape[1]
    qbd = qbd_ref[i]
    kt = kt_ref[i].astype(BF16)
    vt = vt_ref[i].astype(BF16)
    s = jnp.dot(qbd, kt, preferred_element_type=F32)
    qf = qbd.astype(F32)
    kn = kn_ref[i]
    vn = vn_ref[i]
    cnt = cnt_ref[...]
    cntn = cntn_ref[...]
    s_new = [jnp.sum(qf * kn[j:j + 1, :], axis=-1, keepdims=True) for j in range(dec)]
    m = jnp.max(jnp.where(cnt > 0, s, NEG), axis=-1, keepdims=True)
    for j in range(dec):
        m = jnp.maximum(m, jnp.where(cntn[:, j:j + 1] > 0, s_new[j], NEG))
    e = cnt * jnp.exp(jnp.where(cnt > 0, s - m, 0.0))
    l = jnp.sum(e, axis=-1, keepdims=True)
    acc = lax.dot_general(e.astype(BF16), vt, (((1,), (1,)), ((), ())), preferred_element_type=F32)
    for j in range(dec):
        w = cntn[:, j:j + 1]
        ej = w * jnp.exp(jnp.where(w > 0, s_new[j] - m, 0.0))
        l = l + ej
        acc = acc + ej * vn[j:j + 1, :]
    r = acc / l * hmask_ref[...]
    out = r[0:8]
    for h in range(1, WIDTH_B // HEAD_DIM):
        out = out + r[8 * h:8 * h + 8]
    ob_ref[i] = out
    qm = qmbd_ref[i]
    sm = jnp.dot(qm, kmt_ref[i].astype(BF16), preferred_element_type=F32)
    mm = jnp.max(sm, axis=-1, keepdims=True)
    em = jnp.exp(sm - mm)
    lm = jnp.sum(em, axis=-1, keepdims=True)
    am = lax.dot_general(em.astype(BF16), vmt_ref[i].astype(BF16), (((1,), (1,)), ((), ())),
                         preferred_element_type=F32)
    rm = am / lm * hmaskm_ref[...]
    outm = rm[0:8]
    for h in range(1, WIDTH_M // HEAD_DIM):
        outm = outm + rm[8 * h:8 * h + 8]
    om_ref[i] = outm


def _sample_counts(dec, w_buf):
    t = np.arange(8)[:, None]
    t = np.where(t < dec, t, 0)
    def mult(dist):
        c = np.zeros(dist.shape, np.float32)
        for window, dil in DILATIONS:
            c += ((dist >= 0) & (dist % dil == 0) & (dist <= window)).astype(np.float32)
        return c
    cache = mult(w_buf + t - np.arange(w_buf)[None, :])
    new = mult(t - np.arange(dec)[None, :])
    nb, nm = WIDTH_B // HEAD_DIM, WIDTH_M // HEAD_DIM
    hmask = (np.arange(8 * nb)[:, None] // 8 == np.arange(WIDTH_B)[None, :] // HEAD_DIM).astype(np.float32)
    hmaskm = (np.arange(8 * nm)[:, None] // 8 == np.arange(WIDTH_M)[None, :] // HEAD_DIM).astype(np.float32)
    return np.tile(cache, (nb, 1)), np.tile(new, (nb, 1)), hmask, hmaskm


def _block_diag_queries(q, dec, hmask):
    width = q.shape[-1]
    nh = width // HEAD_DIM
    qb = q.reshape(-1, 1, dec, width)
    qb = jnp.pad(qb, ((0, 0), (0, 0), (0, 8 - dec), (0, 0)))
    qb = jnp.broadcast_to(qb, (qb.shape[0], nh, 8, width)).reshape(-1, 8 * nh, width)
    return (qb * hmask[None]).astype(BF16)


def _sample_attn(q, kn, vn, qm, kt, vt, kmt, vmt, dec):
    bd = kt.shape[0]
    w_buf = kt.shape[-1]
    cnt, cntn, hmask, hmaskm = _sample_counts(dec, w_buf)
    qbd = _block_diag_queries(q, dec, hmask)
    qmbd = _block_diag_queries(qm.astype(F32), dec, hmaskm)
    nb8, nm8 = qbd.shape[1], qmbd.shape[1]
    per_b = lambda shape: pl.BlockSpec((SAMPLE_SEQS,) + shape, lambda b: (b,) + (0,) * len(shape))
    full = lambda shape: pl.BlockSpec(shape, lambda b: (0,) * len(shape))
    ob, om = pl.pallas_call(
        _sample_attn_body,
        grid=(bd // SAMPLE_SEQS,),
        in_specs=[per_b((nb8, WIDTH_B)), per_b((WIDTH_B, w_buf)), per_b((WIDTH_B, w_buf)),
                  per_b((dec, WIDTH_B)), per_b((dec, WIDTH_B)),
                  full((nb8, w_buf)), full((nb8, dec)), full((nb8, WIDTH_B)),
                  per_b((nm8, WIDTH_M)), per_b((WIDTH_M, N_MEM)), per_b((WIDTH_M, N_MEM)), full((nm8, WIDTH_M))],
        out_specs=[per_b((8, WIDTH_B)), per_b((8, WIDTH_M))],
        out_shape=[jax.ShapeDtypeStruct((bd, 8, WIDTH_B), F32), jax.ShapeDtypeStruct((bd, 8, WIDTH_M), F32)],
        compiler_params=_cparams(("arbitrary",)),
        name="sample_attn",
    )(qbd, kt, vt, kn.reshape(bd, dec, WIDTH_B), vn.reshape(bd, dec, WIDTH_B),
      jnp.asarray(cnt), jnp.asarray(cntn), jnp.asarray(hmask),
      qmbd, kmt, vmt, jnp.asarray(hmaskm))
    return (ob[:, :dec].reshape(bd * dec, WIDTH_B).astype(BF16),
            om[:, :dec].reshape(bd * dec, WIDTH_M).astype(BF16))


def _post_body(*refs, tc, aliased):
    (x_ref, u_ref, va_ref, ob_ref, om_ref, wg_ref, bg_ref, wout_ref, g2_ref, wr_ref, br_ref,
     tri_ref, upper_ref) = refs[:13]
    x1_ref, xs_ref, ld_ref, tg_ref, seg_ref = refs[13 + aliased:]
    tm = x_ref.shape[0]
    u = u_ref[...]
    if tc is None:
        vaf = va_ref[...]
        mixed = wg_ref[0] * vaf + bg_ref[...]
        for s in range(1, wg_ref.shape[0]):
            mixed = mixed + wg_ref[s] * pltpu.roll(vaf, s, 0)
        oa = u * mixed
    else:
        lane = lax.broadcasted_iota(jnp.int32, (tc, LANE), 1)
        low = lane < HEAD_DIM
        va = va_ref[...].astype(BF16)
        oa_rows = []
        for c in range(tm // tc):
            r0 = c * tc
            tiles = []
            for p in range(3):
                vp = va[r0:r0 + tc, LANE * p:LANE * (p + 1)]
                r = jnp.dot(wg_ref[p], vp, preferred_element_type=F32)
                tiles.append(jnp.where(low, r[:tc], r[tc:]))
            mixed = jnp.concatenate(tiles, axis=1) + bg_ref[...]
            oa_rows.append(u[r0:r0 + tc] * mixed)
        oa = jnp.concatenate(oa_rows, axis=0)
    mixed_all = jnp.concatenate([oa.astype(BF16), ob_ref[...], om_ref[...]], axis=1)
    x1 = x_ref[...] + jnp.dot(mixed_all, wout_ref[...], preferred_element_type=F32)
    x1_ref[...] = x1
    ms = jnp.mean(x1 * x1, axis=-1, keepdims=True)
    h2 = x1 * lax.rsqrt(ms + EPS) * g2_ref[...]
    h_hi = h2.astype(BF16)
    h_lo = (h2 - h_hi.astype(F32)).astype(BF16)
    hw = jnp.dot(h_hi, wr_ref[...], preferred_element_type=F32)
    logits = (hw[:, :LANE] + hw[:, LANE:]
              + jnp.dot(h_lo, wr_ref[:, :LANE], preferred_element_type=F32)) + br_ref[...]
    lane_i = lax.broadcasted_iota(jnp.int32, (tm, LANE), 1)
    lane_r = lane_i.astype(F32)
    vals = logits
    tops, idxs = [], []
    for _ in range(TOP_K):
        mk = jnp.max(vals, axis=-1, keepdims=True)
        ik = jnp.min(jnp.where(vals == mk, lane_r, float(LANE)), axis=-1, keepdims=True)
        vals = jnp.where(lane_r == ik, -jnp.inf, vals)
        tops.append(mk)
        idxs.append(ik)
    es = [jnp.exp(t - tops[0]) for t in tops]
    den = es[0] + es[1] + es[2] + es[3]
    tg = jnp.zeros((tm, LANE), F32)
    for k in range(TOP_K):
        tg = jnp.where(lane_i == k, es[k] / den, tg)
    tg_ref[...] = tg
    col = lax.broadcasted_iota(jnp.int32, (MOE_TM, SEG_ROWS), 1).astype(F32)
    lane_t = lax.broadcasted_iota(jnp.int32, (MOE_TM, LANE), 1)
    hot_all = [lane_r == idxs[k] for k in range(TOP_K)]
    for hf in range(tm // MOE_TM):
        r0 = hf * MOE_TM
        hot = [h[r0:r0 + MOE_TM] for h in hot_all]
        sel = jnp.zeros((MOE_TM, LANE), F32)
        for k in range(TOP_K):
            sel = sel + jnp.where(hot[k], 1.0, 0.0)
        rank = jnp.dot(tri_ref[...], sel.astype(BF16), preferred_element_type=F32)
        length = jnp.sum(sel, axis=0, keepdims=True)
        plen = jnp.floor((length + (PIECE - 1)) * (1.0 / PIECE)) * PIECE
        loff = jnp.dot(jnp.broadcast_to(plen, (8, LANE)).astype(BF16), upper_ref[...],
                       preferred_element_type=F32)[0:1]
        base = loff + rank
        q = jnp.zeros((MOE_TM, SEG_ROWS), F32)
        ld = jnp.zeros((MOE_TM, LANE), F32)
        for k in range(TOP_K):
            ld_k = jnp.sum(jnp.where(hot[k], base, 0.0), axis=-1, keepdims=True)
            q = q + jnp.where(col == ld_k, 1.0, 0.0)
            ld = jnp.where(lane_t == k, ld_k, ld)
        xs_ref[hf * SEG_ROWS:(hf + 1) * SEG_ROWS, :] = lax.dot_general(
            q.astype(BF16), h_hi[r0:r0 + MOE_TM], (((0,), (0,)), ((), ())),
            preferred_element_type=F32).astype(xs_ref.dtype)
        ld_ref[r0:r0 + MOE_TM, :] = ld.astype(jnp.int32)
        seg_ref[hf * 8:(hf + 1) * 8, :] = jnp.broadcast_to(length, (8, LANE)).astype(jnp.int32)


def _post(x, u, va, ob, om, wg, bg, w, tc, total_tiles, tile0=0, xs_all=None):
    t = x.shape[0]
    nt = t // TM
    full = lambda shape: pl.BlockSpec(shape, lambda i: (0,) * len(shape))
    row = lambda width: pl.BlockSpec((TM, width), lambda i: (i, 0))
    per = TM // MOE_TM
    ix = np.arange(MOE_TM)
    tri = jnp.asarray(ix[:, None] > ix[None, :], BF16)
    ex = np.arange(LANE)
    upper = jnp.asarray(ex[:, None] < ex[None, :], BF16)
    step0 = tile0 // per
    in_specs = [row(D_MODEL), row(WIDTH_A), row(WIDTH_A), row(WIDTH_B), row(WIDTH_M),
                full(wg.shape), full(bg.shape), full((D_MODEL, D_MODEL)), full((1, D_MODEL)),
                full((D_MODEL, 2 * LANE)), full((1, LANE)), full((MOE_TM, MOE_TM)), full((LANE, LANE))]
    args = [x, u, va, ob, om, wg, bg, w["w_out"], w["g2"], w["w_router"], w["b_router"], tri, upper]
    aliases = {}
    if xs_all is not None:
        in_specs.append(pl.BlockSpec(memory_space=pl.ANY))
        args.append(xs_all)
        aliases = {len(args) - 1: 1}
    return pl.pallas_call(
        functools.partial(_post_body, tc=tc, aliased=int(xs_all is not None)),
        grid=(nt,),
        in_specs=in_specs,
        out_specs=[row(D_MODEL), pl.BlockSpec((per * SEG_ROWS, D_MODEL), lambda i: (i + step0, 0)),
                   row(LANE), row(LANE), pl.BlockSpec((per * 8, LANE), lambda i: (i, 0))],
        out_shape=[jax.ShapeDtypeStruct((t, D_MODEL), F32),
                   jax.ShapeDtypeStruct((total_tiles * SEG_ROWS, D_MODEL), F32),
                   jax.ShapeDtypeStruct((t, LANE), jnp.int32), jax.ShapeDtypeStruct((t, LANE), F32),
                   jax.ShapeDtypeStruct((t // MOE_TM * 8, LANE), jnp.int32)],
        input_output_aliases=aliases,
        compiler_params=_cparams(("arbitrary",)),
        name="post",
    )(*args)


def _experts_body(te_ref, valid_ref, first_ref, next_ref, half_ref, src_ref, dst_ref,
                  xs_hbm, wgu_hbm, bgu_ref, wd_hbm, bd_ref, ys_hbm,
                  xbuf, ybuf, wgu_f, wd_f, wgu_s, wd_s, gsem, ssem, wsem):
    t = pl.program_id(0)
    nt = pl.num_programs(0)
    slot = t % 2

    def gather(tile, sl):
        for i in range(NPIECE):
            s = src_ref[tile * NPIECE + i]
            pltpu.make_async_copy(xs_hbm.at[pl.ds(pl.multiple_of(s * PIECE, PIECE), PIECE), :],
                                  xbuf.at[sl, pl.ds(i * PIECE, PIECE), :], gsem.at[sl]).start()

    def scatter(tile, sl):
        for i in range(NPIECE):
            d = dst_ref[tile * NPIECE + i]
            pltpu.make_async_copy(ybuf.at[sl, pl.ds(i * PIECE, PIECE), :],
                                  ys_hbm.at[pl.ds(pl.multiple_of(d * PIECE, PIECE), PIECE), :], ssem.at[sl]).start()

    def wait_tile(hbm, buf, sem, sl):
        pltpu.make_async_copy(hbm.at[pl.ds(0, GM_TM), :], buf.at[sl], sem.at[sl]).wait()

    def weight_copies(e):
        return (pltpu.make_async_copy(wgu_hbm.at[e], wgu_f, wsem.at[0]),
                pltpu.make_async_copy(wd_hbm.at[e], wd_f, wsem.at[1]))

    @pl.when(t == 0)
    def _():
        for c in weight_copies(te_ref[0]):
            c.start()
        gather(0, 0)
        ybuf[...] = jnp.zeros(ybuf.shape, ybuf.dtype)

    @pl.when(valid_ref[t] > 0)
    def _():
        nxt = jnp.minimum(t + 1, nt - 1)
        has_next = jnp.logical_and(t + 1 < nt, valid_ref[nxt] > 0)

        @pl.when(first_ref[t] > 0)
        def _():
            for c in weight_copies(te_ref[t]):
                c.wait()
            wgu_s[...] = wgu_f[...].astype(BF16)
            wd_s[...] = wd_f[...].astype(BF16)

            @pl.when(next_ref[t] >= 0)
            def _():
                for c in weight_copies(next_ref[t]):
                    c.start()

        @pl.when(t >= 2)
        def _():
            wait_tile(ys_hbm, ybuf, ssem, slot)

        gather(jnp.where(has_next, t + 1, t), 1 - slot)
        wait_tile(xs_hbm, xbuf, gsem, slot)

        def ffn(rows):
            x = xbuf[slot, pl.ds(0, rows), :].astype(BF16)
            gu = jnp.dot(x, wgu_s[...], preferred_element_type=F32) + bgu_ref[0]
            gate = jnp.minimum(gu[:, :D_MODEL], SWIGLU_LIMIT)
            up = jnp.clip(gu[:, D_MODEL:], -SWIGLU_LIMIT, SWIGLU_LIMIT)
            act = (up + 1.0) * (gate * (1.0 / (1.0 + jnp.exp(-SWIGLU_ALPHA * gate))))
            y = jnp.dot(act.astype(BF16), wd_s[...], preferred_element_type=F32) + bd_ref[0]
            ybuf[slot, pl.ds(0, rows), :] = y.astype(ybuf.dtype)

        @pl.when(half_ref[t] > 0)
        def _():
            ffn(GM_TM // 2)

        @pl.when(half_ref[t] == 0)
        def _():
            ffn(GM_TM)

        scatter(t, slot)

        @pl.when(jnp.logical_not(has_next))
        def _():
            wait_tile(xs_hbm, xbuf, gsem, 1 - slot)
            wait_tile(ys_hbm, ybuf, ssem, slot)

            @pl.when(t >= 1)
            def _():
                wait_tile(ys_hbm, ybuf, ssem, 1 - slot)


def _experts(plan, xs, w, ys_rows):
    n_tiles = plan["tile_expert"].shape[0]
    by_expert = lambda shape: pl.BlockSpec((1,) + shape, lambda t, te, *_: (te[t],) + (0,) * len(shape))
    hbm = pl.BlockSpec(memory_space=pl.ANY)
    return pl.pallas_call(
        _experts_body,
        grid_spec=pltpu.PrefetchScalarGridSpec(
            num_scalar_prefetch=7, grid=(n_tiles,),
            in_specs=[hbm, hbm, by_expert((1, 2 * D_MODEL)), hbm, by_expert((1, D_MODEL))],
            out_specs=hbm,
            scratch_shapes=[pltpu.VMEM((2, GM_TM, D_MODEL), F32), pltpu.VMEM((2, GM_TM, D_MODEL), BF16),
                            pltpu.VMEM((D_MODEL, 2 * D_MODEL), F32), pltpu.VMEM((D_MODEL, D_MODEL), F32),
                            pltpu.VMEM((D_MODEL, 2 * D_MODEL), BF16), pltpu.VMEM((D_MODEL, D_MODEL), BF16),
                            pltpu.SemaphoreType.DMA((2,)), pltpu.SemaphoreType.DMA((2,)),
                            pltpu.SemaphoreType.DMA((2,))]),
        out_shape=jax.ShapeDtypeStruct((ys_rows, D_MODEL), BF16),
        compiler_params=_cparams(("arbitrary",)),
        name="moe_experts",
    )(plan["tile_expert"], plan["tile_valid"], plan["tile_first"], plan["tile_next"], plan["tile_half"],
      plan["src"], plan["dst"],
      xs, w["w_gate_up"], w["b_gate_up"], w["w_down"], w["b_down"])


def _combine_body(used_ref, ld_ref, g_ref, x1_ref, ys_ref, y_ref, *, tile0):
    t = pl.program_id(0)
    row = lax.broadcasted_iota(jnp.int32, (SEG_ROWS, 1), 0)
    ys = jnp.where(row < used_ref[tile0 + t], ys_ref[...].astype(F32), 0.0).astype(BF16)
    col = lax.broadcasted_iota(jnp.int32, (MOE_TM, SEG_ROWS), 1)
    ld = ld_ref[...]
    g = g_ref[...]
    p = jnp.zeros((MOE_TM, SEG_ROWS), F32)
    for k in range(TOP_K):
        p = jnp.where(col == ld[:, k:k + 1], g[:, k:k + 1], p)
    y_ref[...] = x1_ref[...] + jnp.dot(p.astype(BF16), ys, preferred_element_type=F32)


def _combine(used, ld, gates, x1, ys, tile0):
    n = x1.shape[0]
    nt = n // MOE_TM
    rows = lambda width: pl.BlockSpec((MOE_TM, width), lambda t, *_: (t, 0))
    return pl.pallas_call(
        functools.partial(_combine_body, tile0=tile0),
        grid_spec=pltpu.PrefetchScalarGridSpec(
            num_scalar_prefetch=1, grid=(nt,),
            in_specs=[rows(LANE), rows(LANE), rows(D_MODEL),
                      pl.BlockSpec((SEG_ROWS, D_MODEL), lambda t, *_: (t + tile0, 0))],
            out_specs=rows(D_MODEL)),
        out_shape=jax.ShapeDtypeStruct((n, D_MODEL), F32),
        compiler_params=_cparams(("arbitrary",)),
        name="moe_combine",
    )(used, ld, gates, x1, ys)


def _moe(groups, xs, seglen, w):
    i32 = jnp.int32
    nt = seglen.shape[0]
    plen = (seglen + PIECE - 1) // PIECE * PIECE
    loff = jnp.cumsum(plen, axis=1) - plen
    used = jnp.sum(plen, axis=1).astype(i32)
    pp = plen // PIECE
    cp_end = jnp.cumsum(pp, axis=0)
    cp = cp_end - pp
    cnt_e = cp_end[-1]
    tiles_e = (cnt_e + NPIECE - 1) // NPIECE
    tile_end = jnp.cumsum(tiles_e)
    tile_start = tile_end - tiles_e
    n_tiles = (nt * SEG_ROWS + GM_TM - 1) // GM_TM + N_EXPERTS
    tix = jnp.arange(n_tiles, dtype=i32)
    total_tiles = tile_end[-1]
    tile_valid = (tix < total_tiles).astype(i32)
    expert_at = lambda tile: jnp.minimum((tile[:, None] >= tile_end[None, :]).astype(i32).sum(axis=1), N_EXPERTS - 1)
    last_expert = expert_at(jnp.maximum(total_tiles - 1, 0)[None])[0]
    tile_expert = jnp.where(tile_valid > 0, expert_at(tix), last_expert)
    hot_e = tile_expert[:, None] == jnp.arange(N_EXPERTS, dtype=i32)[None, :]
    per_tile = lambda v: jnp.sum(jnp.where(hot_e, v[None, :], 0), axis=1)
    per_tile_rows = lambda m: jnp.sum(jnp.where(hot_e[:, :, None], jnp.transpose(m)[None], 0), axis=1)
    start_t = per_tile(tile_start)
    tile_first = ((tix == start_t) & (tile_valid > 0)).astype(i32)
    following = per_tile(tile_end)
    tile_next = jnp.where(following < total_tiles, expert_at(following), -1)
    j = (tix - start_t)[:, None] * NPIECE + jnp.arange(NPIECE, dtype=i32)[None, :]
    cnt_t = per_tile(cnt_e)
    ok = (tile_valid[:, None] > 0) & (j < cnt_t[:, None])
    tile_half = ((tile_valid > 0) & (cnt_t - (tix - start_t) * NPIECE <= NPIECE // 2)).astype(i32)
    ends_t, cp_t, loff_t = per_tile_rows(cp_end), per_tile_rows(cp), per_tile_rows(loff)
    t_q = jnp.minimum((ends_t[:, None, :] <= j[:, :, None]).astype(i32).sum(axis=2), nt - 1)
    hot_t = t_q[:, :, None] == jnp.arange(nt, dtype=i32)[None, None, :]
    at_t = lambda m: jnp.sum(jnp.where(hot_t, m[:, None, :], 0), axis=2)
    piece = t_q * SEG_PIECES + at_t(loff_t) // PIECE + j - at_t(cp_t)
    src = jnp.where(ok, piece, 0)
    dump = nt * SEG_PIECES + (tix % 2)[:, None] * NPIECE + jnp.arange(NPIECE, dtype=i32)[None, :]
    dst = jnp.where(ok, piece, dump)
    plan = dict(tile_expert=tile_expert.astype(i32), tile_valid=tile_valid, tile_first=tile_first,
                tile_next=tile_next.astype(i32), tile_half=tile_half,
                src=src.reshape(-1).astype(i32), dst=dst.reshape(-1).astype(i32))

    ys = _experts(plan, xs, w, nt * SEG_ROWS + 2 * GM_TM)
    outs, r0 = [], 0
    for x1, ld, tg in groups:
        outs.append(_combine(used, ld, tg, x1, ys, r0 // MOE_TM))
        r0 += x1.shape[0]
    return outs


def _pair_major_to_rows(a3):
    return jnp.transpose(a3, (1, 0, 2)).reshape(a3.shape[1], 3 * LANE)


def kernel(x_prompt, x_sample, mem_prompt, cache_win_k, cache_win_v, cache_mem_k, cache_mem_v, norm1_g, w_in, gv_a, w_s, b_s, gq_b, gk_b, gq_m, gk_m, mem_norm_g, w_mem_kv, w_out, norm2_g, w_router, b_router, w_gate_up, b_gate_up, w_down, b_down):
    batch, seq, _ = x_prompt.shape
    bd, dec, _ = x_sample.shape
    depth = norm1_g.shape[0]
    assert depth == 1 and seq % SPAN == 0 and (bd * dec) % TM == 0 and PAST_LEN % CHUNK == 0
    w_buf = cache_win_k.shape[2]
    assert w_buf == MAX_WINDOW and dec <= 8
    l = 0
    two = lambda g: jnp.concatenate([g, g])[None, :]
    head = np.arange(LANE) // HEAD_DIM
    head2 = np.arange(2 * LANE) // HEAD_DIM
    wr = jnp.pad(w_router[l], ((0, 0), (0, LANE - N_EXPERTS)))
    wr_hi = wr.astype(BF16)
    wr_lo = (wr - wr_hi.astype(F32)).astype(BF16)
    w = dict(
        g1=norm1_g[l][None], w_in=w_in[l].astype(BF16), gva=gv_a[l][None],
        gq=two(gq_b[l]), gk=two(gk_b[l]), gqm=two(gq_m[l]), gkm=two(gk_m[l]),
        bd=jnp.asarray(head[:, None] == head[None, :], BF16),
        bd2=jnp.asarray(head2[:, None] == head2[None, :], BF16),
        gmem=mem_norm_g[l][None], w_mem_kv=w_mem_kv[l].astype(BF16),
        w_out=w_out[l].astype(BF16), g2=norm2_g[l][None],
        w_router=jnp.concatenate([wr_hi, wr_lo], axis=1),
        b_router=jnp.pad(b_router[l], (0, LANE - N_EXPERTS), constant_values=-jnp.inf)[None],
        w_gate_up=w_gate_up[l], b_gate_up=b_gate_up[l][:, None, :], w_down=w_down[l], b_down=b_down[l][:, None, :],
    )
    ngrp = WIDTH_A // HEAD_DIM
    wtri = jnp.where(jnp.tril(jnp.ones((CHUNK, CHUNK), bool)), w_s[l], 0).astype(BF16)
    wg_p = wtri.reshape(ngrp // 2, 2 * CHUNK, CHUNK)
    bg_p = jnp.repeat(jnp.transpose(b_s[l]), HEAD_DIM, axis=1)
    zero = jnp.zeros((ngrp,), F32)
    lanes = lambda tg_: jnp.tile(jnp.repeat(tg_, HEAD_DIM, axis=1), (bd, 1))
    wg_s = jnp.stack([lanes(jnp.stack([w_s[l][:, t, t - s] if t >= s else zero for t in range(dec)]))
                      for s in range(dec)])
    bg_s = lanes(jnp.transpose(b_s[l][:, :dec]))

    xp = x_prompt.reshape(batch * seq, D_MODEL)
    tabs_p = _rope_tables(np.arange(seq))
    u_p, va_p, q3_p, k3_p, v3_p, qm_p = _premix(xp, tabs_p, seq // TM, w)
    ob_p = _attn_prompt(q3_p, k3_p, v3_p, batch, seq)
    km, vm = _memkv(mem_prompt.reshape(batch * N_MEM, D_MODEL), w)
    om_p = _memattn_prompt(qm_p, km.reshape(batch, N_MEM, WIDTH_M), vm.reshape(batch, N_MEM, WIDTH_M), batch, seq)
    tiles_p = batch * seq // MOE_TM
    tiles_all = tiles_p + bd * dec // MOE_TM
    x1_p, xs_all, ld_p, tg_p, seg_p = _post(xp, u_p, va_p, ob_p, om_p, wg_p, bg_p, w, CHUNK, tiles_all)

    xs = x_sample.reshape(bd * dec, D_MODEL)
    tabs_s = _rope_tables(np.tile(PAST_LEN + np.arange(dec), bd))
    u_s, va_s, q3_s, k3_s, v3_s, qm_s = _premix(xs, tabs_s, 1, w)
    q_s, k_s, v_s = (_pair_major_to_rows(a) for a in (q3_s, k3_s, v3_s))
    nb = WIDTH_B // HEAD_DIM
    nm = WIDTH_M // HEAD_DIM
    kt = jnp.transpose(cache_win_k[l], (0, 2, 3, 1)).reshape(bd, WIDTH_B, w_buf)
    vt = jnp.transpose(cache_win_v[l], (0, 2, 3, 1)).reshape(bd, WIDTH_B, w_buf)
    kmt = jnp.transpose(cache_mem_k[l], (0, 2, 3, 1)).reshape(bd, WIDTH_M, N_MEM)
    vmt = jnp.transpose(cache_mem_v[l], (0, 2, 3, 1)).reshape(bd, WIDTH_M, N_MEM)
    ob_s, om_s = _sample_attn(q_s, k_s, v_s, qm_s, kt, vt, kmt, vmt, dec)
    x1_s, xs_all, ld_s, tg_s, seg_s = _post(xs, u_s, va_s, ob_s, om_s, wg_s, bg_s, w, None, tiles_all,
                                             tile0=tiles_p, xs_all=xs_all)

    seglen = jnp.concatenate([seg_p, seg_s])[::8, :N_EXPERTS]
    y_p, y_s = _moe([(x1_p, ld_p, tg_p), (x1_s, ld_s, tg_s)], xs_all, seglen, w)
    y_prompt = y_p.reshape(batch, seq, D_MODEL)
    y_sample = y_s.reshape(bd, dec, D_MODEL)

    n_keep = min(MAX_WINDOW, seq)

    def window_rows(a3):
        a = a3.reshape(3, batch, seq, LANE)[:, :, seq - n_keep:]
        return jnp.transpose(a, (1, 2, 0, 3)).reshape(1, batch, n_keep, nb, HEAD_DIM)

    return (y_prompt, y_sample,
            window_rows(k3_p), window_rows(v3_p),
            km.reshape(1, batch, N_MEM, nm, HEAD_DIM), vm.reshape(1, batch, N_MEM, nm, HEAD_DIM),
            k_s.reshape(1, bd, dec, nb, HEAD_DIM), v_s.reshape(1, bd, dec, nb, HEAD_DIM),
            va_s.reshape(1, bd, dec, WIDTH_A))
```

```python
import functools

import numpy as np
import jax
import jax.numpy as jnp
from jax import lax
from jax.experimental import pallas as pl
from jax.experimental.pallas import tpu as pltpu

F32 = jnp.float32
BF16 = jnp.bfloat16

D_MODEL = 1024
HEAD_DIM = 64
WIDTH_A = 384
WIDTH_B = 384
WIDTH_M = 256
IN_WIDTH = 2 * WIDTH_A + 3 * WIDTH_B + WIDTH_M
CHUNK = 128
DILATIONS = ((128, 1), (512, 4), (2048, 16))
N_SUB = 128
MAX_WINDOW = 2048
N_MEM = 256
ROPE_THETA = 500000.0
ROT_HALF = 8
SCALE = HEAD_DIM ** -0.5
N_EXPERTS = 32
TOP_K = 4
SWIGLU_LIMIT = 7.0
SWIGLU_ALPHA = 1.702
EPS = 1e-6
PAST_LEN = 8192

LANE = 128
NEG = -1e30
TM = 512
SPAN = 2048
QB = 128
SAMPLE_SEQS = 2
MOE_TM = 256
GM_TM = 512
PIECE = 8
NPIECE = GM_TM // PIECE
SEG_ROWS = TOP_K * MOE_TM + N_EXPERTS * PIECE
SEG_PIECES = SEG_ROWS // PIECE
VMEM_LIMIT = 52 * 1024 * 1024


def _cparams(sem):
    return pltpu.CompilerParams(dimension_semantics=sem, vmem_limit_bytes=VMEM_LIMIT)


def _premix_body(x_ref, g1_ref, win_ref, gva_ref, gq_ref, gk_ref, gqm_ref, bd_ref,
                 rc_ref, rs1_ref, rs2_ref,
                 u_ref, va_ref, q3_ref, k3_ref, v3_ref, qm_ref):
    x = x_ref[...]
    tm = x.shape[0]
    ms = jnp.mean(x * x, axis=-1, keepdims=True)
    h = (x * lax.rsqrt(ms + EPS) * g1_ref[...]).astype(BF16)
    z = jnp.dot(h, win_ref[...], preferred_element_type=F32)
    u_ref[...] = z[:, :WIDTH_A]
    va = z[:, WIDTH_A:2 * WIDTH_A]
    va_ms = jnp.mean(va * va, axis=-1, keepdims=True)
    va_ref[...] = va * lax.rsqrt(va_ms + EPS) * gva_ref[...]
    q0, k0, v0, m0 = 2 * WIDTH_A, 2 * WIDTH_A + WIDTH_B, 2 * WIDTH_A + 2 * WIDTH_B, 2 * WIDTH_A + 3 * WIDTH_B
    tiles = ([z[:, q0 + LANE * j:q0 + LANE * (j + 1)] for j in range(3)]
             + [z[:, k0 + LANE * j:k0 + LANE * (j + 1)] for j in range(3)]
             + [z[:, m0 + LANE * j:m0 + LANE * (j + 1)] for j in range(2)])
    sq = [(t * t).astype(BF16) for t in tiles]
    sq = jnp.concatenate([jnp.concatenate(sq[i:i + 2], axis=1) for i in range(0, 8, 2)], axis=0)
    ssum = jnp.dot(sq, bd_ref[...], preferred_element_type=F32)
    inv = [lax.rsqrt(ssum[(i // 2) * tm:(i // 2 + 1) * tm, (i % 2) * LANE:(i % 2 + 1) * LANE] * (1.0 / HEAD_DIM) + EPS)
           for i in range(8)]
    rc, rs1, rs2 = rc_ref[...], rs1_ref[...], rs2_ref[...]

    def rope(t):
        return t * rc + pltpu.roll(t, LANE - ROT_HALF, 1) * rs1 + pltpu.roll(t, ROT_HALF, 1) * rs2

    for j in range(3):
        q3_ref[j] = rope(tiles[j] * inv[j] * gq_ref[...]) * SCALE
        k3_ref[j] = rope(tiles[3 + j] * inv[3 + j] * gk_ref[...])
        v3_ref[j] = z[:, v0 + LANE * j:v0 + LANE * (j + 1)]
    for j in range(2):
        qm_ref[:, LANE * j:LANE * (j + 1)] = (tiles[6 + j] * inv[6 + j] * gqm_ref[...] * SCALE).astype(BF16)


def _premix_parts(x, tabs, seq, w, tm):
    t = x.shape[0]
    n_tab_tiles = seq // tm
    full = lambda shape: pl.BlockSpec(shape, lambda i: (0,) * len(shape))
    tab = pl.BlockSpec((tm, LANE), lambda i: (i % n_tab_tiles, 0))
    row = lambda width: pl.BlockSpec((tm, width), lambda i: (i, 0))
    pair = pl.BlockSpec((3, tm, LANE), lambda i: (0, i, 0))
    return dict(
        steps=t // tm,
        in_specs=[row(D_MODEL), full((1, D_MODEL)), full((D_MODEL, IN_WIDTH)), full((1, WIDTH_A)),
                  full((1, LANE)), full((1, LANE)), full((1, LANE)), full((2 * LANE, 2 * LANE)), tab, tab, tab],
        args=[x, w["g1"], w["w_in"], w["gva"], w["gq"], w["gk"], w["gqm"], w["bd2"], *tabs],
        out_specs=[row(WIDTH_A), row(WIDTH_A), pair, pair, pair, row(WIDTH_M)],
        out_shape=[jax.ShapeDtypeStruct((t, WIDTH_A), F32), jax.ShapeDtypeStruct((t, WIDTH_A), F32),
                   jax.ShapeDtypeStruct((3, t, LANE), F32), jax.ShapeDtypeStruct((3, t, LANE), F32),
                   jax.ShapeDtypeStruct((3, t, LANE), F32), jax.ShapeDtypeStruct((t, WIDTH_M), BF16)])


def _premix(x, tabs, seq, w):
    p = _premix_parts(x, tabs, seq, w, TM)
    return pl.pallas_call(
        _premix_body, grid=(p["steps"],), in_specs=p["in_specs"], out_specs=p["out_specs"],
        out_shape=p["out_shape"], compiler_params=_cparams(("arbitrary",)), name="premix",
    )(*p["args"])


def _premix_sample_body(*refs, n_in, n_out):
    ins_a, ins_b = refs[:n_in[0]], refs[n_in[0]:n_in[0] + n_in[1]]
    outs = refs[n_in[0] + n_in[1]:]
    _premix_body(*ins_a, *outs[:n_out])
    _sample_attn_body(*ins_b, *outs[n_out:])


def _premix_with_sample_attn(a, b):
    assert a["steps"] == b["steps"]
    outs = pl.pallas_call(
        functools.partial(_premix_sample_body, n_in=(len(a["args"]), len(b["args"])), n_out=len(a["out_shape"])),
        grid=(a["steps"],), in_specs=a["in_specs"] + b["in_specs"], out_specs=a["out_specs"] + b["out_specs"],
        out_shape=a["out_shape"] + b["out_shape"], compiler_params=_cparams(("arbitrary",)),
        name="premix_sample_attn",
    )(*a["args"], *b["args"])
    return outs[:len(a["out_shape"])], outs[len(a["out_shape"]):]


def _rope_tables(pos):
    pos = np.asarray(pos, np.float64)
    inv_freq = np.power(ROPE_THETA, -np.arange(ROT_HALF, dtype=np.float64) / ROT_HALF)
    ang = pos[:, None] * inv_freq[None, :]
    cos, sin = np.cos(ang), np.sin(ang)
    t = pos.shape[0]
    rest = HEAD_DIM - 2 * ROT_HALF
    c = np.concatenate([cos, cos, np.ones((t, rest))], axis=1)
    s1 = np.concatenate([-sin, np.zeros((t, HEAD_DIM - ROT_HALF))], axis=1)
    s2 = np.concatenate([np.zeros((t, ROT_HALF)), sin, np.zeros((t, rest))], axis=1)
    two = lambda a: jnp.asarray(np.concatenate([a, a], axis=1), F32)
    return two(c), two(s1), two(s2)


def _attn_body(q_ref, kc_ref, kp_ref, vc_ref, vp_ref, o_ref, m_s, l_s, a_s):
    span_idx = pl.program_id(1)
    p0 = span_idx * SPAN
    lane = lax.broadcasted_iota(jnp.int32, (QB, LANE), 1)
    low = lane < HEAD_DIM
    qi = lax.broadcasted_iota(jnp.int32, (QB, 2 * QB), 0)
    kj = lax.broadcasted_iota(jnp.int32, (QB, 2 * QB), 1)
    band = (kj >= qi) & (kj <= qi + N_SUB)

    def rows(ref, start, d):
        if d == 1:
            return ref[0, pl.ds(start, QB), :]
        return ref[0, pl.ds(start, QB, stride=d), :]

    def unit(d, res, blk, first):
        qstart = res + d * QB * blk
        if blk == 0:
            older = (kp_ref, vp_ref, SPAN - QB * d + res)
            first_key = jnp.maximum((QB * d - p0 - res + d - 1) // d, 0)
            mask = band & (kj >= first_key)
        else:
            older = (kc_ref, vc_ref, qstart - QB * d)
            mask = band
        qb = rows(q_ref, qstart, d)
        kb = jnp.concatenate([rows(older[0], older[2], d), rows(kc_ref, qstart, d)], axis=0).astype(BF16)
        vb = jnp.concatenate([rows(older[1], older[2], d), rows(vc_ref, qstart, d)], axis=0).astype(BF16)
        stats = []
        for hm in (low, jnp.logical_not(low)):
            qh = jnp.where(hm, qb, 0.0).astype(BF16)
            s = lax.dot_general(qh, kb, (((1,), (1,)), ((), ())), preferred_element_type=F32)
            s = jnp.where(mask, s, NEG)
            m = jnp.max(s, axis=-1, keepdims=True)
            e = jnp.exp(s - m)
            l = jnp.sum(e, axis=-1, keepdims=True)
            acc = jnp.dot(e.astype(BF16), vb, preferred_element_type=F32)
            stats.append((m, l, acc))
        m_new = jnp.where(low, stats[0][0], stats[1][0])
        l_new = jnp.where(low, stats[0][1], stats[1][1])
        a_new = jnp.where(low, stats[0][2], stats[1][2])
        sl = (pl.ds(qstart, QB) if d == 1 else pl.ds(qstart, QB, stride=d), slice(None))
        if first:
            m_s[sl] = m_new
            l_s[sl] = l_new
            a_s[sl] = a_new
        else:
            m_old, l_old, a_old = m_s[sl], l_s[sl], a_s[sl]
            m_t = jnp.maximum(m_old, m_new)
            wa = jnp.exp(m_old - m_t)
            wb = jnp.exp(m_new - m_t)
            m_s[sl] = m_t
            l_s[sl] = wa * l_old + wb * l_new
            a_s[sl] = wa * a_old + wb * a_new

    nblk = SPAN // QB
    first = True
    for _, d in DILATIONS:
        per_res = nblk // d
        for res in range(d):
            for blk in range(per_res):
                unit(d, res, blk, first)
        first = False
    o_ref[...] = (a_s[...] / l_s[...]).astype(o_ref.dtype)


def _attn_prompt(q3, k3, v3, batch, seq):
    nspan = seq // SPAN
    cur = pl.BlockSpec((1, SPAN, LANE), lambda b, s, p: (p, b * nspan + s, 0))
    prv = pl.BlockSpec((1, SPAN, LANE), lambda b, s, p: (p, b * nspan + jnp.maximum(s - 1, 0), 0))
    return pl.pallas_call(
        _attn_body,
        grid=(batch, nspan, 3),
        in_specs=[cur, cur, prv, cur, prv],
        out_specs=pl.BlockSpec((SPAN, LANE), lambda b, s, p: (b * nspan + s, p)),
        out_shape=jax.ShapeDtypeStruct((batch * seq, WIDTH_B), BF16),
        scratch_shapes=[pltpu.VMEM((SPAN, LANE), F32)] * 3,
        compiler_params=_cparams(("arbitrary", "arbitrary", "arbitrary")),
        name="attn_prompt",
    )(q3, k3, k3, v3, v3)


def _memkv_body(mem_ref, g_ref, w_ref, gk_ref, bd_ref, k_ref, v_ref):
    x = mem_ref[...]
    ms = jnp.mean(x * x, axis=-1, keepdims=True)
    h = (x * lax.rsqrt(ms + EPS) * g_ref[...]).astype(BF16)
    kv = jnp.dot(h, w_ref[...], preferred_element_type=F32)
    n = x.shape[0]
    kt = [kv[:, LANE * j:LANE * (j + 1)] for j in range(2)]
    sq = jnp.concatenate([(t * t).astype(BF16) for t in kt], axis=0)
    ssum = jnp.dot(sq, bd_ref[...], preferred_element_type=F32)
    for j in range(2):
        inv = lax.rsqrt(ssum[j * n:(j + 1) * n] * (1.0 / HEAD_DIM) + EPS)
        k_ref[:, LANE * j:LANE * (j + 1)] = kt[j] * inv * gk_ref[...]
    v_ref[...] = kv[:, WIDTH_M:]


def _memkv(mem, w):
    n = mem.shape[0]
    return pl.pallas_call(
        _memkv_body,
        out_shape=[jax.ShapeDtypeStruct((n, WIDTH_M), F32)] * 2,
        compiler_params=pltpu.CompilerParams(vmem_limit_bytes=VMEM_LIMIT),
        name="memkv",
    )(mem, w["gmem"], w["w_mem_kv"], w["gkm"], w["bd"])


def _memattn_body(q_ref, k_ref, v_ref, o_ref):
    lane = lax.broadcasted_iota(jnp.int32, (q_ref.shape[0], LANE), 1)
    low = lane < HEAD_DIM
    for j in range(2):
        qp = q_ref[:, LANE * j:LANE * (j + 1)].astype(F32)
        kp = k_ref[0, :, LANE * j:LANE * (j + 1)].astype(BF16)
        vp = v_ref[0, :, LANE * j:LANE * (j + 1)].astype(BF16)
        outs = []
        for hm in (low, jnp.logical_not(low)):
            qh = jnp.where(hm, qp, 0.0).astype(BF16)
            s = lax.dot_general(qh, kp, (((1,), (1,)), ((), ())), preferred_element_type=F32)
            m = jnp.max(s, axis=-1, keepdims=True)
            e = jnp.exp(s - m)
            l = jnp.sum(e, axis=-1, keepdims=True)
            outs.append(jnp.dot(e.astype(BF16), vp, preferred_element_type=F32) / l)
        o_ref[:, LANE * j:LANE * (j + 1)] = jnp.where(low, outs[0], outs[1]).astype(o_ref.dtype)


def _memattn_prompt(qm, km, vm, batch, seq):
    tiles_per_b = seq // TM
    kv = pl.BlockSpec((1, N_MEM, WIDTH_M), lambda i: (i // tiles_per_b, 0, 0))
    return pl.pallas_call(
        _memattn_body,
        grid=(batch * tiles_per_b,),
        in_specs=[pl.BlockSpec((TM, WIDTH_M), lambda i: (i, 0)), kv, kv],
        out_specs=pl.BlockSpec((TM, WIDTH_M), lambda i: (i, 0)),
        out_shape=jax.ShapeDtypeStruct((batch * seq, WIDTH_M), BF16),
        compiler_params=_cparams(("arbitrary",)),
        name="memattn_prompt",
    )(qm, km, vm)


def _sample_attn_body(*refs):
    for i in range(refs[0].shape[0]):
        _sample_attn_one(i, *refs)


def _sample_attn_one(i, qbd_ref, kt_ref, vt_ref, kn_ref, vn_ref, cnt_ref, cntn_ref, hmask_ref,
                     qmbd_ref, kmt_ref, vmt_ref, hmaskm_ref, ob_ref, om_ref):
    dec = kn_ref.shape[1]
    qbd = qbd_ref[i]
    kt = kt_ref[i].astype(BF16)
    vt = vt_ref[i].astype(BF16)
    s = jnp.dot(qbd, kt, preferred_element_type=F32)
    qf = qbd.astype(F32)
    kn = kn_ref[i]
    vn = vn_ref[i]
    cnt = cnt_ref[...]
    cntn = cntn_ref[...]
    s_new = [jnp.sum(qf * kn[j:j + 1, :], axis=-1, keepdims=True) for j in range(dec)]
    m = jnp.max(jnp.where(cnt > 0, s, NEG), axis=-1, keepdims=True)
    for j in range(dec):
        m = jnp.maximum(m, jnp.where(cntn[:, j:j + 1] > 0, s_new[j], NEG))
    e = cnt * jnp.exp(jnp.where(cnt > 0, s - m, 0.0))
    l = jnp.sum(e, axis=-1, keepdims=True)
    acc = lax.dot_general(e.astype(BF16), vt, (((1,), (1,)), ((), ())), preferred_element_type=F32)
    for j in range(dec):
        w = cntn[:, j:j + 1]
        ej = w * jnp.exp(jnp.where(w > 0, s_new[j] - m, 0.0))
        l = l + ej
        acc = acc + ej * vn[j:j + 1, :]
    r = acc / l * hmask_ref[...]
    out = r[0:8]
    for h in range(1, WIDTH_B // HEAD_DIM):
        out = out + r[8 * h:8 * h + 8]
    ob_ref[i] = out
    qm = qmbd_ref[i]
    sm = jnp.dot(qm, kmt_ref[i].astype(BF16), preferred_element_type=F32)
    mm = jnp.max(sm, axis=-1, keepdims=True)
    em = jnp.exp(sm - mm)
    lm = jnp.sum(em, axis=-1, keepdims=True)
    am = lax.dot_general(em.astype(BF16), vmt_ref[i].astype(BF16), (((1,), (1,)), ((), ())),
                         preferred_element_type=F32)
    rm = am / lm * hmaskm_ref[...]
    outm = rm[0:8]
    for h in range(1, WIDTH_M // HEAD_DIM):
        outm = outm + rm[8 * h:8 * h + 8]
    om_ref[i] = outm


def _sample_counts(dec, w_buf):
    t = np.arange(8)[:, None]
    t = np.where(t < dec, t, 0)
    def mult(dist):
        c = np.zeros(dist.shape, np.float32)
        for window, dil in DILATIONS:
            c += ((dist >= 0) & (dist % dil == 0) & (dist <= window)).astype(np.float32)
        return c
    cache = mult(w_buf + t - np.arange(w_buf)[None, :])
    new = mult(t - np.arange(dec)[None, :])
    nb, nm = WIDTH_B // HEAD_DIM, WIDTH_M // HEAD_DIM
    hmask = (np.arange(8 * nb)[:, None] // 8 == np.arange(WIDTH_B)[None, :] // HEAD_DIM).astype(np.float32)
    hmaskm = (np.arange(8 * nm)[:, None] // 8 == np.arange(WIDTH_M)[None, :] // HEAD_DIM).astype(np.float32)
    return np.tile(cache, (nb, 1)), np.tile(new, (nb, 1)), hmask, hmaskm


def _block_diag_queries(q, dec, hmask):
    width = q.shape[-1]
    nh = width // HEAD_DIM
    qb = q.reshape(-1, 1, dec, width)
    qb = jnp.pad(qb, ((0, 0), (0, 0), (0, 8 - dec), (0, 0)))
    qb = jnp.broadcast_to(qb, (qb.shape[0], nh, 8, width)).reshape(-1, 8 * nh, width)
    return (qb * hmask[None]).astype(BF16)


def _sample_attn_parts(q, kn, vn, qm, kt, vt, kmt, vmt, dec):
    bd = kt.shape[0]
    w_buf = kt.shape[-1]
    cnt, cntn, hmask, hmaskm = _sample_counts(dec, w_buf)
    qbd = _block_diag_queries(q, dec, hmask)
    qmbd = _block_diag_queries(qm.astype(F32), dec, hmaskm)
    nb8, nm8 = qbd.shape[1], qmbd.shape[1]
    per_b = lambda shape: pl.BlockSpec((SAMPLE_SEQS,) + shape, lambda b: (b,) + (0,) * len(shape))
    full = lambda shape: pl.BlockSpec(shape, lambda b: (0,) * len(shape))
    return dict(
        steps=bd // SAMPLE_SEQS,
        in_specs=[per_b((nb8, WIDTH_B)), per_b((WIDTH_B, w_buf)), per_b((WIDTH_B, w_buf)),
                  per_b((dec, WIDTH_B)), per_b((dec, WIDTH_B)),
                  full((nb8, w_buf)), full((nb8, dec)), full((nb8, WIDTH_B)),
                  per_b((nm8, WIDTH_M)), per_b((WIDTH_M, N_MEM)), per_b((WIDTH_M, N_MEM)), full((nm8, WIDTH_M))],
        args=[qbd, kt, vt, kn.reshape(bd, dec, WIDTH_B), vn.reshape(bd, dec, WIDTH_B),
              jnp.asarray(cnt), jnp.asarray(cntn), jnp.asarray(hmask), qmbd, kmt, vmt, jnp.asarray(hmaskm)],
        out_specs=[per_b((8, WIDTH_B)), per_b((8, WIDTH_M))],
        out_shape=[jax.ShapeDtypeStruct((bd, 8, WIDTH_B), F32), jax.ShapeDtypeStruct((bd, 8, WIDTH_M), F32)])


def _sample_attn_rows(ob, om, dec):
    return (ob[:, :dec].reshape(-1, WIDTH_B).astype(BF16), om[:, :dec].reshape(-1, WIDTH_M).astype(BF16))


def _post_body(*refs, tc, aliased):
    (x_ref, u_ref, va_ref, ob_ref, om_ref, wg_ref, bg_ref, wout_ref, g2_ref, wr_ref, br_ref,
     tri_ref, upper_ref) = refs[:13]
    x1_ref, xs_ref, ld_ref, tg_ref, seg_ref = refs[13 + aliased:]
    tm = x_ref.shape[0]
    u = u_ref[...]
    if tc is None:
        vaf = va_ref[...]
        mixed = wg_ref[0] * vaf + bg_ref[...]
        for s in range(1, wg_ref.shape[0]):
            mixed = mixed + wg_ref[s] * pltpu.roll(vaf, s, 0)
        oa = u * mixed
    else:
        lane = lax.broadcasted_iota(jnp.int32, (tc, LANE), 1)
        low = lane < HEAD_DIM
        va = va_ref[...].astype(BF16)
        oa_rows = []
        for c in range(tm // tc):
            r0 = c * tc
            tiles = []
            for p in range(3):
                vp = va[r0:r0 + tc, LANE * p:LANE * (p + 1)]
                r = jnp.dot(wg_ref[p], vp, preferred_element_type=F32)
                tiles.append(jnp.where(low, r[:tc], r[tc:]))
            mixed = jnp.concatenate(tiles, axis=1) + bg_ref[...]
            oa_rows.append(u[r0:r0 + tc] * mixed)
        oa = jnp.concatenate(oa_rows, axis=0)
    mixed_all = jnp.concatenate([oa.astype(BF16), ob_ref[...], om_ref[...]], axis=1)
    x1 = x_ref[...] + jnp.dot(mixed_all, wout_ref[...], preferred_element_type=F32)
    x1_ref[...] = x1
    ms = jnp.mean(x1 * x1, axis=-1, keepdims=True)
    h2 = x1 * lax.rsqrt(ms + EPS) * g2_ref[...]
    h_hi = h2.astype(BF16)
    h_lo = (h2 - h_hi.astype(F32)).astype(BF16)
    hw = jnp.dot(h_hi, wr_ref[...], preferred_element_type=F32)
    logits = (hw[:, :LANE] + hw[:, LANE:]
              + jnp.dot(h_lo, wr_ref[:, :LANE], preferred_element_type=F32)) + br_ref[...]
    lane_i = lax.broadcasted_iota(jnp.int32, (tm, LANE), 1)
    lane_r = lane_i.astype(F32)
    vals = logits
    tops, idxs = [], []
    for _ in range(TOP_K):
        mk = jnp.max(vals, axis=-1, keepdims=True)
        ik = jnp.min(jnp.where(vals == mk, lane_r, float(LANE)), axis=-1, keepdims=True)
        vals = jnp.where(lane_r == ik, -jnp.inf, vals)
        tops.append(mk)
        idxs.append(ik)
    es = [jnp.exp(t - tops[0]) for t in tops]
    den = es[0] + es[1] + es[2] + es[3]
    tg = jnp.zeros((tm, LANE), F32)
    for k in range(TOP_K):
        tg = jnp.where(lane_i == k, es[k] / den, tg)
    tg_ref[...] = tg
    col = lax.broadcasted_iota(jnp.int32, (MOE_TM, SEG_ROWS), 1).astype(F32)
    lane_t = lax.broadcasted_iota(jnp.int32, (MOE_TM, LANE), 1)
    hot_all = [lane_r == idxs[k] for k in range(TOP_K)]
    for hf in range(tm // MOE_TM):
        r0 = hf * MOE_TM
        hot = [h[r0:r0 + MOE_TM] for h in hot_all]
        sel = jnp.zeros((MOE_TM, LANE), F32)
        for k in range(TOP_K):
            sel = sel + jnp.where(hot[k], 1.0, 0.0)
        rank = jnp.dot(tri_ref[...], sel.astype(BF16), preferred_element_type=F32)
        length = jnp.sum(sel, axis=0, keepdims=True)
        plen = jnp.floor((length + (PIECE - 1)) * (1.0 / PIECE)) * PIECE
        loff = jnp.dot(jnp.broadcast_to(plen, (8, LANE)).astype(BF16), upper_ref[...],
                       preferred_element_type=F32)[0:1]
        base = loff + rank
        q = jnp.zeros((MOE_TM, SEG_ROWS), F32)
        ld = jnp.zeros((MOE_TM, LANE), F32)
        for k in range(TOP_K):
            ld_k = jnp.sum(jnp.where(hot[k], base, 0.0), axis=-1, keepdims=True)
            q = q + jnp.where(col == ld_k, 1.0, 0.0)
            ld = jnp.where(lane_t == k, ld_k, ld)
        xs_ref[hf * SEG_ROWS:(hf + 1) * SEG_ROWS, :] = lax.dot_general(
            q.astype(BF16), h_hi[r0:r0 + MOE_TM], (((0,), (0,)), ((), ())),
            preferred_element_type=F32).astype(xs_ref.dtype)
        ld_ref[r0:r0 + MOE_TM, :] = ld.astype(jnp.int32)
        seg_ref[hf * 8:(hf + 1) * 8, :] = jnp.broadcast_to(length, (8, LANE)).astype(jnp.int32)


def _post(x, u, va, ob, om, wg, bg, w, tc, total_tiles, tile0=0, xs_all=None):
    t = x.shape[0]
    nt = t // TM
    full = lambda shape: pl.BlockSpec(shape, lambda i: (0,) * len(shape))
    row = lambda width: pl.BlockSpec((TM, width), lambda i: (i, 0))
    per = TM // MOE_TM
    ix = np.arange(MOE_TM)
    tri = jnp.asarray(ix[:, None] > ix[None, :], BF16)
    ex = np.arange(LANE)
    upper = jnp.asarray(ex[:, None] < ex[None, :], BF16)
    step0 = tile0 // per
    in_specs = [row(D_MODEL), row(WIDTH_A), row(WIDTH_A), row(WIDTH_B), row(WIDTH_M),
                full(wg.shape), full(bg.shape), full((D_MODEL, D_MODEL)), full((1, D_MODEL)),
                full((D_MODEL, 2 * LANE)), full((1, LANE)), full((MOE_TM, MOE_TM)), full((LANE, LANE))]
    args = [x, u, va, ob, om, wg, bg, w["w_out"], w["g2"], w["w_router"], w["b_router"], tri, upper]
    aliases = {}
    if xs_all is not None:
        in_specs.append(pl.BlockSpec(memory_space=pl.ANY))
        args.append(xs_all)
        aliases = {len(args) - 1: 1}
    return pl.pallas_call(
        functools.partial(_post_body, tc=tc, aliased=int(xs_all is not None)),
        grid=(nt,),
        in_specs=in_specs,
        out_specs=[row(D_MODEL), pl.BlockSpec((per * SEG_ROWS, D_MODEL), lambda i: (i + step0, 0)),
                   row(LANE), row(LANE), pl.BlockSpec((per * 8, LANE), lambda i: (i, 0))],
        out_shape=[jax.ShapeDtypeStruct((t, D_MODEL), F32),
                   jax.ShapeDtypeStruct((total_tiles * SEG_ROWS, D_MODEL), F32),
                   jax.ShapeDtypeStruct((t, LANE), jnp.int32), jax.ShapeDtypeStruct((t, LANE), F32),
                   jax.ShapeDtypeStruct((t // MOE_TM * 8, LANE), jnp.int32)],
        input_output_aliases=aliases,
        compiler_params=_cparams(("arbitrary",)),
        name="post",
    )(*args)


def _experts_body(te_ref, valid_ref, first_ref, next_ref, half_ref, src_ref, dst_ref,
                  xs_hbm, wgu_hbm, bgu_ref, wd_hbm, bd_ref, ys_hbm,
                  xbuf, ybuf, wgu_f, wd_f, wgu_s, wd_s, gsem, ssem, wsem):
    t = pl.program_id(0)
    nt = pl.num_programs(0)
    slot = t % 2

    def gather(tile, sl):
        for i in range(NPIECE):
            s = src_ref[tile * NPIECE + i]
            pltpu.make_async_copy(xs_hbm.at[pl.ds(pl.multiple_of(s * PIECE, PIECE), PIECE), :],
                                  xbuf.at[sl, pl.ds(i * PIECE, PIECE), :], gsem.at[sl]).start()

    def scatter(tile, sl):
        for i in range(NPIECE):
            d = dst_ref[tile * NPIECE + i]
            pltpu.make_async_copy(ybuf.at[sl, pl.ds(i * PIECE, PIECE), :],
                                  ys_hbm.at[pl.ds(pl.multiple_of(d * PIECE, PIECE), PIECE), :], ssem.at[sl]).start()

    def wait_tile(hbm, buf, sem, sl):
        pltpu.make_async_copy(hbm.at[pl.ds(0, GM_TM), :], buf.at[sl], sem.at[sl]).wait()

    def weight_copies(e):
        return (pltpu.make_async_copy(wgu_hbm.at[e], wgu_f, wsem.at[0]),
                pltpu.make_async_copy(wd_hbm.at[e], wd_f, wsem.at[1]))

    @pl.when(t == 0)
    def _():
        for c in weight_copies(te_ref[0]):
            c.start()
        gather(0, 0)
        ybuf[...] = jnp.zeros(ybuf.shape, ybuf.dtype)

    @pl.when(valid_ref[t] > 0)
    def _():
        nxt = jnp.minimum(t + 1, nt - 1)
        has_next = jnp.logical_and(t + 1 < nt, valid_ref[nxt] > 0)

        @pl.when(first_ref[t] > 0)
        def _():
            for c in weight_copies(te_ref[t]):
                c.wait()
            wgu_s[...] = wgu_f[...].astype(BF16)
            wd_s[...] = wd_f[...].astype(BF16)

            @pl.when(next_ref[t] >= 0)
            def _():
                for c in weight_copies(next_ref[t]):
                    c.start()

        @pl.when(t >= 2)
        def _():
            wait_tile(ys_hbm, ybuf, ssem, slot)

        gather(jnp.where(has_next, t + 1, t), 1 - slot)
        wait_tile(xs_hbm, xbuf, gsem, slot)

        def ffn(rows):
            x = xbuf[slot, pl.ds(0, rows), :].astype(BF16)
            gu = jnp.dot(x, wgu_s[...], preferred_element_type=F32) + bgu_ref[0]
            gate = jnp.minimum(gu[:, :D_MODEL], SWIGLU_LIMIT)
            up = jnp.clip(gu[:, D_MODEL:], -SWIGLU_LIMIT, SWIGLU_LIMIT)
            act = (up + 1.0) * (gate * (1.0 / (1.0 + jnp.exp(-SWIGLU_ALPHA * gate))))
            y = jnp.dot(act.astype(BF16), wd_s[...], preferred_element_type=F32) + bd_ref[0]
            ybuf[slot, pl.ds(0, rows), :] = y.astype(ybuf.dtype)

        @pl.when(half_ref[t] > 0)
        def _():
            ffn(GM_TM // 2)

        @pl.when(half_ref[t] == 0)
        def _():
            ffn(GM_TM)

        scatter(t, slot)

        @pl.when(jnp.logical_not(has_next))
        def _():
            wait_tile(xs_hbm, xbuf, gsem, 1 - slot)
            wait_tile(ys_hbm, ybuf, ssem, slot)

            @pl.when(t >= 1)
            def _():
                wait_tile(ys_hbm, ybuf, ssem, 1 - slot)


def _experts(plan, xs, w, ys_rows):
    n_tiles = plan["tile_expert"].shape[0]
    by_expert = lambda shape: pl.BlockSpec((1,) + shape, lambda t, te, *_: (te[t],) + (0,) * len(shape))
    hbm = pl.BlockSpec(memory_space=pl.ANY)
    return pl.pallas_call(
        _experts_body,
        grid_spec=pltpu.PrefetchScalarGridSpec(
            num_scalar_prefetch=7, grid=(n_tiles,),
            in_specs=[hbm, hbm, by_expert((1, 2 * D_MODEL)), hbm, by_expert((1, D_MODEL))],
            out_specs=hbm,
            scratch_shapes=[pltpu.VMEM((2, GM_TM, D_MODEL), F32), pltpu.VMEM((2, GM_TM, D_MODEL), BF16),
                            pltpu.VMEM((D_MODEL, 2 * D_MODEL), F32), pltpu.VMEM((D_MODEL, D_MODEL), F32),
                            pltpu.VMEM((D_MODEL, 2 * D_MODEL), BF16), pltpu.VMEM((D_MODEL, D_MODEL), BF16),
                            pltpu.SemaphoreType.DMA((2,)), pltpu.SemaphoreType.DMA((2,)),
                            pltpu.SemaphoreType.DMA((2,))]),
        out_shape=jax.ShapeDtypeStruct((ys_rows, D_MODEL), BF16),
        compiler_params=_cparams(("arbitrary",)),
        name="moe_experts",
    )(plan["tile_expert"], plan["tile_valid"], plan["tile_first"], plan["tile_next"], plan["tile_half"],
      plan["src"], plan["dst"],
      xs, w["w_gate_up"], w["b_gate_up"], w["w_down"], w["b_down"])


def _combine_body(used_ref, ld_ref, g_ref, x1_ref, ys_ref, y_ref, *, tile0):
    t = pl.program_id(0)
    row = lax.broadcasted_iota(jnp.int32, (SEG_ROWS, 1), 0)
    ys = jnp.where(row < used_ref[tile0 + t], ys_ref[...].astype(F32), 0.0).astype(BF16)
    col = lax.broadcasted_iota(jnp.int32, (MOE_TM, SEG_ROWS), 1)
    ld = ld_ref[...]
    g = g_ref[...]
    p = jnp.zeros((MOE_TM, SEG_ROWS), F32)
    for k in range(TOP_K):
        p = jnp.where(col == ld[:, k:k + 1], g[:, k:k + 1], p)
    y_ref[...] = x1_ref[...] + jnp.dot(p.astype(BF16), ys, preferred_element_type=F32)


def _combine(used, ld, gates, x1, ys, tile0):
    n = x1.shape[0]
    nt = n // MOE_TM
    rows = lambda width: pl.BlockSpec((MOE_TM, width), lambda t, *_: (t, 0))
    return pl.pallas_call(
        functools.partial(_combine_body, tile0=tile0),
        grid_spec=pltpu.PrefetchScalarGridSpec(
            num_scalar_prefetch=1, grid=(nt,),
            in_specs=[rows(LANE), rows(LANE), rows(D_MODEL),
                      pl.BlockSpec((SEG_ROWS, D_MODEL), lambda t, *_: (t + tile0, 0))],
            out_specs=rows(D_MODEL)),
        out_shape=jax.ShapeDtypeStruct((n, D_MODEL), F32),
        compiler_params=_cparams(("arbitrary",)),
        name="moe_combine",
    )(used, ld, gates, x1, ys)


def _moe(groups, xs, seglen, w):
    i32 = jnp.int32
    nt = seglen.shape[0]
    plen = (seglen + PIECE - 1) // PIECE * PIECE
    loff = jnp.cumsum(plen, axis=1) - plen
    used = jnp.sum(plen, axis=1).astype(i32)
    pp = plen // PIECE
    cp_end = jnp.cumsum(pp, axis=0)
    cp = cp_end - pp
    cnt_e = cp_end[-1]
    tiles_e = (cnt_e + NPIECE - 1) // NPIECE
    tile_end = jnp.cumsum(tiles_e)
    tile_start = tile_end - tiles_e
    n_tiles = (nt * SEG_ROWS + GM_TM - 1) // GM_TM + N_EXPERTS
    tix = jnp.arange(n_tiles, dtype=i32)
    total_tiles = tile_end[-1]
    tile_valid = (tix < total_tiles).astype(i32)
    expert_at = lambda tile: jnp.minimum((tile[:, None] >= tile_end[None, :]).astype(i32).sum(axis=1), N_EXPERTS - 1)
    last_expert = expert_at(jnp.maximum(total_tiles - 1, 0)[None])[0]
    tile_expert = jnp.where(tile_valid > 0, expert_at(tix), last_expert)
    hot_e = tile_expert[:, None] == jnp.arange(N_EXPERTS, dtype=i32)[None, :]
    per_tile = lambda v: jnp.sum(jnp.where(hot_e, v[None, :], 0), axis=1)
    per_tile_rows = lambda m: jnp.sum(jnp.where(hot_e[:, :, None], jnp.transpose(m)[None], 0), axis=1)
    start_t = per_tile(tile_start)
    tile_first = ((tix == start_t) & (tile_valid > 0)).astype(i32)
    following = per_tile(tile_end)
    tile_next = jnp.where(following < total_tiles, expert_at(following), -1)
    j = (tix - start_t)[:, None] * NPIECE + jnp.arange(NPIECE, dtype=i32)[None, :]
    cnt_t = per_tile(cnt_e)
    ok = (tile_valid[:, None] > 0) & (j < cnt_t[:, None])
    tile_half = ((tile_valid > 0) & (cnt_t - (tix - start_t) * NPIECE <= NPIECE // 2)).astype(i32)
    ends_t, cp_t, loff_t = per_tile_rows(cp_end), per_tile_rows(cp), per_tile_rows(loff)
    t_q = jnp.minimum((ends_t[:, None, :] <= j[:, :, None]).astype(i32).sum(axis=2), nt - 1)
    hot_t = t_q[:, :, None] == jnp.arange(nt, dtype=i32)[None, None, :]
    at_t = lambda m: jnp.sum(jnp.where(hot_t, m[:, None, :], 0), axis=2)
    piece = t_q * SEG_PIECES + at_t(loff_t) // PIECE + j - at_t(cp_t)
    src = jnp.where(ok, piece, 0)
    dump = nt * SEG_PIECES + (tix % 2)[:, None] * NPIECE + jnp.arange(NPIECE, dtype=i32)[None, :]
    dst = jnp.where(ok, piece, dump)
    plan = dict(tile_expert=tile_expert.astype(i32), tile_valid=tile_valid, tile_first=tile_first,
                tile_next=tile_next.astype(i32), tile_half=tile_half,
                src=src.reshape(-1).astype(i32), dst=dst.reshape(-1).astype(i32))

    ys = _experts(plan, xs, w, nt * SEG_ROWS + 2 * GM_TM)
    outs, r0 = [], 0
    for x1, ld, tg in groups:
        outs.append(_combine(used, ld, tg, x1, ys, r0 // MOE_TM))
        r0 += x1.shape[0]
    return outs


def _pair_major_to_rows(a3):
    return jnp.transpose(a3, (1, 0, 2)).reshape(a3.shape[1], 3 * LANE)


def kernel(x_prompt, x_sample, mem_prompt, cache_win_k, cache_win_v, cache_mem_k, cache_mem_v, norm1_g, w_in, gv_a, w_s, b_s, gq_b, gk_b, gq_m, gk_m, mem_norm_g, w_mem_kv, w_out, norm2_g, w_router, b_router, w_gate_up, b_gate_up, w_down, b_down):
    batch, seq, _ = x_prompt.shape
    bd, dec, _ = x_sample.shape
    depth = norm1_g.shape[0]
    assert depth == 1 and seq % SPAN == 0 and (bd * dec) % TM == 0 and PAST_LEN % CHUNK == 0
    w_buf = cache_win_k.shape[2]
    assert w_buf == MAX_WINDOW and dec <= 8
    l = 0
    two = lambda g: jnp.concatenate([g, g])[None, :]
    head = np.arange(LANE) // HEAD_DIM
    head2 = np.arange(2 * LANE) // HEAD_DIM
    wr = jnp.pad(w_router[l], ((0, 0), (0, LANE - N_EXPERTS)))
    wr_hi = wr.astype(BF16)
    wr_lo = (wr - wr_hi.astype(F32)).astype(BF16)
    w = dict(
        g1=norm1_g[l][None], w_in=w_in[l].astype(BF16), gva=gv_a[l][None],
        gq=two(gq_b[l]), gk=two(gk_b[l]), gqm=two(gq_m[l]), gkm=two(gk_m[l]),
        bd=jnp.asarray(head[:, None] == head[None, :], BF16),
        bd2=jnp.asarray(head2[:, None] == head2[None, :], BF16),
        gmem=mem_norm_g[l][None], w_mem_kv=w_mem_kv[l].astype(BF16),
        w_out=w_out[l].astype(BF16), g2=norm2_g[l][None],
        w_router=jnp.concatenate([wr_hi, wr_lo], axis=1),
        b_router=jnp.pad(b_router[l], (0, LANE - N_EXPERTS), constant_values=-jnp.inf)[None],
        w_gate_up=w_gate_up[l], b_gate_up=b_gate_up[l][:, None, :], w_down=w_down[l], b_down=b_down[l][:, None, :],
    )
    ngrp = WIDTH_A // HEAD_DIM
    wtri = jnp.where(jnp.tril(jnp.ones((CHUNK, CHUNK), bool)), w_s[l], 0).astype(BF16)
    wg_p = wtri.reshape(ngrp // 2, 2 * CHUNK, CHUNK)
    bg_p = jnp.repeat(jnp.transpose(b_s[l]), HEAD_DIM, axis=1)
    zero = jnp.zeros((ngrp,), F32)
    lanes = lambda tg_: jnp.tile(jnp.repeat(tg_, HEAD_DIM, axis=1), (bd, 1))
    wg_s = jnp.stack([lanes(jnp.stack([w_s[l][:, t, t - s] if t >= s else zero for t in range(dec)]))
                      for s in range(dec)])
    bg_s = lanes(jnp.transpose(b_s[l][:, :dec]))

    xs = x_sample.reshape(bd * dec, D_MODEL)
    tabs_s = _rope_tables(np.tile(PAST_LEN + np.arange(dec), bd))
    u_s, va_s, q3_s, k3_s, v3_s, qm_s = _premix(xs, tabs_s, bd * dec, w)
    q_s, k_s, v_s = (_pair_major_to_rows(a) for a in (q3_s, k3_s, v3_s))
    nb = WIDTH_B // HEAD_DIM
    nm = WIDTH_M // HEAD_DIM
    kt = jnp.transpose(cache_win_k[l], (0, 2, 3, 1)).reshape(bd, WIDTH_B, w_buf)
    vt = jnp.transpose(cache_win_v[l], (0, 2, 3, 1)).reshape(bd, WIDTH_B, w_buf)
    kmt = jnp.transpose(cache_mem_k[l], (0, 2, 3, 1)).reshape(bd, WIDTH_M, N_MEM)
    vmt = jnp.transpose(cache_mem_v[l], (0, 2, 3, 1)).reshape(bd, WIDTH_M, N_MEM)
    sample_parts = _sample_attn_parts(q_s, k_s, v_s, qm_s, kt, vt, kmt, vmt, dec)

    xp = x_prompt.reshape(batch * seq, D_MODEL)
    tabs_p = _rope_tables(np.arange(seq))
    premix_tm = batch * seq // sample_parts["steps"]
    assert seq % premix_tm == 0 and premix_tm % 8 == 0
    (u_p, va_p, q3_p, k3_p, v3_p, qm_p), (ob_s8, om_s8) = _premix_with_sample_attn(
        _premix_parts(xp, tabs_p, seq, w, premix_tm), sample_parts)
    ob_s, om_s = _sample_attn_rows(ob_s8, om_s8, dec)
    ob_p = _attn_prompt(q3_p, k3_p, v3_p, batch, seq)
    km, vm = _memkv(mem_prompt.reshape(batch * N_MEM, D_MODEL), w)
    om_p = _memattn_prompt(qm_p, km.reshape(batch, N_MEM, WIDTH_M), vm.reshape(batch, N_MEM, WIDTH_M), batch, seq)
    tiles_p = batch * seq // MOE_TM
    tiles_all = tiles_p + bd * dec // MOE_TM
    x1_p, xs_all, ld_p, tg_p, seg_p = _post(xp, u_p, va_p, ob_p, om_p, wg_p, bg_p, w, CHUNK, tiles_all)

    x1_s, xs_all, ld_s, tg_s, seg_s = _post(xs, u_s, va_s, ob_s, om_s, wg_s, bg_s, w, None, tiles_all,
                                             tile0=tiles_p, xs_all=xs_all)

    seglen = jnp.concatenate([seg_p, seg_s])[::8, :N_EXPERTS]
    y_p, y_s = _moe([(x1_p, ld_p, tg_p), (x1_s, ld_s, tg_s)], xs_all, seglen, w)
    y_prompt = y_p.reshape(batch, seq, D_MODEL)
    y_sample = y_s.reshape(bd, dec, D_MODEL)

    n_keep = min(MAX_WINDOW, seq)

    def window_rows(a3):
        a = a3.reshape(3, batch, seq, LANE)[:, :, seq - n_keep:]
        return jnp.transpose(a, (1, 2, 0, 3)).reshape(1, batch, n_keep, nb, HEAD_DIM)

    return (y_prompt, y_sample,
            window_rows(k3_p), window_rows(v3_p),
            km.reshape(1, batch, N_MEM, nm, HEAD_DIM), vm.reshape(1, batch, N_MEM, nm, HEAD_DIM),
            k_s.reshape(1, bd, dec, nb, HEAD_DIM), v_s.reshape(1, bd, dec, nb, HEAD_DIM),
            va_s.reshape(1, bd, dec, WIDTH_A))
```

```python
import functools

import numpy as np
import jax
import jax.numpy as jnp
from jax import lax
from jax.experimental import pallas as pl
from jax.experimental.pallas import tpu as pltpu

F32 = jnp.float32
BF16 = jnp.bfloat16

D_MODEL = 1024
HEAD_DIM = 64
WIDTH_A = 384
WIDTH_B = 384
WIDTH_M = 256
IN_WIDTH = 2 * WIDTH_A + 3 * WIDTH_B + WIDTH_M
CHUNK = 128
DILATIONS = ((128, 1), (512, 4), (2048, 16))
N_SUB = 128
MAX_WINDOW = 2048
N_MEM = 256
ROPE_THETA = 500000.0
ROT_HALF = 8
SCALE = HEAD_DIM ** -0.5
N_EXPERTS = 32
TOP_K = 4
SWIGLU_LIMIT = 7.0
SWIGLU_ALPHA = 1.702
EPS = 1e-6
PAST_LEN = 8192

LANE = 128
NEG = -1e30
TM = 512
SPAN = 2048
QB = 128
SAMPLE_SEQS = 2
MOE_TM = 256
GM_TM = 512
PIECE = 8
NPIECE = GM_TM // PIECE
SEG_ROWS = TOP_K * MOE_TM + N_EXPERTS * PIECE
SEG_PIECES = SEG_ROWS // PIECE
VMEM_LIMIT = 52 * 1024 * 1024


def _cparams(sem):
    return pltpu.CompilerParams(dimension_semantics=sem, vmem_limit_bytes=VMEM_LIMIT)


def _premix_body(x_ref, g1_ref, win_ref, gva_ref, gq_ref, gk_ref, gqm_ref, bd_ref,
                 rc_ref, rs1_ref, rs2_ref,
                 u_ref, va_ref, q3_ref, k3_ref, v3_ref, qm_ref):
    x = x_ref[...]
    tm = x.shape[0]
    ms = jnp.mean(x * x, axis=-1, keepdims=True)
    h = (x * lax.rsqrt(ms + EPS) * g1_ref[...]).astype(BF16)
    z = jnp.dot(h, win_ref[...], preferred_element_type=F32)
    u_ref[...] = z[:, :WIDTH_A].astype(u_ref.dtype)
    va = z[:, WIDTH_A:2 * WIDTH_A]
    va_ms = jnp.mean(va * va, axis=-1, keepdims=True)
    va_ref[...] = (va * lax.rsqrt(va_ms + EPS) * gva_ref[...]).astype(va_ref.dtype)
    q0, k0, v0, m0 = 2 * WIDTH_A, 2 * WIDTH_A + WIDTH_B, 2 * WIDTH_A + 2 * WIDTH_B, 2 * WIDTH_A + 3 * WIDTH_B
    tiles = ([z[:, q0 + LANE * j:q0 + LANE * (j + 1)] for j in range(3)]
             + [z[:, k0 + LANE * j:k0 + LANE * (j + 1)] for j in range(3)]
             + [z[:, m0 + LANE * j:m0 + LANE * (j + 1)] for j in range(2)])
    sq = [(t * t).astype(BF16) for t in tiles]
    sq = jnp.concatenate([jnp.concatenate(sq[i:i + 2], axis=1) for i in range(0, 8, 2)], axis=0)
    ssum = jnp.dot(sq, bd_ref[...], preferred_element_type=F32)
    inv = [lax.rsqrt(ssum[(i // 2) * tm:(i // 2 + 1) * tm, (i % 2) * LANE:(i % 2 + 1) * LANE] * (1.0 / HEAD_DIM) + EPS)
           for i in range(8)]
    rc, rs1, rs2 = rc_ref[...], rs1_ref[...], rs2_ref[...]

    def rope(t):
        return t * rc + pltpu.roll(t, LANE - ROT_HALF, 1) * rs1 + pltpu.roll(t, ROT_HALF, 1) * rs2

    for j in range(3):
        q3_ref[j] = rope(tiles[j] * inv[j] * gq_ref[...]) * SCALE
        k3_ref[j] = rope(tiles[3 + j] * inv[3 + j] * gk_ref[...])
        v3_ref[j] = z[:, v0 + LANE * j:v0 + LANE * (j + 1)]
    for j in range(2):
        qm_ref[:, LANE * j:LANE * (j + 1)] = (tiles[6 + j] * inv[6 + j] * gqm_ref[...] * SCALE).astype(BF16)


def _premix_parts(x, tabs, seq, w, tm, gate_dtype=F32):
    t = x.shape[0]
    n_tab_tiles = seq // tm
    full = lambda shape: pl.BlockSpec(shape, lambda i: (0,) * len(shape))
    tab = pl.BlockSpec((tm, LANE), lambda i: (i % n_tab_tiles, 0))
    row = lambda width: pl.BlockSpec((tm, width), lambda i: (i, 0))
    pair = pl.BlockSpec((3, tm, LANE), lambda i: (0, i, 0))
    return dict(
        steps=t // tm,
        in_specs=[row(D_MODEL), full((1, D_MODEL)), full((D_MODEL, IN_WIDTH)), full((1, WIDTH_A)),
                  full((1, LANE)), full((1, LANE)), full((1, LANE)), full((2 * LANE, 2 * LANE)), tab, tab, tab],
        args=[x, w["g1"], w["w_in"], w["gva"], w["gq"], w["gk"], w["gqm"], w["bd2"], *tabs],
        out_specs=[row(WIDTH_A), row(WIDTH_A), pair, pair, pair, row(WIDTH_M)],
        out_shape=[jax.ShapeDtypeStruct((t, WIDTH_A), gate_dtype), jax.ShapeDtypeStruct((t, WIDTH_A), gate_dtype),
                   jax.ShapeDtypeStruct((3, t, LANE), F32), jax.ShapeDtypeStruct((3, t, LANE), F32),
                   jax.ShapeDtypeStruct((3, t, LANE), F32), jax.ShapeDtypeStruct((t, WIDTH_M), BF16)])


def _premix(x, tabs, seq, w):
    p = _premix_parts(x, tabs, seq, w, TM)
    return pl.pallas_call(
        _premix_body, grid=(p["steps"],), in_specs=p["in_specs"], out_specs=p["out_specs"],
        out_shape=p["out_shape"], compiler_params=_cparams(("arbitrary",)), name="premix",
    )(*p["args"])


def _premix_sample_body(*refs, n_in, n_out):
    ins_a, ins_b = refs[:n_in[0]], refs[n_in[0]:n_in[0] + n_in[1]]
    outs = refs[n_in[0] + n_in[1]:]
    _premix_body(*ins_a, *outs[:n_out])
    _sample_attn_body(*ins_b, *outs[n_out:])


def _premix_with_sample_attn(a, b):
    assert a["steps"] == b["steps"]
    outs = pl.pallas_call(
        functools.partial(_premix_sample_body, n_in=(len(a["args"]), len(b["args"])), n_out=len(a["out_shape"])),
        grid=(a["steps"],), in_specs=a["in_specs"] + b["in_specs"], out_specs=a["out_specs"] + b["out_specs"],
        out_shape=a["out_shape"] + b["out_shape"], compiler_params=_cparams(("arbitrary",)),
        name="premix_sample_attn",
    )(*a["args"], *b["args"])
    return outs[:len(a["out_shape"])], outs[len(a["out_shape"]):]


def _rope_tables(pos):
    pos = np.asarray(pos, np.float64)
    inv_freq = np.power(ROPE_THETA, -np.arange(ROT_HALF, dtype=np.float64) / ROT_HALF)
    ang = pos[:, None] * inv_freq[None, :]
    cos, sin = np.cos(ang), np.sin(ang)
    t = pos.shape[0]
    rest = HEAD_DIM - 2 * ROT_HALF
    c = np.concatenate([cos, cos, np.ones((t, rest))], axis=1)
    s1 = np.concatenate([-sin, np.zeros((t, HEAD_DIM - ROT_HALF))], axis=1)
    s2 = np.concatenate([np.zeros((t, ROT_HALF)), sin, np.zeros((t, rest))], axis=1)
    two = lambda a: jnp.asarray(np.concatenate([a, a], axis=1), F32)
    return two(c), two(s1), two(s2)


def _attn_body(q_ref, kc_ref, kp_ref, vc_ref, vp_ref, o_ref, m_s, l_s, a_s):
    span_idx = pl.program_id(1)
    p0 = span_idx * SPAN
    lane = lax.broadcasted_iota(jnp.int32, (QB, LANE), 1)
    low = lane < HEAD_DIM
    qi = lax.broadcasted_iota(jnp.int32, (QB, 2 * QB), 0)
    kj = lax.broadcasted_iota(jnp.int32, (QB, 2 * QB), 1)
    band = (kj >= qi) & (kj <= qi + N_SUB)

    def rows(ref, start, d):
        if d == 1:
            return ref[0, pl.ds(start, QB), :]
        return ref[0, pl.ds(start, QB, stride=d), :]

    def unit(d, res, blk, first):
        qstart = res + d * QB * blk
        if blk == 0:
            older = (kp_ref, vp_ref, SPAN - QB * d + res)
            first_key = jnp.maximum((QB * d - p0 - res + d - 1) // d, 0)
            mask = band & (kj >= first_key)
        else:
            older = (kc_ref, vc_ref, qstart - QB * d)
            mask = band
        qb = rows(q_ref, qstart, d)
        kb = jnp.concatenate([rows(older[0], older[2], d), rows(kc_ref, qstart, d)], axis=0).astype(BF16)
        vb = jnp.concatenate([rows(older[1], older[2], d), rows(vc_ref, qstart, d)], axis=0).astype(BF16)
        stats = []
        for hm in (low, jnp.logical_not(low)):
            qh = jnp.where(hm, qb, 0.0).astype(BF16)
            s = lax.dot_general(qh, kb, (((1,), (1,)), ((), ())), preferred_element_type=F32)
            s = jnp.where(mask, s, NEG)
            m = jnp.max(s, axis=-1, keepdims=True)
            e = jnp.exp(s - m)
            l = jnp.sum(e, axis=-1, keepdims=True)
            acc = jnp.dot(e.astype(BF16), vb, preferred_element_type=F32)
            stats.append((m, l, acc))
        m_new = jnp.where(low, stats[0][0], stats[1][0])
        l_new = jnp.where(low, stats[0][1], stats[1][1])
        a_new = jnp.where(low, stats[0][2], stats[1][2])
        sl = (pl.ds(qstart, QB) if d == 1 else pl.ds(qstart, QB, stride=d), slice(None))
        if first:
            m_s[sl] = m_new
            l_s[sl] = l_new
            a_s[sl] = a_new
        else:
            m_old, l_old, a_old = m_s[sl], l_s[sl], a_s[sl]
            m_t = jnp.maximum(m_old, m_new)
            wa = jnp.exp(m_old - m_t)
            wb = jnp.exp(m_new - m_t)
            m_s[sl] = m_t
            l_s[sl] = wa * l_old + wb * l_new
            a_s[sl] = wa * a_old + wb * a_new

    nblk = SPAN // QB
    first = True
    for _, d in DILATIONS:
        per_res = nblk // d
        for res in range(d):
            for blk in range(per_res):
                unit(d, res, blk, first)
        first = False
    o_ref[...] = (a_s[...] / l_s[...]).astype(o_ref.dtype)


def _attn_prompt(q3, k3, v3, batch, seq):
    nspan = seq // SPAN
    cur = pl.BlockSpec((1, SPAN, LANE), lambda b, s, p: (p, b * nspan + s, 0))
    prv = pl.BlockSpec((1, SPAN, LANE), lambda b, s, p: (p, b * nspan + jnp.maximum(s - 1, 0), 0))
    return pl.pallas_call(
        _attn_body,
        grid=(batch, nspan, 3),
        in_specs=[cur, cur, prv, cur, prv],
        out_specs=pl.BlockSpec((SPAN, LANE), lambda b, s, p: (b * nspan + s, p)),
        out_shape=jax.ShapeDtypeStruct((batch * seq, WIDTH_B), BF16),
        scratch_shapes=[pltpu.VMEM((SPAN, LANE), F32)] * 3,
        compiler_params=_cparams(("arbitrary", "arbitrary", "arbitrary")),
        name="attn_prompt",
    )(q3, k3, k3, v3, v3)


def _memkv_body(mem_ref, g_ref, w_ref, gk_ref, bd_ref, k_ref, v_ref):
    x = mem_ref[...]
    ms = jnp.mean(x * x, axis=-1, keepdims=True)
    h = (x * lax.rsqrt(ms + EPS) * g_ref[...]).astype(BF16)
    kv = jnp.dot(h, w_ref[...], preferred_element_type=F32)
    n = x.shape[0]
    kt = [kv[:, LANE * j:LANE * (j + 1)] for j in range(2)]
    sq = jnp.concatenate([(t * t).astype(BF16) for t in kt], axis=0)
    ssum = jnp.dot(sq, bd_ref[...], preferred_element_type=F32)
    for j in range(2):
        inv = lax.rsqrt(ssum[j * n:(j + 1) * n] * (1.0 / HEAD_DIM) + EPS)
        k_ref[:, LANE * j:LANE * (j + 1)] = kt[j] * inv * gk_ref[...]
    v_ref[...] = kv[:, WIDTH_M:]


def _memkv(mem, w):
    n = mem.shape[0]
    return pl.pallas_call(
        _memkv_body,
        out_shape=[jax.ShapeDtypeStruct((n, WIDTH_M), F32)] * 2,
        compiler_params=pltpu.CompilerParams(vmem_limit_bytes=VMEM_LIMIT),
        name="memkv",
    )(mem, w["gmem"], w["w_mem_kv"], w["gkm"], w["bd"])


def _memattn_body(q_ref, k_ref, v_ref, o_ref):
    lane = lax.broadcasted_iota(jnp.int32, (q_ref.shape[0], LANE), 1)
    low = lane < HEAD_DIM
    for j in range(2):
        qp = q_ref[:, LANE * j:LANE * (j + 1)].astype(F32)
        kp = k_ref[0, :, LANE * j:LANE * (j + 1)].astype(BF16)
        vp = v_ref[0, :, LANE * j:LANE * (j + 1)].astype(BF16)
        outs = []
        for hm in (low, jnp.logical_not(low)):
            qh = jnp.where(hm, qp, 0.0).astype(BF16)
            s = lax.dot_general(qh, kp, (((1,), (1,)), ((), ())), preferred_element_type=F32)
            m = jnp.max(s, axis=-1, keepdims=True)
            e = jnp.exp(s - m)
            l = jnp.sum(e, axis=-1, keepdims=True)
            outs.append(jnp.dot(e.astype(BF16), vp, preferred_element_type=F32) / l)
        o_ref[:, LANE * j:LANE * (j + 1)] = jnp.where(low, outs[0], outs[1]).astype(o_ref.dtype)


def _memattn_prompt(qm, km, vm, batch, seq):
    tiles_per_b = seq // TM
    kv = pl.BlockSpec((1, N_MEM, WIDTH_M), lambda i: (i // tiles_per_b, 0, 0))
    return pl.pallas_call(
        _memattn_body,
        grid=(batch * tiles_per_b,),
        in_specs=[pl.BlockSpec((TM, WIDTH_M), lambda i: (i, 0)), kv, kv],
        out_specs=pl.BlockSpec((TM, WIDTH_M), lambda i: (i, 0)),
        out_shape=jax.ShapeDtypeStruct((batch * seq, WIDTH_M), BF16),
        compiler_params=_cparams(("arbitrary",)),
        name="memattn_prompt",
    )(qm, km, vm)


def _sample_attn_body(*refs):
    for i in range(refs[0].shape[0]):
        _sample_attn_one(i, *refs)


def _sample_attn_one(i, qbd_ref, kt_ref, vt_ref, kn_ref, vn_ref, cnt_ref, cntn_ref, hmask_ref,
                     qmbd_ref, kmt_ref, vmt_ref, hmaskm_ref, ob_ref, om_ref):
    dec = kn_ref.shape[1]
    qbd = qbd_ref[i]
    kt = kt_ref[i].astype(BF16)
    vt = vt_ref[i].astype(BF16)
    s = jnp.dot(qbd, kt, preferred_element_type=F32)
    qf = qbd.astype(F32)
    kn = kn_ref[i]
    vn = vn_ref[i]
    cnt = cnt_ref[...]
    cntn = cntn_ref[...]
    s_new = [jnp.sum(qf * kn[j:j + 1, :], axis=-1, keepdims=True) for j in range(dec)]
    m = jnp.max(jnp.where(cnt > 0, s, NEG), axis=-1, keepdims=True)
    for j in range(dec):
        m = jnp.maximum(m, jnp.where(cntn[:, j:j + 1] > 0, s_new[j], NEG))
    e = cnt * jnp.exp(jnp.where(cnt > 0, s - m, 0.0))
    l = jnp.sum(e, axis=-1, keepdims=True)
    acc = lax.dot_general(e.astype(BF16), vt, (((1,), (1,)), ((), ())), preferred_element_type=F32)
    for j in range(dec):
        w = cntn[:, j:j + 1]
        ej = w * jnp.exp(jnp.where(w > 0, s_new[j] - m, 0.0))
        l = l + ej
        acc = acc + ej * vn[j:j + 1, :]
    r = acc / l * hmask_ref[...]
    out = r[0:8]
    for h in range(1, WIDTH_B // HEAD_DIM):
        out = out + r[8 * h:8 * h + 8]
    ob_ref[i] = out
    qm = qmbd_ref[i]
    sm = jnp.dot(qm, kmt_ref[i].astype(BF16), preferred_element_type=F32)
    mm = jnp.max(sm, axis=-1, keepdims=True)
    em = jnp.exp(sm - mm)
    lm = jnp.sum(em, axis=-1, keepdims=True)
    am = lax.dot_general(em.astype(BF16), vmt_ref[i].astype(BF16), (((1,), (1,)), ((), ())),
                         preferred_element_type=F32)
    rm = am / lm * hmaskm_ref[...]
    outm = rm[0:8]
    for h in range(1, WIDTH_M // HEAD_DIM):
        outm = outm + rm[8 * h:8 * h + 8]
    om_ref[i] = outm


def _sample_counts(dec, w_buf):
    t = np.arange(8)[:, None]
    t = np.where(t < dec, t, 0)
    def mult(dist):
        c = np.zeros(dist.shape, np.float32)
        for window, dil in DILATIONS:
            c += ((dist >= 0) & (dist % dil == 0) & (dist <= window)).astype(np.float32)
        return c
    cache = mult(w_buf + t - np.arange(w_buf)[None, :])
    new = mult(t - np.arange(dec)[None, :])
    nb, nm = WIDTH_B // HEAD_DIM, WIDTH_M // HEAD_DIM
    hmask = (np.arange(8 * nb)[:, None] // 8 == np.arange(WIDTH_B)[None, :] // HEAD_DIM).astype(np.float32)
    hmaskm = (np.arange(8 * nm)[:, None] // 8 == np.arange(WIDTH_M)[None, :] // HEAD_DIM).astype(np.float32)
    return np.tile(cache, (nb, 1)), np.tile(new, (nb, 1)), hmask, hmaskm


def _block_diag_queries(q, dec, hmask):
    width = q.shape[-1]
    nh = width // HEAD_DIM
    qb = q.reshape(-1, 1, dec, width)
    qb = jnp.pad(qb, ((0, 0), (0, 0), (0, 8 - dec), (0, 0)))
    qb = jnp.broadcast_to(qb, (qb.shape[0], nh, 8, width)).reshape(-1, 8 * nh, width)
    return (qb * hmask[None]).astype(BF16)


def _sample_attn_parts(q, kn, vn, qm, kt, vt, kmt, vmt, dec):
    bd = kt.shape[0]
    w_buf = kt.shape[-1]
    cnt, cntn, hmask, hmaskm = _sample_counts(dec, w_buf)
    qbd = _block_diag_queries(q, dec, hmask)
    qmbd = _block_diag_queries(qm.astype(F32), dec, hmaskm)
    nb8, nm8 = qbd.shape[1], qmbd.shape[1]
    per_b = lambda shape: pl.BlockSpec((SAMPLE_SEQS,) + shape, lambda b: (b,) + (0,) * len(shape))
    full = lambda shape: pl.BlockSpec(shape, lambda b: (0,) * len(shape))
    return dict(
        steps=bd // SAMPLE_SEQS,
        in_specs=[per_b((nb8, WIDTH_B)), per_b((WIDTH_B, w_buf)), per_b((WIDTH_B, w_buf)),
                  per_b((dec, WIDTH_B)), per_b((dec, WIDTH_B)),
                  full((nb8, w_buf)), full((nb8, dec)), full((nb8, WIDTH_B)),
                  per_b((nm8, WIDTH_M)), per_b((WIDTH_M, N_MEM)), per_b((WIDTH_M, N_MEM)), full((nm8, WIDTH_M))],
        args=[qbd, kt, vt, kn.reshape(bd, dec, WIDTH_B), vn.reshape(bd, dec, WIDTH_B),
              jnp.asarray(cnt), jnp.asarray(cntn), jnp.asarray(hmask), qmbd, kmt, vmt, jnp.asarray(hmaskm)],
        out_specs=[per_b((8, WIDTH_B)), per_b((8, WIDTH_M))],
        out_shape=[jax.ShapeDtypeStruct((bd, 8, WIDTH_B), F32), jax.ShapeDtypeStruct((bd, 8, WIDTH_M), F32)])


def _sample_attn_rows(ob, om, dec):
    return (ob[:, :dec].reshape(-1, WIDTH_B).astype(BF16), om[:, :dec].reshape(-1, WIDTH_M).astype(BF16))


def _post_body(*refs, tc, aliased):
    (x_ref, u_ref, va_ref, ob_ref, om_ref, wg_ref, bg_ref, wout_ref, g2_ref, wr_ref, br_ref,
     tri_ref, upper_ref) = refs[:13]
    x1_ref, xs_ref, ld_ref, tg_ref, seg_ref = refs[13 + aliased:]
    tm = x_ref.shape[0]
    u = u_ref[...].astype(F32)
    if tc is None:
        vaf = va_ref[...]
        mixed = wg_ref[0] * vaf + bg_ref[...]
        for s in range(1, wg_ref.shape[0]):
            mixed = mixed + wg_ref[s] * pltpu.roll(vaf, s, 0)
        oa = u * mixed
    else:
        lane = lax.broadcasted_iota(jnp.int32, (tc, LANE), 1)
        low = lane < HEAD_DIM
        va = va_ref[...].astype(BF16)
        oa_rows = []
        for c in range(tm // tc):
            r0 = c * tc
            tiles = []
            for p in range(3):
                vp = va[r0:r0 + tc, LANE * p:LANE * (p + 1)]
                r = jnp.dot(wg_ref[p], vp, preferred_element_type=F32)
                tiles.append(jnp.where(low, r[:tc], r[tc:]))
            mixed = jnp.concatenate(tiles, axis=1) + bg_ref[...]
            oa_rows.append(u[r0:r0 + tc] * mixed)
        oa = jnp.concatenate(oa_rows, axis=0)
    mixed_all = jnp.concatenate([oa.astype(BF16), ob_ref[...], om_ref[...]], axis=1)
    x1 = x_ref[...] + jnp.dot(mixed_all, wout_ref[...], preferred_element_type=F32)
    x1_ref[...] = x1
    ms = jnp.mean(x1 * x1, axis=-1, keepdims=True)
    h2 = x1 * lax.rsqrt(ms + EPS) * g2_ref[...]
    h_hi = h2.astype(BF16)
    h_lo = (h2 - h_hi.astype(F32)).astype(BF16)
    hw = jnp.dot(h_hi, wr_ref[...], preferred_element_type=F32)
    logits = (hw[:, :LANE] + hw[:, LANE:]
              + jnp.dot(h_lo, wr_ref[:, :LANE], preferred_element_type=F32)) + br_ref[...]
    lane_i = lax.broadcasted_iota(jnp.int32, (tm, LANE), 1)
    lane_r = lane_i.astype(F32)
    vals = logits
    tops, idxs = [], []
    for _ in range(TOP_K):
        mk = jnp.max(vals, axis=-1, keepdims=True)
        ik = jnp.min(jnp.where(vals == mk, lane_r, float(LANE)), axis=-1, keepdims=True)
        vals = jnp.where(lane_r == ik, -jnp.inf, vals)
        tops.append(mk)
        idxs.append(ik)
    es = [jnp.exp(t - tops[0]) for t in tops]
    den = es[0] + es[1] + es[2] + es[3]
    tg = jnp.zeros((tm, LANE), F32)
    for k in range(TOP_K):
        tg = jnp.where(lane_i == k, es[k] / den, tg)
    tg_ref[...] = tg
    col = lax.broadcasted_iota(jnp.int32, (MOE_TM, SEG_ROWS), 1).astype(F32)
    lane_t = lax.broadcasted_iota(jnp.int32, (MOE_TM, LANE), 1)
    hot_all = [lane_r == idxs[k] for k in range(TOP_K)]
    for hf in range(tm // MOE_TM):
        r0 = hf * MOE_TM
        hot = [h[r0:r0 + MOE_TM] for h in hot_all]
        sel = jnp.zeros((MOE_TM, LANE), F32)
        for k in range(TOP_K):
            sel = sel + jnp.where(hot[k], 1.0, 0.0)
        rank = jnp.dot(tri_ref[...], sel.astype(BF16), preferred_element_type=F32)
        length = jnp.sum(sel, axis=0, keepdims=True)
        plen = jnp.floor((length + (PIECE - 1)) * (1.0 / PIECE)) * PIECE
        loff = jnp.dot(jnp.broadcast_to(plen, (8, LANE)).astype(BF16), upper_ref[...],
                       preferred_element_type=F32)[0:1]
        base = loff + rank
        q = jnp.zeros((MOE_TM, SEG_ROWS), F32)
        ld = jnp.zeros((MOE_TM, LANE), F32)
        for k in range(TOP_K):
            ld_k = jnp.sum(jnp.where(hot[k], base, 0.0), axis=-1, keepdims=True)
            q = q + jnp.where(col == ld_k, 1.0, 0.0)
            ld = jnp.where(lane_t == k, ld_k, ld)
        xs_ref[hf * SEG_ROWS:(hf + 1) * SEG_ROWS, :] = lax.dot_general(
            q.astype(BF16), h_hi[r0:r0 + MOE_TM], (((0,), (0,)), ((), ())),
            preferred_element_type=F32).astype(xs_ref.dtype)
        ld_ref[r0:r0 + MOE_TM, :] = ld.astype(jnp.int32)
        seg_ref[hf * 8:(hf + 1) * 8, :] = jnp.broadcast_to(length, (8, LANE)).astype(jnp.int32)


def _post(x, u, va, ob, om, wg, bg, w, tc, total_tiles, tile0=0, xs_all=None):
    t = x.shape[0]
    nt = t // TM
    full = lambda shape: pl.BlockSpec(shape, lambda i: (0,) * len(shape))
    row = lambda width: pl.BlockSpec((TM, width), lambda i: (i, 0))
    per = TM // MOE_TM
    ix = np.arange(MOE_TM)
    tri = jnp.asarray(ix[:, None] > ix[None, :], BF16)
    ex = np.arange(LANE)
    upper = jnp.asarray(ex[:, None] < ex[None, :], BF16)
    step0 = tile0 // per
    in_specs = [row(D_MODEL), row(WIDTH_A), row(WIDTH_A), row(WIDTH_B), row(WIDTH_M),
                full(wg.shape), full(bg.shape), full((D_MODEL, D_MODEL)), full((1, D_MODEL)),
                full((D_MODEL, 2 * LANE)), full((1, LANE)), full((MOE_TM, MOE_TM)), full((LANE, LANE))]
    args = [x, u, va, ob, om, wg, bg, w["w_out"], w["g2"], w["w_router"], w["b_router"], tri, upper]
    aliases = {}
    if xs_all is not None:
        in_specs.append(pl.BlockSpec(memory_space=pl.ANY))
        args.append(xs_all)
        aliases = {len(args) - 1: 1}
    return pl.pallas_call(
        functools.partial(_post_body, tc=tc, aliased=int(xs_all is not None)),
        grid=(nt,),
        in_specs=in_specs,
        out_specs=[row(D_MODEL), pl.BlockSpec((per * SEG_ROWS, D_MODEL), lambda i: (i + step0, 0)),
                   row(LANE), row(LANE), pl.BlockSpec((per * 8, LANE), lambda i: (i, 0))],
        out_shape=[jax.ShapeDtypeStruct((t, D_MODEL), F32),
                   jax.ShapeDtypeStruct((total_tiles * SEG_ROWS, D_MODEL), BF16),
                   jax.ShapeDtypeStruct((t, LANE), jnp.int32), jax.ShapeDtypeStruct((t, LANE), F32),
                   jax.ShapeDtypeStruct((t // MOE_TM * 8, LANE), jnp.int32)],
        input_output_aliases=aliases,
        compiler_params=_cparams(("arbitrary",)),
        name="post",
    )(*args)


def _experts_body(te_ref, valid_ref, first_ref, next_ref, half_ref, src_ref, dst_ref,
                  xs_hbm, wgu_hbm, bgu_ref, wd_hbm, bd_ref, ys_hbm,
                  xbuf, ybuf, wgu_f, wd_f, wgu_s, wd_s, gsem, ssem, wsem):
    t = pl.program_id(0)
    nt = pl.num_programs(0)
    slot = t % 2

    def gather(tile, sl):
        for i in range(NPIECE):
            s = src_ref[tile * NPIECE + i]
            pltpu.make_async_copy(xs_hbm.at[pl.ds(pl.multiple_of(s * PIECE, PIECE), PIECE), :],
                                  xbuf.at[sl, pl.ds(i * PIECE, PIECE), :], gsem.at[sl]).start()

    def scatter(tile, sl):
        for i in range(NPIECE):
            d = dst_ref[tile * NPIECE + i]
            pltpu.make_async_copy(ybuf.at[sl, pl.ds(i * PIECE, PIECE), :],
                                  ys_hbm.at[pl.ds(pl.multiple_of(d * PIECE, PIECE), PIECE), :], ssem.at[sl]).start()

    def wait_tile(hbm, buf, sem, sl):
        pltpu.make_async_copy(hbm.at[pl.ds(0, GM_TM), :], buf.at[sl], sem.at[sl]).wait()

    def weight_copies(e):
        return (pltpu.make_async_copy(wgu_hbm.at[e], wgu_f, wsem.at[0]),
                pltpu.make_async_copy(wd_hbm.at[e], wd_f, wsem.at[1]))

    @pl.when(t == 0)
    def _():
        for c in weight_copies(te_ref[0]):
            c.start()
        gather(0, 0)
        ybuf[...] = jnp.zeros(ybuf.shape, ybuf.dtype)

    @pl.when(valid_ref[t] > 0)
    def _():
        nxt = jnp.minimum(t + 1, nt - 1)
        has_next = jnp.logical_and(t + 1 < nt, valid_ref[nxt] > 0)

        @pl.when(first_ref[t] > 0)
        def _():
            for c in weight_copies(te_ref[t]):
                c.wait()
            wgu_s[...] = wgu_f[...].astype(BF16)
            wd_s[...] = wd_f[...].astype(BF16)

            @pl.when(next_ref[t] >= 0)
            def _():
                for c in weight_copies(next_ref[t]):
                    c.start()

        @pl.when(t >= 2)
        def _():
            wait_tile(ys_hbm, ybuf, ssem, slot)

        gather(jnp.where(has_next, t + 1, t), 1 - slot)
        wait_tile(xs_hbm, xbuf, gsem, slot)

        def ffn(rows):
            x = xbuf[slot, pl.ds(0, rows), :]
            gu = jnp.dot(x, wgu_s[...], preferred_element_type=F32) + bgu_ref[0]
            gate = jnp.minimum(gu[:, :D_MODEL], SWIGLU_LIMIT)
            up = jnp.clip(gu[:, D_MODEL:], -SWIGLU_LIMIT, SWIGLU_LIMIT)
            act = (up + 1.0) * (gate * (1.0 / (1.0 + jnp.exp(-SWIGLU_ALPHA * gate))))
            y = jnp.dot(act.astype(BF16), wd_s[...], preferred_element_type=F32) + bd_ref[0]
            ybuf[slot, pl.ds(0, rows), :] = y.astype(ybuf.dtype)

        @pl.when(half_ref[t] > 0)
        def _():
            ffn(GM_TM // 2)

        @pl.when(half_ref[t] == 0)
        def _():
            ffn(GM_TM)

        scatter(t, slot)

        @pl.when(jnp.logical_not(has_next))
        def _():
            wait_tile(xs_hbm, xbuf, gsem, 1 - slot)
            wait_tile(ys_hbm, ybuf, ssem, slot)

            @pl.when(t >= 1)
            def _():
                wait_tile(ys_hbm, ybuf, ssem, 1 - slot)


def _experts(plan, xs, w, ys_rows):
    n_tiles = plan["tile_expert"].shape[0]
    by_expert = lambda shape: pl.BlockSpec((1,) + shape, lambda t, te, *_: (te[t],) + (0,) * len(shape))
    hbm = pl.BlockSpec(memory_space=pl.ANY)
    return pl.pallas_call(
        _experts_body,
        grid_spec=pltpu.PrefetchScalarGridSpec(
            num_scalar_prefetch=7, grid=(n_tiles,),
            in_specs=[hbm, hbm, by_expert((1, 2 * D_MODEL)), hbm, by_expert((1, D_MODEL))],
            out_specs=hbm,
            scratch_shapes=[pltpu.VMEM((2, GM_TM, D_MODEL), BF16), pltpu.VMEM((2, GM_TM, D_MODEL), BF16),
                            pltpu.VMEM((D_MODEL, 2 * D_MODEL), F32), pltpu.VMEM((D_MODEL, D_MODEL), F32),
                            pltpu.VMEM((D_MODEL, 2 * D_MODEL), BF16), pltpu.VMEM((D_MODEL, D_MODEL), BF16),
                            pltpu.SemaphoreType.DMA((2,)), pltpu.SemaphoreType.DMA((2,)),
                            pltpu.SemaphoreType.DMA((2,))]),
        out_shape=jax.ShapeDtypeStruct((ys_rows, D_MODEL), BF16),
        compiler_params=_cparams(("arbitrary",)),
        name="moe_experts",
    )(plan["tile_expert"], plan["tile_valid"], plan["tile_first"], plan["tile_next"], plan["tile_half"],
      plan["src"], plan["dst"],
      xs, w["w_gate_up"], w["b_gate_up"], w["w_down"], w["b_down"])


def _combine_body(used_ref, ld_ref, g_ref, x1_ref, ys_ref, y_ref, *, tile0):
    t = pl.program_id(0)
    row = lax.broadcasted_iota(jnp.int32, (SEG_ROWS, 1), 0)
    ys = jnp.where(row < used_ref[tile0 + t], ys_ref[...].astype(F32), 0.0).astype(BF16)
    col = lax.broadcasted_iota(jnp.int32, (MOE_TM, SEG_ROWS), 1)
    ld = ld_ref[...]
    g = g_ref[...]
    p = jnp.zeros((MOE_TM, SEG_ROWS), F32)
    for k in range(TOP_K):
        p = jnp.where(col == ld[:, k:k + 1], g[:, k:k + 1], p)
    y_ref[...] = x1_ref[...] + jnp.dot(p.astype(BF16), ys, preferred_element_type=F32)


def _combine(used, ld, gates, x1, ys, tile0):
    n = x1.shape[0]
    nt = n // MOE_TM
    rows = lambda width: pl.BlockSpec((MOE_TM, width), lambda t, *_: (t, 0))
    return pl.pallas_call(
        functools.partial(_combine_body, tile0=tile0),
        grid_spec=pltpu.PrefetchScalarGridSpec(
            num_scalar_prefetch=1, grid=(nt,),
            in_specs=[rows(LANE), rows(LANE), rows(D_MODEL),
                      pl.BlockSpec((SEG_ROWS, D_MODEL), lambda t, *_: (t + tile0, 0))],
            out_specs=rows(D_MODEL)),
        out_shape=jax.ShapeDtypeStruct((n, D_MODEL), F32),
        compiler_params=_cparams(("arbitrary",)),
        name="moe_combine",
    )(used, ld, gates, x1, ys)


def _moe(groups, xs, seglen, w):
    i32 = jnp.int32
    nt = seglen.shape[0]
    plen = (seglen + PIECE - 1) // PIECE * PIECE
    loff = jnp.cumsum(plen, axis=1) - plen
    used = jnp.sum(plen, axis=1).astype(i32)
    pp = plen // PIECE
    cp_end = jnp.cumsum(pp, axis=0)
    cp = cp_end - pp
    cnt_e = cp_end[-1]
    tiles_e = (cnt_e + NPIECE - 1) // NPIECE
    tile_end = jnp.cumsum(tiles_e)
    tile_start = tile_end - tiles_e
    n_tiles = (nt * SEG_ROWS + GM_TM - 1) // GM_TM + N_EXPERTS
    tix = jnp.arange(n_tiles, dtype=i32)
    total_tiles = tile_end[-1]
    tile_valid = (tix < total_tiles).astype(i32)
    expert_at = lambda tile: jnp.minimum((tile[:, None] >= tile_end[None, :]).astype(i32).sum(axis=1), N_EXPERTS - 1)
    last_expert = expert_at(jnp.maximum(total_tiles - 1, 0)[None])[0]
    tile_expert = jnp.where(tile_valid > 0, expert_at(tix), last_expert)
    hot_e = tile_expert[:, None] == jnp.arange(N_EXPERTS, dtype=i32)[None, :]
    per_tile = lambda v: jnp.sum(jnp.where(hot_e, v[None, :], 0), axis=1)
    per_tile_rows = lambda m: jnp.sum(jnp.where(hot_e[:, :, None], jnp.transpose(m)[None], 0), axis=1)
    start_t = per_tile(tile_start)
    tile_first = ((tix == start_t) & (tile_valid > 0)).astype(i32)
    following = per_tile(tile_end)
    tile_next = jnp.where(following < total_tiles, expert_at(following), -1)
    j = (tix - start_t)[:, None] * NPIECE + jnp.arange(NPIECE, dtype=i32)[None, :]
    cnt_t = per_tile(cnt_e)
    ok = (tile_valid[:, None] > 0) & (j < cnt_t[:, None])
    tile_half = ((tile_valid > 0) & (cnt_t - (tix - start_t) * NPIECE <= NPIECE // 2)).astype(i32)
    ends_t, cp_t, loff_t = per_tile_rows(cp_end), per_tile_rows(cp), per_tile_rows(loff)
    t_q = jnp.minimum((ends_t[:, None, :] <= j[:, :, None]).astype(i32).sum(axis=2), nt - 1)
    hot_t = t_q[:, :, None] == jnp.arange(nt, dtype=i32)[None, None, :]
    at_t = lambda m: jnp.sum(jnp.where(hot_t, m[:, None, :], 0), axis=2)
    piece = t_q * SEG_PIECES + at_t(loff_t) // PIECE + j - at_t(cp_t)
    src = jnp.where(ok, piece, 0)
    dump = nt * SEG_PIECES + (tix % 2)[:, None] * NPIECE + jnp.arange(NPIECE, dtype=i32)[None, :]
    dst = jnp.where(ok, piece, dump)
    plan = dict(tile_expert=tile_expert.astype(i32), tile_valid=tile_valid, tile_first=tile_first,
                tile_next=tile_next.astype(i32), tile_half=tile_half,
                src=src.reshape(-1).astype(i32), dst=dst.reshape(-1).astype(i32))

    ys = _experts(plan, xs, w, nt * SEG_ROWS + 2 * GM_TM)
    outs, r0 = [], 0
    for x1, ld, tg in groups:
        outs.append(_combine(used, ld, tg, x1, ys, r0 // MOE_TM))
        r0 += x1.shape[0]
    return outs


def _pair_major_to_rows(a3):
    return jnp.transpose(a3, (1, 0, 2)).reshape(a3.shape[1], 3 * LANE)


def kernel(x_prompt, x_sample, mem_prompt, cache_win_k, cache_win_v, cache_mem_k, cache_mem_v, norm1_g, w_in, gv_a, w_s, b_s, gq_b, gk_b, gq_m, gk_m, mem_norm_g, w_mem_kv, w_out, norm2_g, w_router, b_router, w_gate_up, b_gate_up, w_down, b_down):
    batch, seq, _ = x_prompt.shape
    bd, dec, _ = x_sample.shape
    depth = norm1_g.shape[0]
    assert depth == 1 and seq % SPAN == 0 and (bd * dec) % TM == 0 and PAST_LEN % CHUNK == 0
    w_buf = cache_win_k.shape[2]
    assert w_buf == MAX_WINDOW and dec <= 8
    l = 0
    two = lambda g: jnp.concatenate([g, g])[None, :]
    head = np.arange(LANE) // HEAD_DIM
    head2 = np.arange(2 * LANE) // HEAD_DIM
    wr = jnp.pad(w_router[l], ((0, 0), (0, LANE - N_EXPERTS)))
    wr_hi = wr.astype(BF16)
    wr_lo = (wr - wr_hi.astype(F32)).astype(BF16)
    w = dict(
        g1=norm1_g[l][None], w_in=w_in[l].astype(BF16), gva=gv_a[l][None],
        gq=two(gq_b[l]), gk=two(gk_b[l]), gqm=two(gq_m[l]), gkm=two(gk_m[l]),
        bd=jnp.asarray(head[:, None] == head[None, :], BF16),
        bd2=jnp.asarray(head2[:, None] == head2[None, :], BF16),
        gmem=mem_norm_g[l][None], w_mem_kv=w_mem_kv[l].astype(BF16),
        w_out=w_out[l].astype(BF16), g2=norm2_g[l][None],
        w_router=jnp.concatenate([wr_hi, wr_lo], axis=1),
        b_router=jnp.pad(b_router[l], (0, LANE - N_EXPERTS), constant_values=-jnp.inf)[None],
        w_gate_up=w_gate_up[l], b_gate_up=b_gate_up[l][:, None, :], w_down=w_down[l], b_down=b_down[l][:, None, :],
    )
    ngrp = WIDTH_A // HEAD_DIM
    wtri = jnp.where(jnp.tril(jnp.ones((CHUNK, CHUNK), bool)), w_s[l], 0).astype(BF16)
    wg_p = wtri.reshape(ngrp // 2, 2 * CHUNK, CHUNK)
    bg_p = jnp.repeat(jnp.transpose(b_s[l]), HEAD_DIM, axis=1)
    zero = jnp.zeros((ngrp,), F32)
    lanes = lambda tg_: jnp.tile(jnp.repeat(tg_, HEAD_DIM, axis=1), (bd, 1))
    wg_s = jnp.stack([lanes(jnp.stack([w_s[l][:, t, t - s] if t >= s else zero for t in range(dec)]))
                      for s in range(dec)])
    bg_s = lanes(jnp.transpose(b_s[l][:, :dec]))

    xs = x_sample.reshape(bd * dec, D_MODEL)
    tabs_s = _rope_tables(np.tile(PAST_LEN + np.arange(dec), bd))
    u_s, va_s, q3_s, k3_s, v3_s, qm_s = _premix(xs, tabs_s, bd * dec, w)
    q_s, k_s, v_s = (_pair_major_to_rows(a) for a in (q3_s, k3_s, v3_s))
    nb = WIDTH_B // HEAD_DIM
    nm = WIDTH_M // HEAD_DIM
    kt = jnp.transpose(cache_win_k[l], (0, 2, 3, 1)).reshape(bd, WIDTH_B, w_buf)
    vt = jnp.transpose(cache_win_v[l], (0, 2, 3, 1)).reshape(bd, WIDTH_B, w_buf)
    kmt = jnp.transpose(cache_mem_k[l], (0, 2, 3, 1)).reshape(bd, WIDTH_M, N_MEM)
    vmt = jnp.transpose(cache_mem_v[l], (0, 2, 3, 1)).reshape(bd, WIDTH_M, N_MEM)
    sample_parts = _sample_attn_parts(q_s, k_s, v_s, qm_s, kt, vt, kmt, vmt, dec)

    xp = x_prompt.reshape(batch * seq, D_MODEL)
    tabs_p = _rope_tables(np.arange(seq))
    premix_tm = batch * seq // sample_parts["steps"]
    assert seq % premix_tm == 0 and premix_tm % 8 == 0
    (u_p, va_p, q3_p, k3_p, v3_p, qm_p), (ob_s8, om_s8) = _premix_with_sample_attn(
        _premix_parts(xp, tabs_p, seq, w, premix_tm, gate_dtype=BF16), sample_parts)
    ob_s, om_s = _sample_attn_rows(ob_s8, om_s8, dec)
    ob_p = _attn_prompt(q3_p, k3_p, v3_p, batch, seq)
    km, vm = _memkv(mem_prompt.reshape(batch * N_MEM, D_MODEL), w)
    om_p = _memattn_prompt(qm_p, km.reshape(batch, N_MEM, WIDTH_M), vm.reshape(batch, N_MEM, WIDTH_M), batch, seq)
    tiles_p = batch * seq // MOE_TM
    tiles_all = tiles_p + bd * dec // MOE_TM
    x1_p, xs_all, ld_p, tg_p, seg_p = _post(xp, u_p, va_p, ob_p, om_p, wg_p, bg_p, w, CHUNK, tiles_all)

    x1_s, xs_all, ld_s, tg_s, seg_s = _post(xs, u_s, va_s, ob_s, om_s, wg_s, bg_s, w, None, tiles_all,
                                             tile0=tiles_p, xs_all=xs_all)

    seglen = jnp.concatenate([seg_p, seg_s])[::8, :N_EXPERTS]
    y_p, y_s = _moe([(x1_p, ld_p, tg_p), (x1_s, ld_s, tg_s)], xs_all, seglen, w)
    y_prompt = y_p.reshape(batch, seq, D_MODEL)
    y_sample = y_s.reshape(bd, dec, D_MODEL)

    n_keep = min(MAX_WINDOW, seq)

    def window_rows(a3):
        a = a3.reshape(3, batch, seq, LANE)[:, :, seq - n_keep:]
        return jnp.transpose(a, (1, 2, 0, 3)).reshape(1, batch, n_keep, nb, HEAD_DIM)

    return (y_prompt, y_sample,
            window_rows(k3_p), window_rows(v3_p),
            km.reshape(1, batch, N_MEM, nm, HEAD_DIM), vm.reshape(1, batch, N_MEM, nm, HEAD_DIM),
            k_s.reshape(1, bd, dec, nb, HEAD_DIM), v_s.reshape(1, bd, dec, nb, HEAD_DIM),
            va_s.reshape(1, bd, dec, WIDTH_A))
```

```python
import functools

import numpy as np
import jax
import jax.numpy as jnp
from jax import lax
from jax.experimental import pallas as pl
from jax.experimental.pallas import tpu as pltpu

F32 = jnp.float32
BF16 = jnp.bfloat16

D_MODEL = 1024
HEAD_DIM = 64
WIDTH_A = 384
WIDTH_B = 384
WIDTH_M = 256
IN_WIDTH = 2 * WIDTH_A + 3 * WIDTH_B + WIDTH_M
CHUNK = 128
DILATIONS = ((128, 1), (512, 4), (2048, 16))
N_SUB = 128
MAX_WINDOW = 2048
N_MEM = 256
ROPE_THETA = 500000.0
ROT_HALF = 8
SCALE = HEAD_DIM ** -0.5
N_EXPERTS = 32
TOP_K = 4
SWIGLU_LIMIT = 7.0
SWIGLU_ALPHA = 1.702
EPS = 1e-6
PAST_LEN = 8192

LANE = 128
NEG = -1e30
TM = 512
SPAN = 2048
QB = 128
SAMPLE_SEQS = 2
MOE_TM = 256
GM_TM = 512
PIECE = 8
NPIECE = GM_TM // PIECE
SEG_ROWS = TOP_K * MOE_TM + N_EXPERTS * PIECE
SEG_PIECES = SEG_ROWS // PIECE
VMEM_LIMIT = 52 * 1024 * 1024


def _cparams(sem):
    return pltpu.CompilerParams(dimension_semantics=sem, vmem_limit_bytes=VMEM_LIMIT)


def _premix_body(x_ref, g1_ref, win_ref, gva_ref, gq_ref, gk_ref, gqm_ref, bd_ref,
                 rc_ref, rs1_ref, rs2_ref,
                 u_ref, va_ref, q3_ref, k3_ref, v3_ref, qm_ref):
    x = x_ref[...]
    tm = x.shape[0]
    ms = jnp.mean(x * x, axis=-1, keepdims=True)
    h = (x * lax.rsqrt(ms + EPS) * g1_ref[...]).astype(BF16)
    z = jnp.dot(h, win_ref[...], preferred_element_type=F32)
    u_ref[...] = z[:, :WIDTH_A].astype(u_ref.dtype)
    va = z[:, WIDTH_A:2 * WIDTH_A]
    va_ms = jnp.mean(va * va, axis=-1, keepdims=True)
    va_ref[...] = (va * lax.rsqrt(va_ms + EPS) * gva_ref[...]).astype(va_ref.dtype)
    q0, k0, v0, m0 = 2 * WIDTH_A, 2 * WIDTH_A + WIDTH_B, 2 * WIDTH_A + 2 * WIDTH_B, 2 * WIDTH_A + 3 * WIDTH_B
    tiles = ([z[:, q0 + LANE * j:q0 + LANE * (j + 1)] for j in range(3)]
             + [z[:, k0 + LANE * j:k0 + LANE * (j + 1)] for j in range(3)]
             + [z[:, m0 + LANE * j:m0 + LANE * (j + 1)] for j in range(2)])
    sq = [(t * t).astype(BF16) for t in tiles]
    sq = jnp.concatenate([jnp.concatenate(sq[i:i + 2], axis=1) for i in range(0, 8, 2)], axis=0)
    ssum = jnp.dot(sq, bd_ref[...], preferred_element_type=F32)
    inv = [lax.rsqrt(ssum[(i // 2) * tm:(i // 2 + 1) * tm, (i % 2) * LANE:(i % 2 + 1) * LANE] * (1.0 / HEAD_DIM) + EPS)
           for i in range(8)]
    rc, rs1, rs2 = rc_ref[...], rs1_ref[...], rs2_ref[...]

    def rope(t):
        return t * rc + pltpu.roll(t, LANE - ROT_HALF, 1) * rs1 + pltpu.roll(t, ROT_HALF, 1) * rs2

    for j in range(3):
        q3_ref[j] = rope(tiles[j] * inv[j] * gq_ref[...]) * SCALE
        k3_ref[j] = rope(tiles[3 + j] * inv[3 + j] * gk_ref[...])
        v3_ref[j] = z[:, v0 + LANE * j:v0 + LANE * (j + 1)]
    for j in range(2):
        qm_ref[:, LANE * j:LANE * (j + 1)] = (tiles[6 + j] * inv[6 + j] * gqm_ref[...] * SCALE).astype(BF16)


def _premix_parts(x, tabs, seq, w, tm, gate_dtype=F32):
    t = x.shape[0]
    n_tab_tiles = seq // tm
    full = lambda shape: pl.BlockSpec(shape, lambda i: (0,) * len(shape))
    tab = pl.BlockSpec((tm, LANE), lambda i: (i % n_tab_tiles, 0))
    row = lambda width: pl.BlockSpec((tm, width), lambda i: (i, 0))
    pair = pl.BlockSpec((3, tm, LANE), lambda i: (0, i, 0))
    return dict(
        steps=t // tm,
        in_specs=[row(D_MODEL), full((1, D_MODEL)), full((D_MODEL, IN_WIDTH)), full((1, WIDTH_A)),
                  full((1, LANE)), full((1, LANE)), full((1, LANE)), full((2 * LANE, 2 * LANE)), tab, tab, tab],
        args=[x, w["g1"], w["w_in"], w["gva"], w["gq"], w["gk"], w["gqm"], w["bd2"], *tabs],
        out_specs=[row(WIDTH_A), row(WIDTH_A), pair, pair, pair, row(WIDTH_M)],
        out_shape=[jax.ShapeDtypeStruct((t, WIDTH_A), gate_dtype), jax.ShapeDtypeStruct((t, WIDTH_A), gate_dtype),
                   jax.ShapeDtypeStruct((3, t, LANE), F32), jax.ShapeDtypeStruct((3, t, LANE), F32),
                   jax.ShapeDtypeStruct((3, t, LANE), F32), jax.ShapeDtypeStruct((t, WIDTH_M), BF16)])


def _premix(x, tabs, seq, w):
    p = _premix_parts(x, tabs, seq, w, TM)
    return pl.pallas_call(
        _premix_body, grid=(p["steps"],), in_specs=p["in_specs"], out_specs=p["out_specs"],
        out_shape=p["out_shape"], compiler_params=_cparams(("arbitrary",)), name="premix",
    )(*p["args"])


def _premix_sample_body(*refs, n_in, n_out):
    ins_a, ins_b = refs[:n_in[0]], refs[n_in[0]:n_in[0] + n_in[1]]
    km_ref, vm_ref = refs[n_in[0] + n_in[1]:n_in[0] + n_in[1] + 2]
    outs = refs[n_in[0] + n_in[1] + 2:]
    _premix_body(*ins_a, *outs[:n_out])
    _sample_attn_body(*ins_b, *outs[n_out:n_out + 2])
    _memattn_body(outs[n_out - 1], km_ref, vm_ref, outs[n_out + 2])


def _premix_with_sample_attn(a, b, km, vm):
    assert a["steps"] == b["steps"]
    tokens = a["out_shape"][0].shape[0]
    tm = tokens // a["steps"]
    steps_per_b = a["steps"] // km.shape[0]
    kv = pl.BlockSpec((1, N_MEM, WIDTH_M), lambda i: (i // steps_per_b, 0, 0))
    outs = pl.pallas_call(
        functools.partial(_premix_sample_body, n_in=(len(a["args"]), len(b["args"])), n_out=len(a["out_shape"])),
        grid=(a["steps"],), in_specs=a["in_specs"] + b["in_specs"] + [kv, kv],
        out_specs=a["out_specs"] + b["out_specs"] + [pl.BlockSpec((tm, WIDTH_M), lambda i: (i, 0))],
        out_shape=a["out_shape"] + b["out_shape"] + [jax.ShapeDtypeStruct((tokens, WIDTH_M), BF16)],
        compiler_params=_cparams(("arbitrary",)),
        name="premix_sample_attn",
    )(*a["args"], *b["args"], km, vm)
    na = len(a["out_shape"])
    return outs[:na], outs[na:na + 2], outs[na + 2]


def _rope_tables(pos):
    pos = np.asarray(pos, np.float64)
    inv_freq = np.power(ROPE_THETA, -np.arange(ROT_HALF, dtype=np.float64) / ROT_HALF)
    ang = pos[:, None] * inv_freq[None, :]
    cos, sin = np.cos(ang), np.sin(ang)
    t = pos.shape[0]
    rest = HEAD_DIM - 2 * ROT_HALF
    c = np.concatenate([cos, cos, np.ones((t, rest))], axis=1)
    s1 = np.concatenate([-sin, np.zeros((t, HEAD_DIM - ROT_HALF))], axis=1)
    s2 = np.concatenate([np.zeros((t, ROT_HALF)), sin, np.zeros((t, rest))], axis=1)
    two = lambda a: jnp.asarray(np.concatenate([a, a], axis=1), F32)
    return two(c), two(s1), two(s2)


def _attn_body(q_ref, kc_ref, kp_ref, vc_ref, vp_ref, o_ref, m_s, l_s, a_s):
    span_idx = pl.program_id(1)
    p0 = span_idx * SPAN
    lane = lax.broadcasted_iota(jnp.int32, (QB, LANE), 1)
    low = lane < HEAD_DIM
    qi = lax.broadcasted_iota(jnp.int32, (QB, 2 * QB), 0)
    kj = lax.broadcasted_iota(jnp.int32, (QB, 2 * QB), 1)
    band = (kj >= qi) & (kj <= qi + N_SUB)

    def rows(ref, start, d):
        if d == 1:
            return ref[0, pl.ds(start, QB), :]
        return ref[0, pl.ds(start, QB, stride=d), :]

    def unit(d, res, blk, first):
        qstart = res + d * QB * blk
        if blk == 0:
            older = (kp_ref, vp_ref, SPAN - QB * d + res)
            first_key = jnp.maximum((QB * d - p0 - res + d - 1) // d, 0)
            mask = band & (kj >= first_key)
        else:
            older = (kc_ref, vc_ref, qstart - QB * d)
            mask = band
        qb = rows(q_ref, qstart, d)
        kb = jnp.concatenate([rows(older[0], older[2], d), rows(kc_ref, qstart, d)], axis=0).astype(BF16)
        vb = jnp.concatenate([rows(older[1], older[2], d), rows(vc_ref, qstart, d)], axis=0).astype(BF16)
        stats = []
        for hm in (low, jnp.logical_not(low)):
            qh = jnp.where(hm, qb, 0.0).astype(BF16)
            s = lax.dot_general(qh, kb, (((1,), (1,)), ((), ())), preferred_element_type=F32)
            s = jnp.where(mask, s, NEG)
            m = jnp.max(s, axis=-1, keepdims=True)
            e = jnp.exp(s - m)
            l = jnp.sum(e, axis=-1, keepdims=True)
            acc = jnp.dot(e.astype(BF16), vb, preferred_element_type=F32)
            stats.append((m, l, acc))
        m_new = jnp.where(low, stats[0][0], stats[1][0])
        l_new = jnp.where(low, stats[0][1], stats[1][1])
        a_new = jnp.where(low, stats[0][2], stats[1][2])
        sl = (pl.ds(qstart, QB) if d == 1 else pl.ds(qstart, QB, stride=d), slice(None))
        if first:
            m_s[sl] = m_new
            l_s[sl] = l_new
            a_s[sl] = a_new
        else:
            m_old, l_old, a_old = m_s[sl], l_s[sl], a_s[sl]
            m_t = jnp.maximum(m_old, m_new)
            wa = jnp.exp(m_old - m_t)
            wb = jnp.exp(m_new - m_t)
            m_s[sl] = m_t
            l_s[sl] = wa * l_old + wb * l_new
            a_s[sl] = wa * a_old + wb * a_new

    nblk = SPAN // QB
    first = True
    for _, d in DILATIONS:
        per_res = nblk // d
        for res in range(d):
            for blk in range(per_res):
                unit(d, res, blk, first)
        first = False
    o_ref[...] = (a_s[...] / l_s[...]).astype(o_ref.dtype)


def _attn_prompt(q3, k3, v3, batch, seq):
    nspan = seq // SPAN
    cur = pl.BlockSpec((1, SPAN, LANE), lambda b, s, p: (p, b * nspan + s, 0))
    prv = pl.BlockSpec((1, SPAN, LANE), lambda b, s, p: (p, b * nspan + jnp.maximum(s - 1, 0), 0))
    return pl.pallas_call(
        _attn_body,
        grid=(batch, nspan, 3),
        in_specs=[cur, cur, prv, cur, prv],
        out_specs=pl.BlockSpec((SPAN, LANE), lambda b, s, p: (b * nspan + s, p)),
        out_shape=jax.ShapeDtypeStruct((batch * seq, WIDTH_B), BF16),
        scratch_shapes=[pltpu.VMEM((SPAN, LANE), F32)] * 3,
        compiler_params=_cparams(("arbitrary", "arbitrary", "arbitrary")),
        name="attn_prompt",
    )(q3, k3, k3, v3, v3)


def _memkv_body(mem_ref, g_ref, w_ref, gk_ref, bd_ref, k_ref, v_ref):
    x = mem_ref[...]
    ms = jnp.mean(x * x, axis=-1, keepdims=True)
    h = (x * lax.rsqrt(ms + EPS) * g_ref[...]).astype(BF16)
    kv = jnp.dot(h, w_ref[...], preferred_element_type=F32)
    n = x.shape[0]
    kt = [kv[:, LANE * j:LANE * (j + 1)] for j in range(2)]
    sq = jnp.concatenate([(t * t).astype(BF16) for t in kt], axis=0)
    ssum = jnp.dot(sq, bd_ref[...], preferred_element_type=F32)
    for j in range(2):
        inv = lax.rsqrt(ssum[j * n:(j + 1) * n] * (1.0 / HEAD_DIM) + EPS)
        k_ref[:, LANE * j:LANE * (j + 1)] = kt[j] * inv * gk_ref[...]
    v_ref[...] = kv[:, WIDTH_M:]


def _memkv(mem, w):
    n = mem.shape[0]
    return pl.pallas_call(
        _memkv_body,
        out_shape=[jax.ShapeDtypeStruct((n, WIDTH_M), F32)] * 2,
        compiler_params=pltpu.CompilerParams(vmem_limit_bytes=VMEM_LIMIT),
        name="memkv",
    )(mem, w["gmem"], w["w_mem_kv"], w["gkm"], w["bd"])


def _memattn_body(q_ref, k_ref, v_ref, o_ref):
    lane = lax.broadcasted_iota(jnp.int32, (q_ref.shape[0], LANE), 1)
    low = lane < HEAD_DIM
    for j in range(2):
        qp = q_ref[:, LANE * j:LANE * (j + 1)].astype(F32)
        kp = k_ref[0, :, LANE * j:LANE * (j + 1)].astype(BF16)
        vp = v_ref[0, :, LANE * j:LANE * (j + 1)].astype(BF16)
        outs = []
        for hm in (low, jnp.logical_not(low)):
            qh = jnp.where(hm, qp, 0.0).astype(BF16)
            s = lax.dot_general(qh, kp, (((1,), (1,)), ((), ())), preferred_element_type=F32)
            m = jnp.max(s, axis=-1, keepdims=True)
            e = jnp.exp(s - m)
            l = jnp.sum(e, axis=-1, keepdims=True)
            outs.append(jnp.dot(e.astype(BF16), vp, preferred_element_type=F32) / l)
        o_ref[:, LANE * j:LANE * (j + 1)] = jnp.where(low, outs[0], outs[1]).astype(o_ref.dtype)


def _sample_attn_body(*refs):
    for i in range(refs[0].shape[0]):
        _sample_attn_one(i, *refs)


def _sample_attn_one(i, qbd_ref, kt_ref, vt_ref, kn_ref, vn_ref, cnt_ref, cntn_ref, hmask_ref,
                     qmbd_ref, kmt_ref, vmt_ref, hmaskm_ref, ob_ref, om_ref):
    dec = kn_ref.shape[1]
    qbd = qbd_ref[i]
    kt = kt_ref[i].astype(BF16)
    vt = vt_ref[i].astype(BF16)
    s = jnp.dot(qbd, kt, preferred_element_type=F32)
    qf = qbd.astype(F32)
    kn = kn_ref[i]
    vn = vn_ref[i]
    cnt = cnt_ref[...]
    cntn = cntn_ref[...]
    s_new = [jnp.sum(qf * kn[j:j + 1, :], axis=-1, keepdims=True) for j in range(dec)]
    m = jnp.max(jnp.where(cnt > 0, s, NEG), axis=-1, keepdims=True)
    for j in range(dec):
        m = jnp.maximum(m, jnp.where(cntn[:, j:j + 1] > 0, s_new[j], NEG))
    e = cnt * jnp.exp(jnp.where(cnt > 0, s - m, 0.0))
    l = jnp.sum(e, axis=-1, keepdims=True)
    acc = lax.dot_general(e.astype(BF16), vt, (((1,), (1,)), ((), ())), preferred_element_type=F32)
    for j in range(dec):
        w = cntn[:, j:j + 1]
        ej = w * jnp.exp(jnp.where(w > 0, s_new[j] - m, 0.0))
        l = l + ej
        acc = acc + ej * vn[j:j + 1, :]
    r = acc / l * hmask_ref[...]
    out = r[0:8]
    for h in range(1, WIDTH_B // HEAD_DIM):
        out = out + r[8 * h:8 * h + 8]
    ob_ref[i] = out
    qm = qmbd_ref[i]
    sm = jnp.dot(qm, kmt_ref[i].astype(BF16), preferred_element_type=F32)
    mm = jnp.max(sm, axis=-1, keepdims=True)
    em = jnp.exp(sm - mm)
    lm = jnp.sum(em, axis=-1, keepdims=True)
    am = lax.dot_general(em.astype(BF16), vmt_ref[i].astype(BF16), (((1,), (1,)), ((), ())),
                         preferred_element_type=F32)
    rm = am / lm * hmaskm_ref[...]
    outm = rm[0:8]
    for h in range(1, WIDTH_M // HEAD_DIM):
        outm = outm + rm[8 * h:8 * h + 8]
    om_ref[i] = outm


def _sample_counts(dec, w_buf):
    t = np.arange(8)[:, None]
    t = np.where(t < dec, t, 0)
    def mult(dist):
        c = np.zeros(dist.shape, np.float32)
        for window, dil in DILATIONS:
            c += ((dist >= 0) & (dist % dil == 0) & (dist <= window)).astype(np.float32)
        return c
    cache = mult(w_buf + t - np.arange(w_buf)[None, :])
    new = mult(t - np.arange(dec)[None, :])
    nb, nm = WIDTH_B // HEAD_DIM, WIDTH_M // HEAD_DIM
    hmask = (np.arange(8 * nb)[:, None] // 8 == np.arange(WIDTH_B)[None, :] // HEAD_DIM).astype(np.float32)
    hmaskm = (np.arange(8 * nm)[:, None] // 8 == np.arange(WIDTH_M)[None, :] // HEAD_DIM).astype(np.float32)
    return np.tile(cache, (nb, 1)), np.tile(new, (nb, 1)), hmask, hmaskm


def _block_diag_queries(q, dec, hmask):
    width = q.shape[-1]
    nh = width // HEAD_DIM
    qb = q.reshape(-1, 1, dec, width)
    qb = jnp.pad(qb, ((0, 0), (0, 0), (0, 8 - dec), (0, 0)))
    qb = jnp.broadcast_to(qb, (qb.shape[0], nh, 8, width)).reshape(-1, 8 * nh, width)
    return (qb * hmask[None]).astype(BF16)


def _sample_attn_parts(q, kn, vn, qm, kt, vt, kmt, vmt, dec):
    bd = kt.shape[0]
    w_buf = kt.shape[-1]
    cnt, cntn, hmask, hmaskm = _sample_counts(dec, w_buf)
    qbd = _block_diag_queries(q, dec, hmask)
    qmbd = _block_diag_queries(qm.astype(F32), dec, hmaskm)
    nb8, nm8 = qbd.shape[1], qmbd.shape[1]
    per_b = lambda shape: pl.BlockSpec((SAMPLE_SEQS,) + shape, lambda b: (b,) + (0,) * len(shape))
    full = lambda shape: pl.BlockSpec(shape, lambda b: (0,) * len(shape))
    return dict(
        steps=bd // SAMPLE_SEQS,
        in_specs=[per_b((nb8, WIDTH_B)), per_b((WIDTH_B, w_buf)), per_b((WIDTH_B, w_buf)),
                  per_b((dec, WIDTH_B)), per_b((dec, WIDTH_B)),
                  full((nb8, w_buf)), full((nb8, dec)), full((nb8, WIDTH_B)),
                  per_b((nm8, WIDTH_M)), per_b((WIDTH_M, N_MEM)), per_b((WIDTH_M, N_MEM)), full((nm8, WIDTH_M))],
        args=[qbd, kt, vt, kn.reshape(bd, dec, WIDTH_B), vn.reshape(bd, dec, WIDTH_B),
              jnp.asarray(cnt), jnp.asarray(cntn), jnp.asarray(hmask), qmbd, kmt, vmt, jnp.asarray(hmaskm)],
        out_specs=[per_b((8, WIDTH_B)), per_b((8, WIDTH_M))],
        out_shape=[jax.ShapeDtypeStruct((bd, 8, WIDTH_B), F32), jax.ShapeDtypeStruct((bd, 8, WIDTH_M), F32)])


def _sample_attn_rows(ob, om, dec):
    return (ob[:, :dec].reshape(-1, WIDTH_B).astype(BF16), om[:, :dec].reshape(-1, WIDTH_M).astype(BF16))


def _post_body(*refs, tc, aliased):
    (x_ref, u_ref, va_ref, ob_ref, om_ref, wg_ref, bg_ref, wout_ref, g2_ref, wr_ref, br_ref,
     tri_ref, upper_ref) = refs[:13]
    x1_ref, xs_ref, ld_ref, tg_ref, seg_ref = refs[13 + aliased:]
    tm = x_ref.shape[0]
    u = u_ref[...].astype(F32)
    if tc is None:
        vaf = va_ref[...]
        mixed = wg_ref[0] * vaf + bg_ref[...]
        for s in range(1, wg_ref.shape[0]):
            mixed = mixed + wg_ref[s] * pltpu.roll(vaf, s, 0)
        oa = u * mixed
    else:
        lane = lax.broadcasted_iota(jnp.int32, (tc, LANE), 1)
        low = lane < HEAD_DIM
        va = va_ref[...].astype(BF16)
        oa_rows = []
        for c in range(tm // tc):
            r0 = c * tc
            tiles = []
            for p in range(3):
                vp = va[r0:r0 + tc, LANE * p:LANE * (p + 1)]
                r = jnp.dot(wg_ref[p], vp, preferred_element_type=F32)
                tiles.append(jnp.where(low, r[:tc], r[tc:]))
            mixed = jnp.concatenate(tiles, axis=1) + bg_ref[...]
            oa_rows.append(u[r0:r0 + tc] * mixed)
        oa = jnp.concatenate(oa_rows, axis=0)
    mixed_all = jnp.concatenate([oa.astype(BF16), ob_ref[...], om_ref[...]], axis=1)
    x1 = x_ref[...] + jnp.dot(mixed_all, wout_ref[...], preferred_element_type=F32)
    x1_ref[...] = x1
    ms = jnp.mean(x1 * x1, axis=-1, keepdims=True)
    h2 = x1 * lax.rsqrt(ms + EPS) * g2_ref[...]
    h_hi = h2.astype(BF16)
    h_lo = (h2 - h_hi.astype(F32)).astype(BF16)
    hw = jnp.dot(h_hi, wr_ref[...], preferred_element_type=F32)
    logits = (hw[:, :LANE] + hw[:, LANE:]
              + jnp.dot(h_lo, wr_ref[:, :LANE], preferred_element_type=F32)) + br_ref[...]
    lane_i = lax.broadcasted_iota(jnp.int32, (tm, LANE), 1)
    lane_r = lane_i.astype(F32)
    vals = logits
    tops, idxs = [], []
    for _ in range(TOP_K):
        mk = jnp.max(vals, axis=-1, keepdims=True)
        ik = jnp.min(jnp.where(vals == mk, lane_r, float(LANE)), axis=-1, keepdims=True)
        vals = jnp.where(lane_r == ik, -jnp.inf, vals)
        tops.append(mk)
        idxs.append(ik)
    es = [jnp.exp(t - tops[0]) for t in tops]
    den = es[0] + es[1] + es[2] + es[3]
    tg = jnp.zeros((tm, LANE), F32)
    for k in range(TOP_K):
        tg = jnp.where(lane_i == k, es[k] / den, tg)
    tg_ref[...] = tg
    col = lax.broadcasted_iota(jnp.int32, (MOE_TM, SEG_ROWS), 1).astype(F32)
    lane_t = lax.broadcasted_iota(jnp.int32, (MOE_TM, LANE), 1)
    hot_all = [lane_r == idxs[k] for k in range(TOP_K)]
    for hf in range(tm // MOE_TM):
        r0 = hf * MOE_TM
        hot = [h[r0:r0 + MOE_TM] for h in hot_all]
        sel = jnp.zeros((MOE_TM, LANE), F32)
        for k in range(TOP_K):
            sel = sel + jnp.where(hot[k], 1.0, 0.0)
        rank = jnp.dot(tri_ref[...], sel.astype(BF16), preferred_element_type=F32)
        length = jnp.sum(sel, axis=0, keepdims=True)
        plen = jnp.floor((length + (PIECE - 1)) * (1.0 / PIECE)) * PIECE
        loff = jnp.dot(jnp.broadcast_to(plen, (8, LANE)).astype(BF16), upper_ref[...],
                       preferred_element_type=F32)[0:1]
        base = loff + rank
        hit = None
        ld = jnp.zeros((MOE_TM, LANE), F32)
        for k in range(TOP_K):
            ld_k = jnp.sum(jnp.where(hot[k], base, 0.0), axis=-1, keepdims=True)
            hit = (col == ld_k) if hit is None else hit | (col == ld_k)
            ld = jnp.where(lane_t == k, ld_k, ld)
        q = jnp.where(hit, 1.0, 0.0)
        xs_ref[hf * SEG_ROWS:(hf + 1) * SEG_ROWS, :] = lax.dot_general(
            q.astype(BF16), h_hi[r0:r0 + MOE_TM], (((0,), (0,)), ((), ())),
            preferred_element_type=F32).astype(xs_ref.dtype)
        ld_ref[r0:r0 + MOE_TM, :] = ld.astype(jnp.int32)
        seg_ref[hf * 8:(hf + 1) * 8, :] = jnp.broadcast_to(length, (8, LANE)).astype(jnp.int32)


def _post(x, u, va, ob, om, wg, bg, w, tc, total_tiles, tile0=0, xs_all=None):
    t = x.shape[0]
    nt = t // TM
    full = lambda shape: pl.BlockSpec(shape, lambda i: (0,) * len(shape))
    row = lambda width: pl.BlockSpec((TM, width), lambda i: (i, 0))
    per = TM // MOE_TM
    ix = np.arange(MOE_TM)
    tri = jnp.asarray(ix[:, None] > ix[None, :], BF16)
    ex = np.arange(LANE)
    upper = jnp.asarray(ex[:, None] < ex[None, :], BF16)
    step0 = tile0 // per
    in_specs = [row(D_MODEL), row(WIDTH_A), row(WIDTH_A), row(WIDTH_B), row(WIDTH_M),
                full(wg.shape), full(bg.shape), full((D_MODEL, D_MODEL)), full((1, D_MODEL)),
                full((D_MODEL, 2 * LANE)), full((1, LANE)), full((MOE_TM, MOE_TM)), full((LANE, LANE))]
    args = [x, u, va, ob, om, wg, bg, w["w_out"], w["g2"], w["w_router"], w["b_router"], tri, upper]
    aliases = {}
    if xs_all is not None:
        in_specs.append(pl.BlockSpec(memory_space=pl.ANY))
        args.append(xs_all)
        aliases = {len(args) - 1: 1}
    return pl.pallas_call(
        functools.partial(_post_body, tc=tc, aliased=int(xs_all is not None)),
        grid=(nt,),
        in_specs=in_specs,
        out_specs=[row(D_MODEL), pl.BlockSpec((per * SEG_ROWS, D_MODEL), lambda i: (i + step0, 0)),
                   row(LANE), row(LANE), pl.BlockSpec((per * 8, LANE), lambda i: (i, 0))],
        out_shape=[jax.ShapeDtypeStruct((t, D_MODEL), F32),
                   jax.ShapeDtypeStruct((total_tiles * SEG_ROWS, D_MODEL), BF16),
                   jax.ShapeDtypeStruct((t, LANE), jnp.int32), jax.ShapeDtypeStruct((t, LANE), F32),
                   jax.ShapeDtypeStruct((t // MOE_TM * 8, LANE), jnp.int32)],
        input_output_aliases=aliases,
        compiler_params=_cparams(("arbitrary",)),
        name="post",
    )(*args)


def _experts_body(te_ref, valid_ref, first_ref, next_ref, half_ref, src_ref, dst_ref,
                  xs_hbm, wgu_hbm, bgu_ref, wd_hbm, bd_ref, ys_hbm,
                  xbuf, ybuf, wgu_f, wd_f, wgu_s, wd_s, gsem, ssem, wsem):
    t = pl.program_id(0)
    nt = pl.num_programs(0)
    slot = t % 2

    def gather(tile, sl):
        for i in range(NPIECE):
            s = src_ref[tile * NPIECE + i]
            pltpu.make_async_copy(xs_hbm.at[pl.ds(pl.multiple_of(s * PIECE, PIECE), PIECE), :],
                                  xbuf.at[sl, pl.ds(i * PIECE, PIECE), :], gsem.at[sl]).start()

    def scatter(tile, sl):
        for i in range(NPIECE):
            d = dst_ref[tile * NPIECE + i]
            pltpu.make_async_copy(ybuf.at[sl, pl.ds(i * PIECE, PIECE), :],
                                  ys_hbm.at[pl.ds(pl.multiple_of(d * PIECE, PIECE), PIECE), :], ssem.at[sl]).start()

    def wait_tile(hbm, buf, sem, sl):
        pltpu.make_async_copy(hbm.at[pl.ds(0, GM_TM), :], buf.at[sl], sem.at[sl]).wait()

    def weight_copies(e):
        return (pltpu.make_async_copy(wgu_hbm.at[e], wgu_f, wsem.at[0]),
                pltpu.make_async_copy(wd_hbm.at[e], wd_f, wsem.at[1]))

    @pl.when(t == 0)
    def _():
        for c in weight_copies(te_ref[0]):
            c.start()
        gather(0, 0)
        ybuf[...] = jnp.zeros(ybuf.shape, ybuf.dtype)

    @pl.when(valid_ref[t] > 0)
    def _():
        nxt = jnp.minimum(t + 1, nt - 1)
        has_next = jnp.logical_and(t + 1 < nt, valid_ref[nxt] > 0)

        @pl.when(first_ref[t] > 0)
        def _():
            for c in weight_copies(te_ref[t]):
                c.wait()
            wgu_s[...] = wgu_f[...].astype(BF16)
            wd_s[...] = wd_f[...].astype(BF16)

            @pl.when(next_ref[t] >= 0)
            def _():
                for c in weight_copies(next_ref[t]):
                    c.start()

        @pl.when(t >= 2)
        def _():
            wait_tile(ys_hbm, ybuf, ssem, slot)

        gather(jnp.where(has_next, t + 1, t), 1 - slot)
        wait_tile(xs_hbm, xbuf, gsem, slot)

        def ffn(rows):
            x = xbuf[slot, pl.ds(0, rows), :]
            gu = jnp.dot(x, wgu_s[...], preferred_element_type=F32) + bgu_ref[0]
            gate = jnp.minimum(gu[:, :D_MODEL], SWIGLU_LIMIT)
            up = jnp.clip(gu[:, D_MODEL:], -SWIGLU_LIMIT, SWIGLU_LIMIT)
            act = (up + 1.0) * (gate * (1.0 / (1.0 + jnp.exp(-SWIGLU_ALPHA * gate))))
            y = jnp.dot(act.astype(BF16), wd_s[...], preferred_element_type=F32) + bd_ref[0]
            ybuf[slot, pl.ds(0, rows), :] = y.astype(ybuf.dtype)

        @pl.when(half_ref[t] > 0)
        def _():
            ffn(GM_TM // 2)

        @pl.when(half_ref[t] == 0)
        def _():
            ffn(GM_TM)

        scatter(t, slot)

        @pl.when(jnp.logical_not(has_next))
        def _():
            wait_tile(xs_hbm, xbuf, gsem, 1 - slot)
            wait_tile(ys_hbm, ybuf, ssem, slot)

            @pl.when(t >= 1)
            def _():
                wait_tile(ys_hbm, ybuf, ssem, 1 - slot)


def _experts(plan, xs, w, ys_rows):
    n_tiles = plan["tile_expert"].shape[0]
    by_expert = lambda shape: pl.BlockSpec((1,) + shape, lambda t, te, *_: (te[t],) + (0,) * len(shape))
    hbm = pl.BlockSpec(memory_space=pl.ANY)
    return pl.pallas_call(
        _experts_body,
        grid_spec=pltpu.PrefetchScalarGridSpec(
            num_scalar_prefetch=7, grid=(n_tiles,),
            in_specs=[hbm, hbm, by_expert((1, 2 * D_MODEL)), hbm, by_expert((1, D_MODEL))],
            out_specs=hbm,
            scratch_shapes=[pltpu.VMEM((2, GM_TM, D_MODEL), BF16), pltpu.VMEM((2, GM_TM, D_MODEL), BF16),
                            pltpu.VMEM((D_MODEL, 2 * D_MODEL), F32), pltpu.VMEM((D_MODEL, D_MODEL), F32),
                            pltpu.VMEM((D_MODEL, 2 * D_MODEL), BF16), pltpu.VMEM((D_MODEL, D_MODEL), BF16),
                            pltpu.SemaphoreType.DMA((2,)), pltpu.SemaphoreType.DMA((2,)),
                            pltpu.SemaphoreType.DMA((2,))]),
        out_shape=jax.ShapeDtypeStruct((ys_rows, D_MODEL), BF16),
        compiler_params=_cparams(("arbitrary",)),
        name="moe_experts",
    )(plan["tile_expert"], plan["tile_valid"], plan["tile_first"], plan["tile_next"], plan["tile_half"],
      plan["src"], plan["dst"],
      xs, w["w_gate_up"], w["b_gate_up"], w["w_down"], w["b_down"])


def _combine_body(used_ref, ld_ref, g_ref, x1_ref, ys_ref, y_ref, *, tile0):
    t = pl.program_id(0)
    row = lax.broadcasted_iota(jnp.int32, (SEG_ROWS, 1), 0)
    ys = jnp.where(row < used_ref[tile0 + t], ys_ref[...].astype(F32), 0.0).astype(BF16)
    col = lax.broadcasted_iota(jnp.int32, (MOE_TM, SEG_ROWS), 1)
    ld = ld_ref[...]
    g = g_ref[...]
    p = jnp.zeros((MOE_TM, SEG_ROWS), F32)
    for k in range(TOP_K):
        p = jnp.where(col == ld[:, k:k + 1], g[:, k:k + 1], p)
    y_ref[...] = x1_ref[...] + jnp.dot(p.astype(BF16), ys, preferred_element_type=F32)


def _combine(used, ld, gates, x1, ys, tile0):
    n = x1.shape[0]
    nt = n // MOE_TM
    rows = lambda width: pl.BlockSpec((MOE_TM, width), lambda t, *_: (t, 0))
    return pl.pallas_call(
        functools.partial(_combine_body, tile0=tile0),
        grid_spec=pltpu.PrefetchScalarGridSpec(
            num_scalar_prefetch=1, grid=(nt,),
            in_specs=[rows(LANE), rows(LANE), rows(D_MODEL),
                      pl.BlockSpec((SEG_ROWS, D_MODEL), lambda t, *_: (t + tile0, 0))],
            out_specs=rows(D_MODEL)),
        out_shape=jax.ShapeDtypeStruct((n, D_MODEL), F32),
        compiler_params=_cparams(("arbitrary",)),
        name="moe_combine",
    )(used, ld, gates, x1, ys)


def _moe(groups, xs, seglen, w):
    i32 = jnp.int32
    nt = seglen.shape[0]
    plen = (seglen + PIECE - 1) // PIECE * PIECE
    loff = jnp.cumsum(plen, axis=1) - plen
    used = jnp.sum(plen, axis=1).astype(i32)
    pp = plen // PIECE
    cp_end = jnp.cumsum(pp, axis=0)
    cp = cp_end - pp
    cnt_e = cp_end[-1]
    tiles_e = (cnt_e + NPIECE - 1) // NPIECE
    tile_end = jnp.cumsum(tiles_e)
    tile_start = tile_end - tiles_e
    n_tiles = (nt * SEG_ROWS + GM_TM - 1) // GM_TM + N_EXPERTS
    tix = jnp.arange(n_tiles, dtype=i32)
    total_tiles = tile_end[-1]
    tile_valid = (tix < total_tiles).astype(i32)
    expert_at = lambda tile: jnp.minimum((tile[:, None] >= tile_end[None, :]).astype(i32).sum(axis=1), N_EXPERTS - 1)
    last_expert = expert_at(jnp.maximum(total_tiles - 1, 0)[None])[0]
    tile_expert = jnp.where(tile_valid > 0, expert_at(tix), last_expert)
    hot_e = tile_expert[:, None] == jnp.arange(N_EXPERTS, dtype=i32)[None, :]
    per_tile = lambda v: jnp.sum(jnp.where(hot_e, v[None, :], 0), axis=1)
    per_tile_rows = lambda m: jnp.sum(jnp.where(hot_e[:, :, None], jnp.transpose(m)[None], 0), axis=1)
    start_t = per_tile(tile_start)
    tile_first = ((tix == start_t) & (tile_valid > 0)).astype(i32)
    following = per_tile(tile_end)
    tile_next = jnp.where(following < total_tiles, expert_at(following), -1)
    j = (tix - start_t)[:, None] * NPIECE + jnp.arange(NPIECE, dtype=i32)[None, :]
    cnt_t = per_tile(cnt_e)
    ok = (tile_valid[:, None] > 0) & (j < cnt_t[:, None])
    tile_half = ((tile_valid > 0) & (cnt_t - (tix - start_t) * NPIECE <= NPIECE // 2)).astype(i32)
    ends_t, cp_t, loff_t = per_tile_rows(cp_end), per_tile_rows(cp), per_tile_rows(loff)
    t_q = jnp.minimum((ends_t[:, None, :] <= j[:, :, None]).astype(i32).sum(axis=2), nt - 1)
    hot_t = t_q[:, :, None] == jnp.arange(nt, dtype=i32)[None, None, :]
    at_t = lambda m: jnp.sum(jnp.where(hot_t, m[:, None, :], 0), axis=2)
    piece = t_q * SEG_PIECES + at_t(loff_t) // PIECE + j - at_t(cp_t)
    src = jnp.where(ok, piece, 0)
    dump = nt * SEG_PIECES + (tix % 2)[:, None] * NPIECE + jnp.arange(NPIECE, dtype=i32)[None, :]
    dst = jnp.where(ok, piece, dump)
    plan = dict(tile_expert=tile_expert.astype(i32), tile_valid=tile_valid, tile_first=tile_first,
                tile_next=tile_next.astype(i32), tile_half=tile_half,
                src=src.reshape(-1).astype(i32), dst=dst.reshape(-1).astype(i32))

    ys = _experts(plan, xs, w, nt * SEG_ROWS + 2 * GM_TM)
    outs, r0 = [], 0
    for x1, ld, tg in groups:
        outs.append(_combine(used, ld, tg, x1, ys, r0 // MOE_TM))
        r0 += x1.shape[0]
    return outs


def _pair_major_to_rows(a3):
    return jnp.transpose(a3, (1, 0, 2)).reshape(a3.shape[1], 3 * LANE)


def kernel(x_prompt, x_sample, mem_prompt, cache_win_k, cache_win_v, cache_mem_k, cache_mem_v, norm1_g, w_in, gv_a, w_s, b_s, gq_b, gk_b, gq_m, gk_m, mem_norm_g, w_mem_kv, w_out, norm2_g, w_router, b_router, w_gate_up, b_gate_up, w_down, b_down):
    batch, seq, _ = x_prompt.shape
    bd, dec, _ = x_sample.shape
    depth = norm1_g.shape[0]
    assert depth == 1 and seq % SPAN == 0 and (bd * dec) % TM == 0 and PAST_LEN % CHUNK == 0
    w_buf = cache_win_k.shape[2]
    assert w_buf == MAX_WINDOW and dec <= 8
    l = 0
    two = lambda g: jnp.concatenate([g, g])[None, :]
    head = np.arange(LANE) // HEAD_DIM
    head2 = np.arange(2 * LANE) // HEAD_DIM
    wr = jnp.pad(w_router[l], ((0, 0), (0, LANE - N_EXPERTS)))
    wr_hi = wr.astype(BF16)
    wr_lo = (wr - wr_hi.astype(F32)).astype(BF16)
    w = dict(
        g1=norm1_g[l][None], w_in=w_in[l].astype(BF16), gva=gv_a[l][None],
        gq=two(gq_b[l]), gk=two(gk_b[l]), gqm=two(gq_m[l]), gkm=two(gk_m[l]),
        bd=jnp.asarray(head[:, None] == head[None, :], BF16),
        bd2=jnp.asarray(head2[:, None] == head2[None, :], BF16),
        gmem=mem_norm_g[l][None], w_mem_kv=w_mem_kv[l].astype(BF16),
        w_out=w_out[l].astype(BF16), g2=norm2_g[l][None],
        w_router=jnp.concatenate([wr_hi, wr_lo], axis=1),
        b_router=jnp.pad(b_router[l], (0, LANE - N_EXPERTS), constant_values=-jnp.inf)[None],
        w_gate_up=w_gate_up[l], b_gate_up=b_gate_up[l][:, None, :], w_down=w_down[l], b_down=b_down[l][:, None, :],
    )
    ngrp = WIDTH_A // HEAD_DIM
    wtri = jnp.where(jnp.tril(jnp.ones((CHUNK, CHUNK), bool)), w_s[l], 0).astype(BF16)
    wg_p = wtri.reshape(ngrp // 2, 2 * CHUNK, CHUNK)
    bg_p = jnp.repeat(jnp.transpose(b_s[l]), HEAD_DIM, axis=1)
    zero = jnp.zeros((ngrp,), F32)
    lanes = lambda tg_: jnp.tile(jnp.repeat(tg_, HEAD_DIM, axis=1), (bd, 1))
    wg_s = jnp.stack([lanes(jnp.stack([w_s[l][:, t, t - s] if t >= s else zero for t in range(dec)]))
                      for s in range(dec)])
    bg_s = lanes(jnp.transpose(b_s[l][:, :dec]))

    xs = x_sample.reshape(bd * dec, D_MODEL)
    tabs_s = _rope_tables(np.tile(PAST_LEN + np.arange(dec), bd))
    u_s, va_s, q3_s, k3_s, v3_s, qm_s = _premix(xs, tabs_s, bd * dec, w)
    q_s, k_s, v_s = (_pair_major_to_rows(a) for a in (q3_s, k3_s, v3_s))
    nb = WIDTH_B // HEAD_DIM
    nm = WIDTH_M // HEAD_DIM
    kt = jnp.transpose(cache_win_k[l], (0, 2, 3, 1)).reshape(bd, WIDTH_B, w_buf)
    vt = jnp.transpose(cache_win_v[l], (0, 2, 3, 1)).reshape(bd, WIDTH_B, w_buf)
    kmt = jnp.transpose(cache_mem_k[l], (0, 2, 3, 1)).reshape(bd, WIDTH_M, N_MEM)
    vmt = jnp.transpose(cache_mem_v[l], (0, 2, 3, 1)).reshape(bd, WIDTH_M, N_MEM)
    sample_parts = _sample_attn_parts(q_s, k_s, v_s, qm_s, kt, vt, kmt, vmt, dec)

    xp = x_prompt.reshape(batch * seq, D_MODEL)
    tabs_p = _rope_tables(np.arange(seq))
    premix_tm = batch * seq // sample_parts["steps"]
    assert seq % premix_tm == 0 and premix_tm % 8 == 0
    km, vm = _memkv(mem_prompt.reshape(batch * N_MEM, D_MODEL), w)
    (u_p, va_p, q3_p, k3_p, v3_p, _), (ob_s8, om_s8), om_p = _premix_with_sample_attn(
        _premix_parts(xp, tabs_p, seq, w, premix_tm, gate_dtype=BF16), sample_parts,
        km.reshape(batch, N_MEM, WIDTH_M), vm.reshape(batch, N_MEM, WIDTH_M))
    ob_s, om_s = _sample_attn_rows(ob_s8, om_s8, dec)
    ob_p = _attn_prompt(q3_p, k3_p, v3_p, batch, seq)
    tiles_p = batch * seq // MOE_TM
    tiles_all = tiles_p + bd * dec // MOE_TM
    x1_p, xs_all, ld_p, tg_p, seg_p = _post(xp, u_p, va_p, ob_p, om_p, wg_p, bg_p, w, CHUNK, tiles_all)

    x1_s, xs_all, ld_s, tg_s, seg_s = _post(xs, u_s, va_s, ob_s, om_s, wg_s, bg_s, w, None, tiles_all,
                                             tile0=tiles_p, xs_all=xs_all)

    seglen = jnp.concatenate([seg_p, seg_s])[::8, :N_EXPERTS]
    y_p, y_s = _moe([(x1_p, ld_p, tg_p), (x1_s, ld_s, tg_s)], xs_all, seglen, w)
    y_prompt = y_p.reshape(batch, seq, D_MODEL)
    y_sample = y_s.reshape(bd, dec, D_MODEL)

    n_keep = min(MAX_WINDOW, seq)

    def window_rows(a3):
        a = a3.reshape(3, batch, seq, LANE)[:, :, seq - n_keep:]
        return jnp.transpose(a, (1, 2, 0, 3)).reshape(1, batch, n_keep, nb, HEAD_DIM)

    return (y_prompt, y_sample,
            window_rows(k3_p), window_rows(v3_p),
            km.reshape(1, batch, N_MEM, nm, HEAD_DIM), vm.reshape(1, batch, N_MEM, nm, HEAD_DIM),
            k_s.reshape(1, bd, dec, nb, HEAD_DIM), v_s.reshape(1, bd, dec, nb, HEAD_DIM),
            va_s.reshape(1, bd, dec, WIDTH_A))
```

```python
import functools

import numpy as np
import jax
import jax.numpy as jnp
from jax import lax
from jax.experimental import pallas as pl
from jax.experimental.pallas import tpu as pltpu

F32 = jnp.float32
BF16 = jnp.bfloat16

D_MODEL = 1024
HEAD_DIM = 64
WIDTH_A = 384
WIDTH_B = 384
WIDTH_M = 256
IN_WIDTH = 2 * WIDTH_A + 3 * WIDTH_B + WIDTH_M
CHUNK = 128
DILATIONS = ((128, 1), (512, 4), (2048, 16))
N_SUB = 128
MAX_WINDOW = 2048
N_MEM = 256
ROPE_THETA = 500000.0
ROT_HALF = 8
SCALE = HEAD_DIM ** -0.5
N_EXPERTS = 32
TOP_K = 4
SWIGLU_LIMIT = 7.0
SWIGLU_ALPHA = 1.702
EPS = 1e-6
PAST_LEN = 8192

LANE = 128
NEG = -1e30
TM = 512
SPAN = 2048
QB = 128
SAMPLE_SEQS = 2
MOE_TM = 256
GM_TM = 512
PIECE = 8
NPIECE = GM_TM // PIECE
SEG_ROWS = TOP_K * MOE_TM + N_EXPERTS * PIECE
SEG_PIECES = SEG_ROWS // PIECE
VMEM_LIMIT = 52 * 1024 * 1024


def _cparams(sem):
    return pltpu.CompilerParams(dimension_semantics=sem, vmem_limit_bytes=VMEM_LIMIT)


def _premix_body(x_ref, g1_ref, win_ref, gva_ref, gq_ref, gk_ref, gqm_ref, bd_ref,
                 rc_ref, rs1_ref, rs2_ref,
                 u_ref, va_ref, q3_ref, k3_ref, v3_ref, qm_ref):
    x = x_ref[...]
    tm = x.shape[0]
    ms = jnp.mean(x * x, axis=-1, keepdims=True)
    h = (x * lax.rsqrt(ms + EPS) * g1_ref[...]).astype(BF16)
    z = jnp.dot(h, win_ref[...], preferred_element_type=F32)
    u_ref[...] = z[:, :WIDTH_A].astype(u_ref.dtype)
    va = z[:, WIDTH_A:2 * WIDTH_A]
    va_ms = jnp.mean(va * va, axis=-1, keepdims=True)
    va_ref[...] = (va * lax.rsqrt(va_ms + EPS) * gva_ref[...]).astype(va_ref.dtype)
    q0, k0, v0, m0 = 2 * WIDTH_A, 2 * WIDTH_A + WIDTH_B, 2 * WIDTH_A + 2 * WIDTH_B, 2 * WIDTH_A + 3 * WIDTH_B
    tiles = ([z[:, q0 + LANE * j:q0 + LANE * (j + 1)] for j in range(3)]
             + [z[:, k0 + LANE * j:k0 + LANE * (j + 1)] for j in range(3)]
             + [z[:, m0 + LANE * j:m0 + LANE * (j + 1)] for j in range(2)])
    sq = [(t * t).astype(BF16) for t in tiles]
    sq = jnp.concatenate([jnp.concatenate(sq[i:i + 2], axis=1) for i in range(0, 8, 2)], axis=0)
    ssum = jnp.dot(sq, bd_ref[...], preferred_element_type=F32)
    inv = [lax.rsqrt(ssum[(i // 2) * tm:(i // 2 + 1) * tm, (i % 2) * LANE:(i % 2 + 1) * LANE] * (1.0 / HEAD_DIM) + EPS)
           for i in range(8)]
    rc, rs1, rs2 = rc_ref[...], rs1_ref[...], rs2_ref[...]

    def rope(t):
        return t * rc + pltpu.roll(t, LANE - ROT_HALF, 1) * rs1 + pltpu.roll(t, ROT_HALF, 1) * rs2

    for j in range(3):
        q3_ref[j] = rope(tiles[j] * inv[j] * gq_ref[...]) * SCALE
        k3_ref[j] = rope(tiles[3 + j] * inv[3 + j] * gk_ref[...])
        v3_ref[j] = z[:, v0 + LANE * j:v0 + LANE * (j + 1)]
    for j in range(2):
        qm_ref[:, LANE * j:LANE * (j + 1)] = (tiles[6 + j] * inv[6 + j] * gqm_ref[...] * SCALE).astype(BF16)


def _premix_parts(x, tabs, seq, w, tm, gate_dtype=F32):
    t = x.shape[0]
    n_tab_tiles = seq // tm
    full = lambda shape: pl.BlockSpec(shape, lambda i: (0,) * len(shape))
    tab = pl.BlockSpec((tm, LANE), lambda i: (i % n_tab_tiles, 0))
    row = lambda width: pl.BlockSpec((tm, width), lambda i: (i, 0))
    pair = pl.BlockSpec((3, tm, LANE), lambda i: (0, i, 0))
    return dict(
        steps=t // tm,
        in_specs=[row(D_MODEL), full((1, D_MODEL)), full((D_MODEL, IN_WIDTH)), full((1, WIDTH_A)),
                  full((1, LANE)), full((1, LANE)), full((1, LANE)), full((2 * LANE, 2 * LANE)), tab, tab, tab],
        args=[x, w["g1"], w["w_in"], w["gva"], w["gq"], w["gk"], w["gqm"], w["bd2"], *tabs],
        out_specs=[row(WIDTH_A), row(WIDTH_A), pair, pair, pair, row(WIDTH_M)],
        out_shape=[jax.ShapeDtypeStruct((t, WIDTH_A), gate_dtype), jax.ShapeDtypeStruct((t, WIDTH_A), gate_dtype),
                   jax.ShapeDtypeStruct((3, t, LANE), F32), jax.ShapeDtypeStruct((3, t, LANE), F32),
                   jax.ShapeDtypeStruct((3, t, LANE), F32), jax.ShapeDtypeStruct((t, WIDTH_M), BF16)])


def _premix(x, tabs, seq, w):
    p = _premix_parts(x, tabs, seq, w, TM)
    return pl.pallas_call(
        _premix_body, grid=(p["steps"],), in_specs=p["in_specs"], out_specs=p["out_specs"],
        out_shape=p["out_shape"], compiler_params=_cparams(("arbitrary",)), name="premix",
    )(*p["args"])


def _premix_sample_body(*refs, n_in, n_out):
    ins_a, ins_b = refs[:n_in[0]], refs[n_in[0]:n_in[0] + n_in[1]]
    km_ref, vm_ref = refs[n_in[0] + n_in[1]:n_in[0] + n_in[1] + 2]
    outs = refs[n_in[0] + n_in[1] + 2:]
    _premix_body(*ins_a, *outs[:n_out])
    _sample_attn_body(*ins_b, *outs[n_out:n_out + 2])
    _memattn_body(outs[n_out - 1], km_ref, vm_ref, outs[n_out + 2])


def _premix_with_sample_attn(a, b, km, vm):
    assert a["steps"] == b["steps"]
    tokens = a["out_shape"][0].shape[0]
    tm = tokens // a["steps"]
    steps_per_b = a["steps"] // km.shape[0]
    kv = pl.BlockSpec((1, N_MEM, WIDTH_M), lambda i: (i // steps_per_b, 0, 0))
    outs = pl.pallas_call(
        functools.partial(_premix_sample_body, n_in=(len(a["args"]), len(b["args"])), n_out=len(a["out_shape"])),
        grid=(a["steps"],), in_specs=a["in_specs"] + b["in_specs"] + [kv, kv],
        out_specs=a["out_specs"] + b["out_specs"] + [pl.BlockSpec((tm, WIDTH_M), lambda i: (i, 0))],
        out_shape=a["out_shape"] + b["out_shape"] + [jax.ShapeDtypeStruct((tokens, WIDTH_M), BF16)],
        compiler_params=_cparams(("arbitrary",)),
        name="premix_sample_attn",
    )(*a["args"], *b["args"], km, vm)
    na = len(a["out_shape"])
    return outs[:na], outs[na:na + 2], outs[na + 2]


def _rope_tables(pos):
    pos = np.asarray(pos, np.float64)
    inv_freq = np.power(ROPE_THETA, -np.arange(ROT_HALF, dtype=np.float64) / ROT_HALF)
    ang = pos[:, None] * inv_freq[None, :]
    cos, sin = np.cos(ang), np.sin(ang)
    t = pos.shape[0]
    rest = HEAD_DIM - 2 * ROT_HALF
    c = np.concatenate([cos, cos, np.ones((t, rest))], axis=1)
    s1 = np.concatenate([-sin, np.zeros((t, HEAD_DIM - ROT_HALF))], axis=1)
    s2 = np.concatenate([np.zeros((t, ROT_HALF)), sin, np.zeros((t, rest))], axis=1)
    two = lambda a: jnp.asarray(np.concatenate([a, a], axis=1), F32)
    return two(c), two(s1), two(s2)


def _attn_body(q_ref, kc_ref, kp_ref, vc_ref, vp_ref, o_ref, *scratch):
    states = [scratch[i:i + 3] for i in range(0, len(scratch), 3)]
    span_idx = pl.program_id(1)
    p0 = span_idx * SPAN
    lane = lax.broadcasted_iota(jnp.int32, (QB, LANE), 1)
    low = lane < HEAD_DIM
    qi = lax.broadcasted_iota(jnp.int32, (QB, 2 * QB), 0)
    kj = lax.broadcasted_iota(jnp.int32, (QB, 2 * QB), 1)
    band = (kj >= qi) & (kj <= qi + N_SUB)

    def rows(ref, start, d):
        if d == 1:
            return ref[0, pl.ds(start, QB), :]
        return ref[0, pl.ds(start, QB, stride=d), :]

    def unit(d, res, blk, state):
        qstart = res + d * QB * blk
        if blk == 0:
            older = (kp_ref, vp_ref, SPAN - QB * d + res)
            first_key = jnp.maximum((QB * d - p0 - res + d - 1) // d, 0)
            mask = band & (kj >= first_key)
        else:
            older = (kc_ref, vc_ref, qstart - QB * d)
            mask = band
        qb = rows(q_ref, qstart, d)
        kb = jnp.concatenate([rows(older[0], older[2], d), rows(kc_ref, qstart, d)], axis=0).astype(BF16)
        vb = jnp.concatenate([rows(older[1], older[2], d), rows(vc_ref, qstart, d)], axis=0).astype(BF16)
        stats = []
        for hm in (low, jnp.logical_not(low)):
            qh = jnp.where(hm, qb, 0.0).astype(BF16)
            s = lax.dot_general(qh, kb, (((1,), (1,)), ((), ())), preferred_element_type=F32)
            s = jnp.where(mask, s, NEG)
            m = jnp.max(s, axis=-1, keepdims=True)
            e = jnp.exp(s - m)
            l = jnp.sum(e, axis=-1, keepdims=True)
            acc = jnp.dot(e.astype(BF16), vb, preferred_element_type=F32)
            stats.append((m, l, acc))
        m_new = jnp.where(low, stats[0][0], stats[1][0])
        l_new = jnp.where(low, stats[0][1], stats[1][1])
        a_new = jnp.where(low, stats[0][2], stats[1][2])
        if state is not None:
            sl = (pl.ds(qstart, QB, stride=d), slice(None))
            state[0][sl], state[1][sl], state[2][sl] = m_new, l_new, a_new
        else:
            sl = (pl.ds(qstart, QB), slice(None))
            parts = [(m_new, l_new, a_new)] + [(st[0][sl], st[1][sl], st[2][sl]) for st in states]
            m_t = functools.reduce(jnp.maximum, [p[0] for p in parts])
            ws = [jnp.exp(p[0] - m_t) for p in parts]
            l_t = sum(w * p[1] for w, p in zip(ws, parts))
            a_t = sum(w * p[2] for w, p in zip(ws, parts))
            o_ref[sl] = (a_t / l_t).astype(o_ref.dtype)

    nblk = SPAN // QB
    for (_, d), state in zip(reversed(DILATIONS), states + [None]):
        assert (d == 1) == (state is None)
        per_res = nblk // d
        for res in range(d):
            for blk in range(per_res):
                unit(d, res, blk, state)


def _attn_prompt(q3, k3, v3, batch, seq):
    nspan = seq // SPAN
    cur = pl.BlockSpec((1, SPAN, LANE), lambda b, s, p: (p, b * nspan + s, 0))
    prv = pl.BlockSpec((1, SPAN, LANE), lambda b, s, p: (p, b * nspan + jnp.maximum(s - 1, 0), 0))
    return pl.pallas_call(
        _attn_body,
        grid=(batch, nspan, 3),
        in_specs=[cur, cur, prv, cur, prv],
        out_specs=pl.BlockSpec((SPAN, LANE), lambda b, s, p: (b * nspan + s, p)),
        out_shape=jax.ShapeDtypeStruct((batch * seq, WIDTH_B), BF16),
        scratch_shapes=[pltpu.VMEM((SPAN, LANE), F32)] * (3 * (len(DILATIONS) - 1)),
        compiler_params=_cparams(("arbitrary", "arbitrary", "arbitrary")),
        name="attn_prompt",
    )(q3, k3, k3, v3, v3)


def _memkv_body(mem_ref, g_ref, w_ref, gk_ref, bd_ref, k_ref, v_ref):
    x = mem_ref[...]
    ms = jnp.mean(x * x, axis=-1, keepdims=True)
    h = (x * lax.rsqrt(ms + EPS) * g_ref[...]).astype(BF16)
    kv = jnp.dot(h, w_ref[...], preferred_element_type=F32)
    n = x.shape[0]
    kt = [kv[:, LANE * j:LANE * (j + 1)] for j in range(2)]
    sq = jnp.concatenate([(t * t).astype(BF16) for t in kt], axis=0)
    ssum = jnp.dot(sq, bd_ref[...], preferred_element_type=F32)
    for j in range(2):
        inv = lax.rsqrt(ssum[j * n:(j + 1) * n] * (1.0 / HEAD_DIM) + EPS)
        k_ref[:, LANE * j:LANE * (j + 1)] = kt[j] * inv * gk_ref[...]
    v_ref[...] = kv[:, WIDTH_M:]


def _memkv(mem, w):
    n = mem.shape[0]
    return pl.pallas_call(
        _memkv_body,
        out_shape=[jax.ShapeDtypeStruct((n, WIDTH_M), F32)] * 2,
        compiler_params=pltpu.CompilerParams(vmem_limit_bytes=VMEM_LIMIT),
        name="memkv",
    )(mem, w["gmem"], w["w_mem_kv"], w["gkm"], w["bd"])


def _memattn_body(q_ref, k_ref, v_ref, o_ref):
    lane = lax.broadcasted_iota(jnp.int32, (q_ref.shape[0], LANE), 1)
    low = lane < HEAD_DIM
    for j in range(2):
        qp = q_ref[:, LANE * j:LANE * (j + 1)].astype(F32)
        kp = k_ref[0, :, LANE * j:LANE * (j + 1)].astype(BF16)
        vp = v_ref[0, :, LANE * j:LANE * (j + 1)].astype(BF16)
        outs = []
        for hm in (low, jnp.logical_not(low)):
            qh = jnp.where(hm, qp, 0.0).astype(BF16)
            s = lax.dot_general(qh, kp, (((1,), (1,)), ((), ())), preferred_element_type=F32)
            m = jnp.max(s, axis=-1, keepdims=True)
            e = jnp.exp(s - m)
            l = jnp.sum(e, axis=-1, keepdims=True)
            outs.append(jnp.dot(e.astype(BF16), vp, preferred_element_type=F32) / l)
        o_ref[:, LANE * j:LANE * (j + 1)] = jnp.where(low, outs[0], outs[1]).astype(o_ref.dtype)


def _sample_attn_body(*refs):
    for i in range(refs[0].shape[0]):
        _sample_attn_one(i, *refs)


def _sample_attn_one(i, qbd_ref, kt_ref, vt_ref, kn_ref, vn_ref, cnt_ref, cntn_ref, hmask_ref,
                     qmbd_ref, kmt_ref, vmt_ref, hmaskm_ref, ob_ref, om_ref):
    dec = kn_ref.shape[1]
    qbd = qbd_ref[i]
    kt = kt_ref[i].astype(BF16)
    vt = vt_ref[i].astype(BF16)
    s = jnp.dot(qbd, kt, preferred_element_type=F32)
    qf = qbd.astype(F32)
    kn = kn_ref[i]
    vn = vn_ref[i]
    cnt = cnt_ref[...]
    cntn = cntn_ref[...]
    s_new = [jnp.sum(qf * kn[j:j + 1, :], axis=-1, keepdims=True) for j in range(dec)]
    m = jnp.max(jnp.where(cnt > 0, s, NEG), axis=-1, keepdims=True)
    for j in range(dec):
        m = jnp.maximum(m, jnp.where(cntn[:, j:j + 1] > 0, s_new[j], NEG))
    e = cnt * jnp.exp(jnp.where(cnt > 0, s - m, 0.0))
    l = jnp.sum(e, axis=-1, keepdims=True)
    acc = lax.dot_general(e.astype(BF16), vt, (((1,), (1,)), ((), ())), preferred_element_type=F32)
    for j in range(dec):
        w = cntn[:, j:j + 1]
        ej = w * jnp.exp(jnp.where(w > 0, s_new[j] - m, 0.0))
        l = l + ej
        acc = acc + ej * vn[j:j + 1, :]
    r = acc / l * hmask_ref[...]
    out = r[0:8]
    for h in range(1, WIDTH_B // HEAD_DIM):
        out = out + r[8 * h:8 * h + 8]
    ob_ref[i] = out
    qm = qmbd_ref[i]
    sm = jnp.dot(qm, kmt_ref[i].astype(BF16), preferred_element_type=F32)
    mm = jnp.max(sm, axis=-1, keepdims=True)
    em = jnp.exp(sm - mm)
    lm = jnp.sum(em, axis=-1, keepdims=True)
    am = lax.dot_general(em.astype(BF16), vmt_ref[i].astype(BF16), (((1,), (1,)), ((), ())),
                         preferred_element_type=F32)
    rm = am / lm * hmaskm_ref[...]
    outm = rm[0:8]
    for h in range(1, WIDTH_M // HEAD_DIM):
        outm = outm + rm[8 * h:8 * h + 8]
    om_ref[i] = outm


def _sample_counts(dec, w_buf):
    t = np.arange(8)[:, None]
    t = np.where(t < dec, t, 0)
    def mult(dist):
        c = np.zeros(dist.shape, np.float32)
        for window, dil in DILATIONS:
            c += ((dist >= 0) & (dist % dil == 0) & (dist <= window)).astype(np.float32)
        return c
    cache = mult(w_buf + t - np.arange(w_buf)[None, :])
    new = mult(t - np.arange(dec)[None, :])
    nb, nm = WIDTH_B // HEAD_DIM, WIDTH_M // HEAD_DIM
    hmask = (np.arange(8 * nb)[:, None] // 8 == np.arange(WIDTH_B)[None, :] // HEAD_DIM).astype(np.float32)
    hmaskm = (np.arange(8 * nm)[:, None] // 8 == np.arange(WIDTH_M)[None, :] // HEAD_DIM).astype(np.float32)
    return np.tile(cache, (nb, 1)), np.tile(new, (nb, 1)), hmask, hmaskm


def _block_diag_queries(q, dec, hmask):
    width = q.shape[-1]
    nh = width // HEAD_DIM
    qb = q.reshape(-1, 1, dec, width)
    qb = jnp.pad(qb, ((0, 0), (0, 0), (0, 8 - dec), (0, 0)))
    qb = jnp.broadcast_to(qb, (qb.shape[0], nh, 8, width)).reshape(-1, 8 * nh, width)
    return (qb * hmask[None]).astype(BF16)


def _sample_attn_parts(q, kn, vn, qm, kt, vt, kmt, vmt, dec):
    bd = kt.shape[0]
    w_buf = kt.shape[-1]
    cnt, cntn, hmask, hmaskm = _sample_counts(dec, w_buf)
    qbd = _block_diag_queries(q, dec, hmask)
    qmbd = _block_diag_queries(qm.astype(F32), dec, hmaskm)
    nb8, nm8 = qbd.shape[1], qmbd.shape[1]
    per_b = lambda shape: pl.BlockSpec((SAMPLE_SEQS,) + shape, lambda b: (b,) + (0,) * len(shape))
    full = lambda shape: pl.BlockSpec(shape, lambda b: (0,) * len(shape))
    return dict(
        steps=bd // SAMPLE_SEQS,
        in_specs=[per_b((nb8, WIDTH_B)), per_b((WIDTH_B, w_buf)), per_b((WIDTH_B, w_buf)),
                  per_b((dec, WIDTH_B)), per_b((dec, WIDTH_B)),
                  full((nb8, w_buf)), full((nb8, dec)), full((nb8, WIDTH_B)),
                  per_b((nm8, WIDTH_M)), per_b((WIDTH_M, N_MEM)), per_b((WIDTH_M, N_MEM)), full((nm8, WIDTH_M))],
        args=[qbd, kt, vt, kn.reshape(bd, dec, WIDTH_B), vn.reshape(bd, dec, WIDTH_B),
              jnp.asarray(cnt), jnp.asarray(cntn), jnp.asarray(hmask), qmbd, kmt, vmt, jnp.asarray(hmaskm)],
        out_specs=[per_b((8, WIDTH_B)), per_b((8, WIDTH_M))],
        out_shape=[jax.ShapeDtypeStruct((bd, 8, WIDTH_B), F32), jax.ShapeDtypeStruct((bd, 8, WIDTH_M), F32)])


def _sample_attn_rows(ob, om, dec):
    return (ob[:, :dec].reshape(-1, WIDTH_B).astype(BF16), om[:, :dec].reshape(-1, WIDTH_M).astype(BF16))


def _post_body(*refs, tc, aliased):
    (x_ref, u_ref, va_ref, ob_ref, om_ref, wg_ref, bg_ref, wout_ref, g2_ref, wr_ref, br_ref,
     tri_ref, upper_ref) = refs[:13]
    x1_ref, xs_ref, ld_ref, tg_ref, seg_ref = refs[13 + aliased:]
    tm = x_ref.shape[0]
    u = u_ref[...].astype(F32)
    if tc is None:
        vaf = va_ref[...]
        mixed = wg_ref[0] * vaf + bg_ref[...]
        for s in range(1, wg_ref.shape[0]):
            mixed = mixed + wg_ref[s] * pltpu.roll(vaf, s, 0)
        oa = u * mixed
    else:
        lane = lax.broadcasted_iota(jnp.int32, (tc, LANE), 1)
        low = lane < HEAD_DIM
        va = va_ref[...].astype(BF16)
        oa_rows = []
        for c in range(tm // tc):
            r0 = c * tc
            tiles = []
            for p in range(3):
                vp = va[r0:r0 + tc, LANE * p:LANE * (p + 1)]
                r = jnp.dot(wg_ref[p], vp, preferred_element_type=F32)
                tiles.append(jnp.where(low, r[:tc], r[tc:]))
            mixed = jnp.concatenate(tiles, axis=1) + bg_ref[...]
            oa_rows.append(u[r0:r0 + tc] * mixed)
        oa = jnp.concatenate(oa_rows, axis=0)
    mixed_all = jnp.concatenate([oa.astype(BF16), ob_ref[...], om_ref[...]], axis=1)
    x1 = x_ref[...] + jnp.dot(mixed_all, wout_ref[...], preferred_element_type=F32)
    x1_ref[...] = x1
    ms = jnp.mean(x1 * x1, axis=-1, keepdims=True)
    h2 = x1 * lax.rsqrt(ms + EPS) * g2_ref[...]
    h_hi = h2.astype(BF16)
    h_lo = (h2 - h_hi.astype(F32)).astype(BF16)
    hw = jnp.dot(h_hi, wr_ref[...], preferred_element_type=F32)
    logits = (hw[:, :LANE] + hw[:, LANE:]
              + jnp.dot(h_lo, wr_ref[:, :LANE], preferred_element_type=F32)) + br_ref[...]
    lane_i = lax.broadcasted_iota(jnp.int32, (tm, LANE), 1)
    lane_r = lane_i.astype(F32)
    vals = logits
    tops, idxs = [], []
    for _ in range(TOP_K):
        mk = jnp.max(vals, axis=-1, keepdims=True)
        ik = jnp.min(jnp.where(vals == mk, lane_r, float(LANE)), axis=-1, keepdims=True)
        vals = jnp.where(lane_r == ik, -jnp.inf, vals)
        tops.append(mk)
        idxs.append(ik)
    es = [jnp.exp(t - tops[0]) for t in tops]
    den = es[0] + es[1] + es[2] + es[3]
    tg = jnp.zeros((tm, LANE), F32)
    for k in range(TOP_K):
        tg = jnp.where(lane_i == k, es[k] / den, tg)
    tg_ref[...] = tg
    col = lax.broadcasted_iota(jnp.int32, (MOE_TM, SEG_ROWS), 1).astype(F32)
    lane_t = lax.broadcasted_iota(jnp.int32, (MOE_TM, LANE), 1)
    hot_all = [lane_r == idxs[k] for k in range(TOP_K)]
    for hf in range(tm // MOE_TM):
        r0 = hf * MOE_TM
        hot = [h[r0:r0 + MOE_TM] for h in hot_all]
        sel = jnp.zeros((MOE_TM, LANE), F32)
        for k in range(TOP_K):
            sel = sel + jnp.where(hot[k], 1.0, 0.0)
        rank = jnp.dot(tri_ref[...], sel.astype(BF16), preferred_element_type=F32)
        length = jnp.sum(sel, axis=0, keepdims=True)
        plen = jnp.floor((length + (PIECE - 1)) * (1.0 / PIECE)) * PIECE
        loff = jnp.dot(jnp.broadcast_to(plen, (8, LANE)).astype(BF16), upper_ref[...],
                       preferred_element_type=F32)[0:1]
        base = loff + rank
        hit = None
        ld = jnp.zeros((MOE_TM, LANE), F32)
        for k in range(TOP_K):
            ld_k = jnp.sum(jnp.where(hot[k], base, 0.0), axis=-1, keepdims=True)
            hit = (col == ld_k) if hit is None else hit | (col == ld_k)
            ld = jnp.where(lane_t == k, ld_k, ld)
        q = jnp.where(hit, 1.0, 0.0)
        xs_ref[hf * SEG_ROWS:(hf + 1) * SEG_ROWS, :] = lax.dot_general(
            q.astype(BF16), h_hi[r0:r0 + MOE_TM], (((0,), (0,)), ((), ())),
            preferred_element_type=F32).astype(xs_ref.dtype)
        ld_ref[r0:r0 + MOE_TM, :] = ld.astype(jnp.int32)
        seg_ref[hf * 8:(hf + 1) * 8, :] = jnp.broadcast_to(length, (8, LANE)).astype(jnp.int32)


def _post(x, u, va, ob, om, wg, bg, w, tc, total_tiles, tile0=0, xs_all=None):
    t = x.shape[0]
    nt = t // TM
    full = lambda shape: pl.BlockSpec(shape, lambda i: (0,) * len(shape))
    row = lambda width: pl.BlockSpec((TM, width), lambda i: (i, 0))
    per = TM // MOE_TM
    ix = np.arange(MOE_TM)
    tri = jnp.asarray(ix[:, None] > ix[None, :], BF16)
    ex = np.arange(LANE)
    upper = jnp.asarray(ex[:, None] < ex[None, :], BF16)
    step0 = tile0 // per
    in_specs = [row(D_MODEL), row(WIDTH_A), row(WIDTH_A), row(WIDTH_B), row(WIDTH_M),
                full(wg.shape), full(bg.shape), full((D_MODEL, D_MODEL)), full((1, D_MODEL)),
                full((D_MODEL, 2 * LANE)), full((1, LANE)), full((MOE_TM, MOE_TM)), full((LANE, LANE))]
    args = [x, u, va, ob, om, wg, bg, w["w_out"], w["g2"], w["w_router"], w["b_router"], tri, upper]
    aliases = {}
    if xs_all is not None:
        in_specs.append(pl.BlockSpec(memory_space=pl.ANY))
        args.append(xs_all)
        aliases = {len(args) - 1: 1}
    return pl.pallas_call(
        functools.partial(_post_body, tc=tc, aliased=int(xs_all is not None)),
        grid=(nt,),
        in_specs=in_specs,
        out_specs=[row(D_MODEL), pl.BlockSpec((per * SEG_ROWS, D_MODEL), lambda i: (i + step0, 0)),
                   row(LANE), row(LANE), pl.BlockSpec((per * 8, LANE), lambda i: (i, 0))],
        out_shape=[jax.ShapeDtypeStruct((t, D_MODEL), F32),
                   jax.ShapeDtypeStruct((total_tiles * SEG_ROWS, D_MODEL), BF16),
                   jax.ShapeDtypeStruct((t, LANE), jnp.int32), jax.ShapeDtypeStruct((t, LANE), F32),
                   jax.ShapeDtypeStruct((t // MOE_TM * 8, LANE), jnp.int32)],
        input_output_aliases=aliases,
        compiler_params=_cparams(("arbitrary",)),
        name="post",
    )(*args)


def _experts_body(te_ref, valid_ref, first_ref, next_ref, half_ref, src_ref, dst_ref,
                  xs_hbm, wgu_hbm, bgu_ref, wd_hbm, bd_ref, ys_hbm,
                  xbuf, ybuf, wgu_f, wd_f, wgu_s, wd_s, gsem, ssem, wsem):
    t = pl.program_id(0)
    nt = pl.num_programs(0)
    slot = t % 2

    def gather(tile, sl):
        for i in range(NPIECE):
            s = src_ref[tile * NPIECE + i]
            pltpu.make_async_copy(xs_hbm.at[pl.ds(pl.multiple_of(s * PIECE, PIECE), PIECE), :],
                                  xbuf.at[sl, pl.ds(i * PIECE, PIECE), :], gsem.at[sl]).start()

    def scatter(tile, sl):
        for i in range(NPIECE):
            d = dst_ref[tile * NPIECE + i]
            pltpu.make_async_copy(ybuf.at[sl, pl.ds(i * PIECE, PIECE), :],
                                  ys_hbm.at[pl.ds(pl.multiple_of(d * PIECE, PIECE), PIECE), :], ssem.at[sl]).start()

    def wait_tile(hbm, buf, sem, sl):
        pltpu.make_async_copy(hbm.at[pl.ds(0, GM_TM), :], buf.at[sl], sem.at[sl]).wait()

    def weight_copies(e):
        return (pltpu.make_async_copy(wgu_hbm.at[e], wgu_f, wsem.at[0]),
                pltpu.make_async_copy(wd_hbm.at[e], wd_f, wsem.at[1]))

    @pl.when(t == 0)
    def _():
        for c in weight_copies(te_ref[0]):
            c.start()
        gather(0, 0)
        ybuf[...] = jnp.zeros(ybuf.shape, ybuf.dtype)

    @pl.when(valid_ref[t] > 0)
    def _():
        nxt = jnp.minimum(t + 1, nt - 1)
        has_next = jnp.logical_and(t + 1 < nt, valid_ref[nxt] > 0)

        @pl.when(first_ref[t] > 0)
        def _():
            for c in weight_copies(te_ref[t]):
                c.wait()
            wgu_s[...] = wgu_f[...].astype(BF16)
            wd_s[...] = wd_f[...].astype(BF16)

            @pl.when(next_ref[t] >= 0)
            def _():
                for c in weight_copies(next_ref[t]):
                    c.start()

        @pl.when(t >= 2)
        def _():
            wait_tile(ys_hbm, ybuf, ssem, slot)

        gather(jnp.where(has_next, t + 1, t), 1 - slot)
        wait_tile(xs_hbm, xbuf, gsem, slot)

        def ffn(rows):
            x = xbuf[slot, pl.ds(0, rows), :]
            gu = jnp.dot(x, wgu_s[...], preferred_element_type=F32) + bgu_ref[0]
            gate = jnp.minimum(gu[:, :D_MODEL], SWIGLU_LIMIT)
            up = jnp.clip(gu[:, D_MODEL:], -SWIGLU_LIMIT, SWIGLU_LIMIT)
            act = (up + 1.0) * (gate * (1.0 / (1.0 + jnp.exp(-SWIGLU_ALPHA * gate))))
            y = jnp.dot(act.astype(BF16), wd_s[...], preferred_element_type=F32) + bd_ref[0]
            ybuf[slot, pl.ds(0, rows), :] = y.astype(ybuf.dtype)

        @pl.when(half_ref[t] > 0)
        def _():
            ffn(GM_TM // 2)

        @pl.when(half_ref[t] == 0)
        def _():
            ffn(GM_TM)

        scatter(t, slot)

        @pl.when(jnp.logical_not(has_next))
        def _():
            wait_tile(xs_hbm, xbuf, gsem, 1 - slot)
            wait_tile(ys_hbm, ybuf, ssem, slot)

            @pl.when(t >= 1)
            def _():
                wait_tile(ys_hbm, ybuf, ssem, 1 - slot)


def _experts(plan, xs, w, ys_rows):
    n_tiles = plan["tile_expert"].shape[0]
    by_expert = lambda shape: pl.BlockSpec((1,) + shape, lambda t, te, *_: (te[t],) + (0,) * len(shape))
    hbm = pl.BlockSpec(memory_space=pl.ANY)
    return pl.pallas_call(
        _experts_body,
        grid_spec=pltpu.PrefetchScalarGridSpec(
            num_scalar_prefetch=7, grid=(n_tiles,),
            in_specs=[hbm, hbm, by_expert((1, 2 * D_MODEL)), hbm, by_expert((1, D_MODEL))],
            out_specs=hbm,
            scratch_shapes=[pltpu.VMEM((2, GM_TM, D_MODEL), BF16), pltpu.VMEM((2, GM_TM, D_MODEL), BF16),
                            pltpu.VMEM((D_MODEL, 2 * D_MODEL), F32), pltpu.VMEM((D_MODEL, D_MODEL), F32),
                            pltpu.VMEM((D_MODEL, 2 * D_MODEL), BF16), pltpu.VMEM((D_MODEL, D_MODEL), BF16),
                            pltpu.SemaphoreType.DMA((2,)), pltpu.SemaphoreType.DMA((2,)),
                            pltpu.SemaphoreType.DMA((2,))]),
        out_shape=jax.ShapeDtypeStruct((ys_rows, D_MODEL), BF16),
        compiler_params=_cparams(("arbitrary",)),
        name="moe_experts",
    )(plan["tile_expert"], plan["tile_valid"], plan["tile_first"], plan["tile_next"], plan["tile_half"],
      plan["src"], plan["dst"],
      xs, w["w_gate_up"], w["b_gate_up"], w["w_down"], w["b_down"])


def _combine_body(used_ref, ld_ref, g_ref, x1_ref, ys_ref, y_ref, *, tile0):
    t = pl.program_id(0)
    row = lax.broadcasted_iota(jnp.int32, (SEG_ROWS, 1), 0)
    ys = jnp.where(row < used_ref[tile0 + t], ys_ref[...].astype(F32), 0.0).astype(BF16)
    col = lax.broadcasted_iota(jnp.int32, (MOE_TM, SEG_ROWS), 1)
    ld = ld_ref[...]
    g = g_ref[...]
    p = jnp.zeros((MOE_TM, SEG_ROWS), F32)
    for k in range(TOP_K):
        p = jnp.where(col == ld[:, k:k + 1], g[:, k:k + 1], p)
    y_ref[...] = x1_ref[...] + jnp.dot(p.astype(BF16), ys, preferred_element_type=F32)


def _combine(used, ld, gates, x1, ys, tile0):
    n = x1.shape[0]
    nt = n // MOE_TM
    rows = lambda width: pl.BlockSpec((MOE_TM, width), lambda t, *_: (t, 0))
    return pl.pallas_call(
        functools.partial(_combine_body, tile0=tile0),
        grid_spec=pltpu.PrefetchScalarGridSpec(
            num_scalar_prefetch=1, grid=(nt,),
            in_specs=[rows(LANE), rows(LANE), rows(D_MODEL),
                      pl.BlockSpec((SEG_ROWS, D_MODEL), lambda t, *_: (t + tile0, 0))],
            out_specs=rows(D_MODEL)),
        out_shape=jax.ShapeDtypeStruct((n, D_MODEL), F32),
        compiler_params=_cparams(("arbitrary",)),
        name="moe_combine",
    )(used, ld, gates, x1, ys)


def _moe(groups, xs, seglen, w):
    i32 = jnp.int32
    nt = seglen.shape[0]
    plen = (seglen + PIECE - 1) // PIECE * PIECE
    loff = jnp.cumsum(plen, axis=1) - plen
    used = jnp.sum(plen, axis=1).astype(i32)
    pp = plen // PIECE
    cp_end = jnp.cumsum(pp, axis=0)
    cp = cp_end - pp
    cnt_e = cp_end[-1]
    tiles_e = (cnt_e + NPIECE - 1) // NPIECE
    tile_end = jnp.cumsum(tiles_e)
    tile_start = tile_end - tiles_e
    n_tiles = (nt * SEG_ROWS + GM_TM - 1) // GM_TM + N_EXPERTS
    tix = jnp.arange(n_tiles, dtype=i32)
    total_tiles = tile_end[-1]
    tile_valid = (tix < total_tiles).astype(i32)
    expert_at = lambda tile: jnp.minimum((tile[:, None] >= tile_end[None, :]).astype(i32).sum(axis=1), N_EXPERTS - 1)
    last_expert = expert_at(jnp.maximum(total_tiles - 1, 0)[None])[0]
    tile_expert = jnp.where(tile_valid > 0, expert_at(tix), last_expert)
    hot_e = tile_expert[:, None] == jnp.arange(N_EXPERTS, dtype=i32)[None, :]
    per_tile = lambda v: jnp.sum(jnp.where(hot_e, v[None, :], 0), axis=1)
    per_tile_rows = lambda m: jnp.sum(jnp.where(hot_e[:, :, None], jnp.transpose(m)[None], 0), axis=1)
    start_t = per_tile(tile_start)
    tile_first = ((tix == start_t) & (tile_valid > 0)).astype(i32)
    following = per_tile(tile_end)
    tile_next = jnp.where(following < total_tiles, expert_at(following), -1)
    j = (tix - start_t)[:, None] * NPIECE + jnp.arange(NPIECE, dtype=i32)[None, :]
    cnt_t = per_tile(cnt_e)
    ok = (tile_valid[:, None] > 0) & (j < cnt_t[:, None])
    tile_half = ((tile_valid > 0) & (cnt_t - (tix - start_t) * NPIECE <= NPIECE // 2)).astype(i32)
    ends_t, cp_t, loff_t = per_tile_rows(cp_end), per_tile_rows(cp), per_tile_rows(loff)
    t_q = jnp.minimum((ends_t[:, None, :] <= j[:, :, None]).astype(i32).sum(axis=2), nt - 1)
    hot_t = t_q[:, :, None] == jnp.arange(nt, dtype=i32)[None, None, :]
    at_t = lambda m: jnp.sum(jnp.where(hot_t, m[:, None, :], 0), axis=2)
    piece = t_q * SEG_PIECES + at_t(loff_t) // PIECE + j - at_t(cp_t)
    src = jnp.where(ok, piece, 0)
    dump = nt * SEG_PIECES + (tix % 2)[:, None] * NPIECE + jnp.arange(NPIECE, dtype=i32)[None, :]
    dst = jnp.where(ok, piece, dump)
    plan = dict(tile_expert=tile_expert.astype(i32), tile_valid=tile_valid, tile_first=tile_first,
                tile_next=tile_next.astype(i32), tile_half=tile_half,
                src=src.reshape(-1).astype(i32), dst=dst.reshape(-1).astype(i32))

    ys = _experts(plan, xs, w, nt * SEG_ROWS + 2 * GM_TM)
    outs, r0 = [], 0
    for x1, ld, tg in groups:
        outs.append(_combine(used, ld, tg, x1, ys, r0 // MOE_TM))
        r0 += x1.shape[0]
    return outs


def _pair_major_to_rows(a3):
    return jnp.transpose(a3, (1, 0, 2)).reshape(a3.shape[1], 3 * LANE)


def kernel(x_prompt, x_sample, mem_prompt, cache_win_k, cache_win_v, cache_mem_k, cache_mem_v, norm1_g, w_in, gv_a, w_s, b_s, gq_b, gk_b, gq_m, gk_m, mem_norm_g, w_mem_kv, w_out, norm2_g, w_router, b_router, w_gate_up, b_gate_up, w_down, b_down):
    batch, seq, _ = x_prompt.shape
    bd, dec, _ = x_sample.shape
    depth = norm1_g.shape[0]
    assert depth == 1 and seq % SPAN == 0 and (bd * dec) % TM == 0 and PAST_LEN % CHUNK == 0
    w_buf = cache_win_k.shape[2]
    assert w_buf == MAX_WINDOW and dec <= 8
    l = 0
    two = lambda g: jnp.concatenate([g, g])[None, :]
    head = np.arange(LANE) // HEAD_DIM
    head2 = np.arange(2 * LANE) // HEAD_DIM
    wr = jnp.pad(w_router[l], ((0, 0), (0, LANE - N_EXPERTS)))
    wr_hi = wr.astype(BF16)
    wr_lo = (wr - wr_hi.astype(F32)).astype(BF16)
    w = dict(
        g1=norm1_g[l][None], w_in=w_in[l].astype(BF16), gva=gv_a[l][None],
        gq=two(gq_b[l]), gk=two(gk_b[l]), gqm=two(gq_m[l]), gkm=two(gk_m[l]),
        bd=jnp.asarray(head[:, None] == head[None, :], BF16),
        bd2=jnp.asarray(head2[:, None] == head2[None, :], BF16),
        gmem=mem_norm_g[l][None], w_mem_kv=w_mem_kv[l].astype(BF16),
        w_out=w_out[l].astype(BF16), g2=norm2_g[l][None],
        w_router=jnp.concatenate([wr_hi, wr_lo], axis=1),
        b_router=jnp.pad(b_router[l], (0, LANE - N_EXPERTS), constant_values=-jnp.inf)[None],
        w_gate_up=w_gate_up[l], b_gate_up=b_gate_up[l][:, None, :], w_down=w_down[l], b_down=b_down[l][:, None, :],
    )
    ngrp = WIDTH_A // HEAD_DIM
    wtri = jnp.where(jnp.tril(jnp.ones((CHUNK, CHUNK), bool)), w_s[l], 0).astype(BF16)
    wg_p = wtri.reshape(ngrp // 2, 2 * CHUNK, CHUNK)
    bg_p = jnp.repeat(jnp.transpose(b_s[l]), HEAD_DIM, axis=1)
    zero = jnp.zeros((ngrp,), F32)
    lanes = lambda tg_: jnp.tile(jnp.repeat(tg_, HEAD_DIM, axis=1), (bd, 1))
    wg_s = jnp.stack([lanes(jnp.stack([w_s[l][:, t, t - s] if t >= s else zero for t in range(dec)]))
                      for s in range(dec)])
    bg_s = lanes(jnp.transpose(b_s[l][:, :dec]))

    xs = x_sample.reshape(bd * dec, D_MODEL)
    tabs_s = _rope_tables(np.tile(PAST_LEN + np.arange(dec), bd))
    u_s, va_s, q3_s, k3_s, v3_s, qm_s = _premix(xs, tabs_s, bd * dec, w)
    q_s, k_s, v_s = (_pair_major_to_rows(a) for a in (q3_s, k3_s, v3_s))
    nb = WIDTH_B // HEAD_DIM
    nm = WIDTH_M // HEAD_DIM
    kt = jnp.transpose(cache_win_k[l], (0, 2, 3, 1)).reshape(bd, WIDTH_B, w_buf)
    vt = jnp.transpose(cache_win_v[l], (0, 2, 3, 1)).reshape(bd, WIDTH_B, w_buf)
    kmt = jnp.transpose(cache_mem_k[l], (0, 2, 3, 1)).reshape(bd, WIDTH_M, N_MEM)
    vmt = jnp.transpose(cache_mem_v[l], (0, 2, 3, 1)).reshape(bd, WIDTH_M, N_MEM)
    sample_parts = _sample_attn_parts(q_s, k_s, v_s, qm_s, kt, vt, kmt, vmt, dec)

    xp = x_prompt.reshape(batch * seq, D_MODEL)
    tabs_p = _rope_tables(np.arange(seq))
    premix_tm = batch * seq // sample_parts["steps"]
    assert seq % premix_tm == 0 and premix_tm % 8 == 0
    km, vm = _memkv(mem_prompt.reshape(batch * N_MEM, D_MODEL), w)
    (u_p, va_p, q3_p, k3_p, v3_p, _), (ob_s8, om_s8), om_p = _premix_with_sample_attn(
        _premix_parts(xp, tabs_p, seq, w, premix_tm, gate_dtype=BF16), sample_parts,
        km.reshape(batch, N_MEM, WIDTH_M), vm.reshape(batch, N_MEM, WIDTH_M))
    ob_s, om_s = _sample_attn_rows(ob_s8, om_s8, dec)
    ob_p = _attn_prompt(q3_p, k3_p, v3_p, batch, seq)
    tiles_p = batch * seq // MOE_TM
    tiles_all = tiles_p + bd * dec // MOE_TM
    x1_p, xs_all, ld_p, tg_p, seg_p = _post(xp, u_p, va_p, ob_p, om_p, wg_p, bg_p, w, CHUNK, tiles_all)

    x1_s, xs_all, ld_s, tg_s, seg_s = _post(xs, u_s, va_s, ob_s, om_s, wg_s, bg_s, w, None, tiles_all,
                                             tile0=tiles_p, xs_all=xs_all)

    seglen = jnp.concatenate([seg_p, seg_s])[::8, :N_EXPERTS]
    y_p, y_s = _moe([(x1_p, ld_p, tg_p), (x1_s, ld_s, tg_s)], xs_all, seglen, w)
    y_prompt = y_p.reshape(batch, seq, D_MODEL)
    y_sample = y_s.reshape(bd, dec, D_MODEL)

    n_keep = min(MAX_WINDOW, seq)

    def window_rows(a3):
        a = a3.reshape(3, batch, seq, LANE)[:, :, seq - n_keep:]
        return jnp.transpose(a, (1, 2, 0, 3)).reshape(1, batch, n_keep, nb, HEAD_DIM)

    return (y_prompt, y_sample,
            window_rows(k3_p), window_rows(v3_p),
            km.reshape(1, batch, N_MEM, nm, HEAD_DIM), vm.reshape(1, batch, N_MEM, nm, HEAD_DIM),
            k_s.reshape(1, bd, dec, nb, HEAD_DIM), v_s.reshape(1, bd, dec, nb, HEAD_DIM),
            va_s.reshape(1, bd, dec, WIDTH_A))
```

```python
import functools

import numpy as np
import jax
import jax.numpy as jnp
from jax import lax
from jax.experimental import pallas as pl
from jax.experimental.pallas import tpu as pltpu

F32 = jnp.float32
BF16 = jnp.bfloat16

D_MODEL = 1024
HEAD_DIM = 64
WIDTH_A = 384
WIDTH_B = 384
WIDTH_M = 256
IN_WIDTH = 2 * WIDTH_A + 3 * WIDTH_B + WIDTH_M
CHUNK = 128
DILATIONS = ((128, 1), (512, 4), (2048, 16))
N_SUB = 128
MAX_WINDOW = 2048
N_MEM = 256
ROPE_THETA = 500000.0
ROT_HALF = 8
SCALE = HEAD_DIM ** -0.5
N_EXPERTS = 32
TOP_K = 4
SWIGLU_LIMIT = 7.0
SWIGLU_ALPHA = 1.702
EPS = 1e-6
PAST_LEN = 8192

LANE = 128
NEG = -1e30
TM = 512
SPAN = 2048
QB = 128
SAMPLE_SEQS = 2
MOE_TM = 256
GM_TM = 512
PIECE = 8
NPIECE = GM_TM // PIECE
SEG_ROWS = TOP_K * MOE_TM + N_EXPERTS * PIECE
SEG_PIECES = SEG_ROWS // PIECE
VMEM_LIMIT = 52 * 1024 * 1024


def _cparams(sem):
    return pltpu.CompilerParams(dimension_semantics=sem, vmem_limit_bytes=VMEM_LIMIT)


def _premix_body(x_ref, g1_ref, win_ref, gva_ref, gq_ref, gk_ref, gqm_ref, bd_ref,
                 rc_ref, rs1_ref, rs2_ref,
                 u_ref, va_ref, q3_ref, k3_ref, v3_ref, qm_ref):
    x = x_ref[...]
    tm = x.shape[0]
    ms = jnp.mean(x * x, axis=-1, keepdims=True)
    h = (x * lax.rsqrt(ms + EPS) * g1_ref[...]).astype(BF16)
    z = jnp.dot(h, win_ref[...], preferred_element_type=F32)
    u_ref[...] = z[:, :WIDTH_A].astype(u_ref.dtype)
    va = z[:, WIDTH_A:2 * WIDTH_A]
    va_ms = jnp.mean(va * va, axis=-1, keepdims=True)
    va_ref[...] = (va * lax.rsqrt(va_ms + EPS) * gva_ref[...]).astype(va_ref.dtype)
    q0, k0, v0, m0 = 2 * WIDTH_A, 2 * WIDTH_A + WIDTH_B, 2 * WIDTH_A + 2 * WIDTH_B, 2 * WIDTH_A + 3 * WIDTH_B
    tiles = ([z[:, q0 + LANE * j:q0 + LANE * (j + 1)] for j in range(3)]
             + [z[:, k0 + LANE * j:k0 + LANE * (j + 1)] for j in range(3)]
             + [z[:, m0 + LANE * j:m0 + LANE * (j + 1)] for j in range(2)])
    sq = [(t * t).astype(BF16) for t in tiles]
    sq = jnp.concatenate([jnp.concatenate(sq[i:i + 2], axis=1) for i in range(0, 8, 2)], axis=0)
    ssum = jnp.dot(sq, bd_ref[...], preferred_element_type=F32)
    inv = [lax.rsqrt(ssum[(i // 2) * tm:(i // 2 + 1) * tm, (i % 2) * LANE:(i % 2 + 1) * LANE] * (1.0 / HEAD_DIM) + EPS)
           for i in range(8)]
    rc, rs1, rs2 = rc_ref[...], rs1_ref[...], rs2_ref[...]

    def rope(t):
        return t * rc + pltpu.roll(t, LANE - ROT_HALF, 1) * rs1 + pltpu.roll(t, ROT_HALF, 1) * rs2

    for j in range(3):
        q3_ref[j] = rope(tiles[j] * inv[j] * gq_ref[...]) * SCALE
        k3_ref[j] = rope(tiles[3 + j] * inv[3 + j] * gk_ref[...])
        v3_ref[j] = z[:, v0 + LANE * j:v0 + LANE * (j + 1)]
    for j in range(2):
        qm_ref[:, LANE * j:LANE * (j + 1)] = (tiles[6 + j] * inv[6 + j] * gqm_ref[...] * SCALE).astype(BF16)


def _premix_parts(x, tabs, seq, w, tm, gate_dtype=F32):
    t = x.shape[0]
    n_tab_tiles = seq // tm
    full = lambda shape: pl.BlockSpec(shape, lambda i: (0,) * len(shape))
    tab = pl.BlockSpec((tm, LANE), lambda i: (i % n_tab_tiles, 0))
    row = lambda width: pl.BlockSpec((tm, width), lambda i: (i, 0))
    pair = pl.BlockSpec((3, tm, LANE), lambda i: (0, i, 0))
    return dict(
        steps=t // tm,
        in_specs=[row(D_MODEL), full((1, D_MODEL)), full((D_MODEL, IN_WIDTH)), full((1, WIDTH_A)),
                  full((1, LANE)), full((1, LANE)), full((1, LANE)), full((2 * LANE, 2 * LANE)), tab, tab, tab],
        args=[x, w["g1"], w["w_in"], w["gva"], w["gq"], w["gk"], w["gqm"], w["bd2"], *tabs],
        out_specs=[row(WIDTH_A), row(WIDTH_A), pair, pair, pair, row(WIDTH_M)],
        out_shape=[jax.ShapeDtypeStruct((t, WIDTH_A), gate_dtype), jax.ShapeDtypeStruct((t, WIDTH_A), gate_dtype),
                   jax.ShapeDtypeStruct((3, t, LANE), F32), jax.ShapeDtypeStruct((3, t, LANE), F32),
                   jax.ShapeDtypeStruct((3, t, LANE), F32), jax.ShapeDtypeStruct((t, WIDTH_M), BF16)])


def _premix(x, tabs, seq, w):
    p = _premix_parts(x, tabs, seq, w, TM)
    return pl.pallas_call(
        _premix_body, grid=(p["steps"],), in_specs=p["in_specs"], out_specs=p["out_specs"],
        out_shape=p["out_shape"], compiler_params=_cparams(("arbitrary",)), name="premix",
    )(*p["args"])


def _premix_sample_body(*refs, n_in, n_out):
    ins_a, ins_b = refs[:n_in[0]], refs[n_in[0]:n_in[0] + n_in[1]]
    km_ref, vm_ref = refs[n_in[0] + n_in[1]:n_in[0] + n_in[1] + 2]
    outs = refs[n_in[0] + n_in[1] + 2:]
    _premix_body(*ins_a, *outs[:n_out])
    _sample_attn_body(*ins_b, *outs[n_out:n_out + 2])
    _memattn_body(outs[n_out - 1], km_ref, vm_ref, outs[n_out + 2])


def _premix_with_sample_attn(a, b, km, vm):
    assert a["steps"] == b["steps"]
    tokens = a["out_shape"][0].shape[0]
    tm = tokens // a["steps"]
    steps_per_b = a["steps"] // km.shape[0]
    kv = pl.BlockSpec((1, N_MEM, WIDTH_M), lambda i: (i // steps_per_b, 0, 0))
    outs = pl.pallas_call(
        functools.partial(_premix_sample_body, n_in=(len(a["args"]), len(b["args"])), n_out=len(a["out_shape"])),
        grid=(a["steps"],), in_specs=a["in_specs"] + b["in_specs"] + [kv, kv],
        out_specs=a["out_specs"] + b["out_specs"] + [pl.BlockSpec((tm, WIDTH_M), lambda i: (i, 0))],
        out_shape=a["out_shape"] + b["out_shape"] + [jax.ShapeDtypeStruct((tokens, WIDTH_M), BF16)],
        compiler_params=_cparams(("arbitrary",)),
        name="premix_sample_attn",
    )(*a["args"], *b["args"], km, vm)
    na = len(a["out_shape"])
    return outs[:na], outs[na:na + 2], outs[na + 2]


def _rope_tables(pos):
    pos = np.asarray(pos, np.float64)
    inv_freq = np.power(ROPE_THETA, -np.arange(ROT_HALF, dtype=np.float64) / ROT_HALF)
    ang = pos[:, None] * inv_freq[None, :]
    cos, sin = np.cos(ang), np.sin(ang)
    t = pos.shape[0]
    rest = HEAD_DIM - 2 * ROT_HALF
    c = np.concatenate([cos, cos, np.ones((t, rest))], axis=1)
    s1 = np.concatenate([-sin, np.zeros((t, HEAD_DIM - ROT_HALF))], axis=1)
    s2 = np.concatenate([np.zeros((t, ROT_HALF)), sin, np.zeros((t, rest))], axis=1)
    two = lambda a: jnp.asarray(np.concatenate([a, a], axis=1), F32)
    return two(c), two(s1), two(s2)


def _attn_body(q_ref, kc_ref, kp_ref, vc_ref, vp_ref, o_ref, *scratch):
    states = [scratch[i:i + 3] for i in range(0, len(scratch), 3)]
    span_idx = pl.program_id(1)
    p0 = span_idx * SPAN
    lane = lax.broadcasted_iota(jnp.int32, (QB, LANE), 1)
    low = lane < HEAD_DIM
    qi = lax.broadcasted_iota(jnp.int32, (QB, 2 * QB), 0)
    kj = lax.broadcasted_iota(jnp.int32, (QB, 2 * QB), 1)
    band = (kj >= qi) & (kj <= qi + N_SUB)

    def rows(ref, start, d):
        if d == 1:
            return ref[0, pl.ds(start, QB), :]
        return ref[0, pl.ds(start, QB, stride=d), :]

    def unit(d, res, blk, state):
        qstart = res + d * QB * blk
        if blk == 0:
            older = (kp_ref, vp_ref, SPAN - QB * d + res)
            first_key = jnp.maximum((QB * d - p0 - res + d - 1) // d, 0)
            mask = band & (kj >= first_key)
        else:
            older = (kc_ref, vc_ref, qstart - QB * d)
            mask = band
        qb = rows(q_ref, qstart, d)
        kb = jnp.concatenate([rows(older[0], older[2], d), rows(kc_ref, qstart, d)], axis=0).astype(BF16)
        vb = jnp.concatenate([rows(older[1], older[2], d), rows(vc_ref, qstart, d)], axis=0).astype(BF16)
        qh = jnp.concatenate([jnp.where(low, qb, 0.0), jnp.where(low, 0.0, qb)], axis=0).astype(BF16)
        s2 = lax.dot_general(qh, kb, (((1,), (1,)), ((), ())), preferred_element_type=F32)
        ms, ls, es = [], [], []
        for h in range(2):
            s = jnp.where(mask, s2[h * QB:(h + 1) * QB], NEG)
            m = jnp.max(s, axis=-1, keepdims=True)
            e = jnp.exp(s - m)
            ms.append(m)
            ls.append(jnp.sum(e, axis=-1, keepdims=True))
            es.append(e.astype(BF16))
        acc2 = jnp.dot(jnp.concatenate(es, axis=0), vb, preferred_element_type=F32)
        stats = [(ms[h], ls[h], acc2[h * QB:(h + 1) * QB]) for h in range(2)]
        m_new = jnp.where(low, stats[0][0], stats[1][0])
        l_new = jnp.where(low, stats[0][1], stats[1][1])
        a_new = jnp.where(low, stats[0][2], stats[1][2])
        if state is not None:
            sl = (pl.ds(qstart, QB, stride=d), slice(None))
            state[0][sl], state[1][sl], state[2][sl] = m_new, l_new, a_new
        else:
            sl = (pl.ds(qstart, QB), slice(None))
            parts = [(m_new, l_new, a_new)] + [(st[0][sl], st[1][sl], st[2][sl]) for st in states]
            m_t = functools.reduce(jnp.maximum, [p[0] for p in parts])
            ws = [jnp.exp(p[0] - m_t) for p in parts]
            l_t = sum(w * p[1] for w, p in zip(ws, parts))
            a_t = sum(w * p[2] for w, p in zip(ws, parts))
            o_ref[sl] = (a_t / l_t).astype(o_ref.dtype)

    nblk = SPAN // QB
    for (_, d), state in zip(reversed(DILATIONS), states + [None]):
        assert (d == 1) == (state is None)
        per_res = nblk // d
        for res in range(d):
            for blk in range(per_res):
                unit(d, res, blk, state)


def _attn_prompt(q3, k3, v3, batch, seq):
    nspan = seq // SPAN
    cur = pl.BlockSpec((1, SPAN, LANE), lambda b, s, p: (p, b * nspan + s, 0))
    prv = pl.BlockSpec((1, SPAN, LANE), lambda b, s, p: (p, b * nspan + jnp.maximum(s - 1, 0), 0))
    return pl.pallas_call(
        _attn_body,
        grid=(batch, nspan, 3),
        in_specs=[cur, cur, prv, cur, prv],
        out_specs=pl.BlockSpec((SPAN, LANE), lambda b, s, p: (b * nspan + s, p)),
        out_shape=jax.ShapeDtypeStruct((batch * seq, WIDTH_B), BF16),
        scratch_shapes=[pltpu.VMEM((SPAN, LANE), F32)] * (3 * (len(DILATIONS) - 1)),
        compiler_params=_cparams(("arbitrary", "arbitrary", "arbitrary")),
        name="attn_prompt",
    )(q3, k3, k3, v3, v3)


def _memkv_body(mem_ref, g_ref, w_ref, gk_ref, bd_ref, k_ref, v_ref):
    x = mem_ref[...]
    ms = jnp.mean(x * x, axis=-1, keepdims=True)
    h = (x * lax.rsqrt(ms + EPS) * g_ref[...]).astype(BF16)
    kv = jnp.dot(h, w_ref[...], preferred_element_type=F32)
    n = x.shape[0]
    kt = [kv[:, LANE * j:LANE * (j + 1)] for j in range(2)]
    sq = jnp.concatenate([(t * t).astype(BF16) for t in kt], axis=0)
    ssum = jnp.dot(sq, bd_ref[...], preferred_element_type=F32)
    for j in range(2):
        inv = lax.rsqrt(ssum[j * n:(j + 1) * n] * (1.0 / HEAD_DIM) + EPS)
        k_ref[:, LANE * j:LANE * (j + 1)] = kt[j] * inv * gk_ref[...]
    v_ref[...] = kv[:, WIDTH_M:]


def _memkv(mem, w):
    n = mem.shape[0]
    return pl.pallas_call(
        _memkv_body,
        out_shape=[jax.ShapeDtypeStruct((n, WIDTH_M), F32)] * 2,
        compiler_params=pltpu.CompilerParams(vmem_limit_bytes=VMEM_LIMIT),
        name="memkv",
    )(mem, w["gmem"], w["w_mem_kv"], w["gkm"], w["bd"])


def _memattn_body(q_ref, k_ref, v_ref, o_ref):
    lane = lax.broadcasted_iota(jnp.int32, (q_ref.shape[0], LANE), 1)
    low = lane < HEAD_DIM
    for j in range(2):
        qp = q_ref[:, LANE * j:LANE * (j + 1)].astype(F32)
        kp = k_ref[0, :, LANE * j:LANE * (j + 1)].astype(BF16)
        vp = v_ref[0, :, LANE * j:LANE * (j + 1)].astype(BF16)
        n = qp.shape[0]
        qh = jnp.concatenate([jnp.where(low, qp, 0.0), jnp.where(low, 0.0, qp)], axis=0).astype(BF16)
        s = lax.dot_general(qh, kp, (((1,), (1,)), ((), ())), preferred_element_type=F32)
        m = jnp.max(s, axis=-1, keepdims=True)
        e = jnp.exp(s - m)
        l = jnp.sum(e, axis=-1, keepdims=True)
        out = jnp.dot(e.astype(BF16), vp, preferred_element_type=F32) / l
        o_ref[:, LANE * j:LANE * (j + 1)] = jnp.where(low, out[:n], out[n:]).astype(o_ref.dtype)


def _sample_attn_body(*refs):
    for i in range(refs[0].shape[0]):
        _sample_attn_one(i, *refs)


def _sample_attn_one(i, qbd_ref, kt_ref, vt_ref, kn_ref, vn_ref, cnt_ref, cntn_ref, hmask_ref,
                     qmbd_ref, kmt_ref, vmt_ref, hmaskm_ref, ob_ref, om_ref):
    dec = kn_ref.shape[1]
    qbd = qbd_ref[i]
    kt = kt_ref[i].astype(BF16)
    vt = vt_ref[i].astype(BF16)
    s = jnp.dot(qbd, kt, preferred_element_type=F32)
    qf = qbd.astype(F32)
    kn = kn_ref[i]
    vn = vn_ref[i]
    cnt = cnt_ref[...]
    cntn = cntn_ref[...]
    s_new = [jnp.sum(qf * kn[j:j + 1, :], axis=-1, keepdims=True) for j in range(dec)]
    m = jnp.max(jnp.where(cnt > 0, s, NEG), axis=-1, keepdims=True)
    for j in range(dec):
        m = jnp.maximum(m, jnp.where(cntn[:, j:j + 1] > 0, s_new[j], NEG))
    e = cnt * jnp.exp(jnp.where(cnt > 0, s - m, 0.0))
    l = jnp.sum(e, axis=-1, keepdims=True)
    acc = lax.dot_general(e.astype(BF16), vt, (((1,), (1,)), ((), ())), preferred_element_type=F32)
    for j in range(dec):
        w = cntn[:, j:j + 1]
        ej = w * jnp.exp(jnp.where(w > 0, s_new[j] - m, 0.0))
        l = l + ej
        acc = acc + ej * vn[j:j + 1, :]
    r = acc / l * hmask_ref[...]
    out = r[0:8]
    for h in range(1, WIDTH_B // HEAD_DIM):
        out = out + r[8 * h:8 * h + 8]
    ob_ref[i] = out
    qm = qmbd_ref[i]
    sm = jnp.dot(qm, kmt_ref[i].astype(BF16), preferred_element_type=F32)
    mm = jnp.max(sm, axis=-1, keepdims=True)
    em = jnp.exp(sm - mm)
    lm = jnp.sum(em, axis=-1, keepdims=True)
    am = lax.dot_general(em.astype(BF16), vmt_ref[i].astype(BF16), (((1,), (1,)), ((), ())),
                         preferred_element_type=F32)
    rm = am / lm * hmaskm_ref[...]
    outm = rm[0:8]
    for h in range(1, WIDTH_M // HEAD_DIM):
        outm = outm + rm[8 * h:8 * h + 8]
    om_ref[i] = outm


def _sample_counts(dec, w_buf):
    t = np.arange(8)[:, None]
    t = np.where(t < dec, t, 0)
    def mult(dist):
        c = np.zeros(dist.shape, np.float32)
        for window, dil in DILATIONS:
            c += ((dist >= 0) & (dist % dil == 0) & (dist <= window)).astype(np.float32)
        return c
    cache = mult(w_buf + t - np.arange(w_buf)[None, :])
    new = mult(t - np.arange(dec)[None, :])
    nb, nm = WIDTH_B // HEAD_DIM, WIDTH_M // HEAD_DIM
    hmask = (np.arange(8 * nb)[:, None] // 8 == np.arange(WIDTH_B)[None, :] // HEAD_DIM).astype(np.float32)
    hmaskm = (np.arange(8 * nm)[:, None] // 8 == np.arange(WIDTH_M)[None, :] // HEAD_DIM).astype(np.float32)
    return np.tile(cache, (nb, 1)), np.tile(new, (nb, 1)), hmask, hmaskm


def _block_diag_queries(q, dec, hmask):
    width = q.shape[-1]
    nh = width // HEAD_DIM
    qb = q.reshape(-1, 1, dec, width)
    qb = jnp.pad(qb, ((0, 0), (0, 0), (0, 8 - dec), (0, 0)))
    qb = jnp.broadcast_to(qb, (qb.shape[0], nh, 8, width)).reshape(-1, 8 * nh, width)
    return (qb * hmask[None]).astype(BF16)


def _sample_attn_parts(q, kn, vn, qm, kt, vt, kmt, vmt, dec):
    bd = kt.shape[0]
    w_buf = kt.shape[-1]
    cnt, cntn, hmask, hmaskm = _sample_counts(dec, w_buf)
    qbd = _block_diag_queries(q, dec, hmask)
    qmbd = _block_diag_queries(qm.astype(F32), dec, hmaskm)
    nb8, nm8 = qbd.shape[1], qmbd.shape[1]
    per_b = lambda shape: pl.BlockSpec((SAMPLE_SEQS,) + shape, lambda b: (b,) + (0,) * len(shape))
    full = lambda shape: pl.BlockSpec(shape, lambda b: (0,) * len(shape))
    return dict(
        steps=bd // SAMPLE_SEQS,
        in_specs=[per_b((nb8, WIDTH_B)), per_b((WIDTH_B, w_buf)), per_b((WIDTH_B, w_buf)),
                  per_b((dec, WIDTH_B)), per_b((dec, WIDTH_B)),
                  full((nb8, w_buf)), full((nb8, dec)), full((nb8, WIDTH_B)),
                  per_b((nm8, WIDTH_M)), per_b((WIDTH_M, N_MEM)), per_b((WIDTH_M, N_MEM)), full((nm8, WIDTH_M))],
        args=[qbd, kt, vt, kn.reshape(bd, dec, WIDTH_B), vn.reshape(bd, dec, WIDTH_B),
              jnp.asarray(cnt), jnp.asarray(cntn), jnp.asarray(hmask), qmbd, kmt, vmt, jnp.asarray(hmaskm)],
        out_specs=[per_b((8, WIDTH_B)), per_b((8, WIDTH_M))],
        out_shape=[jax.ShapeDtypeStruct((bd, 8, WIDTH_B), F32), jax.ShapeDtypeStruct((bd, 8, WIDTH_M), F32)])


def _sample_attn_rows(ob, om, dec):
    return (ob[:, :dec].reshape(-1, WIDTH_B).astype(BF16), om[:, :dec].reshape(-1, WIDTH_M).astype(BF16))


def _post_body(*refs, tc, aliased):
    (x_ref, u_ref, va_ref, ob_ref, om_ref, wg_ref, bg_ref, wout_ref, g2_ref, wr_ref, br_ref,
     tri_ref, upper_ref) = refs[:13]
    x1_ref, xs_ref, ld_ref, tg_ref, seg_ref = refs[13 + aliased:]
    tm = x_ref.shape[0]
    u = u_ref[...].astype(F32)
    if tc is None:
        vaf = va_ref[...]
        mixed = wg_ref[0] * vaf + bg_ref[...]
        for s in range(1, wg_ref.shape[0]):
            mixed = mixed + wg_ref[s] * pltpu.roll(vaf, s, 0)
        oa = u * mixed
    else:
        lane = lax.broadcasted_iota(jnp.int32, (tc, LANE), 1)
        low = lane < HEAD_DIM
        va = va_ref[...].astype(BF16)
        oa_rows = []
        for c in range(tm // tc):
            r0 = c * tc
            tiles = []
            for p in range(3):
                vp = va[r0:r0 + tc, LANE * p:LANE * (p + 1)]
                r = jnp.dot(wg_ref[p], vp, preferred_element_type=F32)
                tiles.append(jnp.where(low, r[:tc], r[tc:]))
            mixed = jnp.concatenate(tiles, axis=1) + bg_ref[...]
            oa_rows.append(u[r0:r0 + tc] * mixed)
        oa = jnp.concatenate(oa_rows, axis=0)
    mixed_all = jnp.concatenate([oa.astype(BF16), ob_ref[...], om_ref[...]], axis=1)
    x1 = x_ref[...] + jnp.dot(mixed_all, wout_ref[...], preferred_element_type=F32)
    x1_ref[...] = x1
    ms = jnp.mean(x1 * x1, axis=-1, keepdims=True)
    h2 = x1 * lax.rsqrt(ms + EPS) * g2_ref[...]
    h_hi = h2.astype(BF16)
    h_lo = (h2 - h_hi.astype(F32)).astype(BF16)
    hw = jnp.dot(h_hi, wr_ref[...], preferred_element_type=F32)
    logits = (hw[:, :LANE] + hw[:, LANE:]
              + jnp.dot(h_lo, wr_ref[:, :LANE], preferred_element_type=F32)) + br_ref[...]
    lane_i = lax.broadcasted_iota(jnp.int32, (tm, LANE), 1)
    lane_r = lane_i.astype(F32)
    vals = logits
    tops, idxs = [], []
    for _ in range(TOP_K):
        mk = jnp.max(vals, axis=-1, keepdims=True)
        ik = jnp.min(jnp.where(vals == mk, lane_r, float(LANE)), axis=-1, keepdims=True)
        vals = jnp.where(lane_r == ik, -jnp.inf, vals)
        tops.append(mk)
        idxs.append(ik)
    es = [jnp.exp(t - tops[0]) for t in tops]
    den = es[0] + es[1] + es[2] + es[3]
    tg = jnp.zeros((tm, LANE), F32)
    for k in range(TOP_K):
        tg = jnp.where(lane_i == k, es[k] / den, tg)
    tg_ref[...] = tg
    col = lax.broadcasted_iota(jnp.int32, (MOE_TM, SEG_ROWS), 1).astype(F32)
    lane_t = lax.broadcasted_iota(jnp.int32, (MOE_TM, LANE), 1)
    hot_all = [lane_r == idxs[k] for k in range(TOP_K)]
    for hf in range(tm // MOE_TM):
        r0 = hf * MOE_TM
        hot = [h[r0:r0 + MOE_TM] for h in hot_all]
        sel = jnp.zeros((MOE_TM, LANE), F32)
        for k in range(TOP_K):
            sel = sel + jnp.where(hot[k], 1.0, 0.0)
        rank = jnp.dot(tri_ref[...], sel.astype(BF16), preferred_element_type=F32)
        length = jnp.sum(sel, axis=0, keepdims=True)
        plen = jnp.floor((length + (PIECE - 1)) * (1.0 / PIECE)) * PIECE
        loff = jnp.dot(jnp.broadcast_to(plen, (8, LANE)).astype(BF16), upper_ref[...],
                       preferred_element_type=F32)[0:1]
        base = loff + rank
        hit = None
        ld = jnp.zeros((MOE_TM, LANE), F32)
        for k in range(TOP_K):
            ld_k = jnp.sum(jnp.where(hot[k], base, 0.0), axis=-1, keepdims=True)
            hit = (col == ld_k) if hit is None else hit | (col == ld_k)
            ld = jnp.where(lane_t == k, ld_k, ld)
        q = jnp.where(hit, 1.0, 0.0)
        xs_ref[hf * SEG_ROWS:(hf + 1) * SEG_ROWS, :] = lax.dot_general(
            q.astype(BF16), h_hi[r0:r0 + MOE_TM], (((0,), (0,)), ((), ())),
            preferred_element_type=F32).astype(xs_ref.dtype)
        ld_ref[r0:r0 + MOE_TM, :] = ld.astype(jnp.int32)
        seg_ref[hf * 8:(hf + 1) * 8, :] = jnp.broadcast_to(length, (8, LANE)).astype(jnp.int32)


def _post(x, u, va, ob, om, wg, bg, w, tc, total_tiles, tile0=0, xs_all=None):
    t = x.shape[0]
    nt = t // TM
    full = lambda shape: pl.BlockSpec(shape, lambda i: (0,) * len(shape))
    row = lambda width: pl.BlockSpec((TM, width), lambda i: (i, 0))
    per = TM // MOE_TM
    ix = np.arange(MOE_TM)
    tri = jnp.asarray(ix[:, None] > ix[None, :], BF16)
    ex = np.arange(LANE)
    upper = jnp.asarray(ex[:, None] < ex[None, :], BF16)
    step0 = tile0 // per
    in_specs = [row(D_MODEL), row(WIDTH_A), row(WIDTH_A), row(WIDTH_B), row(WIDTH_M),
                full(wg.shape), full(bg.shape), full((D_MODEL, D_MODEL)), full((1, D_MODEL)),
                full((D_MODEL, 2 * LANE)), full((1, LANE)), full((MOE_TM, MOE_TM)), full((LANE, LANE))]
    args = [x, u, va, ob, om, wg, bg, w["w_out"], w["g2"], w["w_router"], w["b_router"], tri, upper]
    aliases = {}
    if xs_all is not None:
        in_specs.append(pl.BlockSpec(memory_space=pl.ANY))
        args.append(xs_all)
        aliases = {len(args) - 1: 1}
    return pl.pallas_call(
        functools.partial(_post_body, tc=tc, aliased=int(xs_all is not None)),
        grid=(nt,),
        in_specs=in_specs,
        out_specs=[row(D_MODEL), pl.BlockSpec((per * SEG_ROWS, D_MODEL), lambda i: (i + step0, 0)),
                   row(LANE), row(LANE), pl.BlockSpec((per * 8, LANE), lambda i: (i, 0))],
        out_shape=[jax.ShapeDtypeStruct((t, D_MODEL), F32),
                   jax.ShapeDtypeStruct((total_tiles * SEG_ROWS, D_MODEL), BF16),
                   jax.ShapeDtypeStruct((t, LANE), jnp.int32), jax.ShapeDtypeStruct((t, LANE), F32),
                   jax.ShapeDtypeStruct((t // MOE_TM * 8, LANE), jnp.int32)],
        input_output_aliases=aliases,
        compiler_params=_cparams(("arbitrary",)),
        name="post",
    )(*args)


def _experts_body(te_ref, valid_ref, first_ref, next_ref, half_ref, src_ref, dst_ref,
                  xs_hbm, wgu_hbm, bgu_ref, wd_hbm, bd_ref, ys_hbm,
                  xbuf, ybuf, wgu_f, wd_f, wgu_s, wd_s, gsem, ssem, wsem):
    t = pl.program_id(0)
    nt = pl.num_programs(0)
    slot = t % 2

    def gather(tile, sl):
        for i in range(NPIECE):
            s = src_ref[tile * NPIECE + i]
            pltpu.make_async_copy(xs_hbm.at[pl.ds(pl.multiple_of(s * PIECE, PIECE), PIECE), :],
                                  xbuf.at[sl, pl.ds(i * PIECE, PIECE), :], gsem.at[sl]).start()

    def scatter(tile, sl):
        for i in range(NPIECE):
            d = dst_ref[tile * NPIECE + i]
            pltpu.make_async_copy(ybuf.at[sl, pl.ds(i * PIECE, PIECE), :],
                                  ys_hbm.at[pl.ds(pl.multiple_of(d * PIECE, PIECE), PIECE), :], ssem.at[sl]).start()

    def wait_tile(hbm, buf, sem, sl):
        pltpu.make_async_copy(hbm.at[pl.ds(0, GM_TM), :], buf.at[sl], sem.at[sl]).wait()

    def weight_copies(e):
        return (pltpu.make_async_copy(wgu_hbm.at[e], wgu_f, wsem.at[0]),
                pltpu.make_async_copy(wd_hbm.at[e], wd_f, wsem.at[1]))

    @pl.when(t == 0)
    def _():
        for c in weight_copies(te_ref[0]):
            c.start()
        gather(0, 0)
        ybuf[...] = jnp.zeros(ybuf.shape, ybuf.dtype)

    @pl.when(valid_ref[t] > 0)
    def _():
        nxt = jnp.minimum(t + 1, nt - 1)
        has_next = jnp.logical_and(t + 1 < nt, valid_ref[nxt] > 0)

        @pl.when(first_ref[t] > 0)
        def _():
            for c in weight_copies(te_ref[t]):
                c.wait()
            wgu_s[...] = wgu_f[...].astype(BF16)
            wd_s[...] = wd_f[...].astype(BF16)

            @pl.when(next_ref[t] >= 0)
            def _():
                for c in weight_copies(next_ref[t]):
                    c.start()

        @pl.when(t >= 2)
        def _():
            wait_tile(ys_hbm, ybuf, ssem, slot)

        gather(jnp.where(has_next, t + 1, t), 1 - slot)
        wait_tile(xs_hbm, xbuf, gsem, slot)

        def ffn(rows):
            x = xbuf[slot, pl.ds(0, rows), :]
            gu = jnp.dot(x, wgu_s[...], preferred_element_type=F32) + bgu_ref[0]
            gate = jnp.minimum(gu[:, :D_MODEL], SWIGLU_LIMIT)
            up = jnp.clip(gu[:, D_MODEL:], -SWIGLU_LIMIT, SWIGLU_LIMIT)
            act = (up + 1.0) * (gate * (1.0 / (1.0 + jnp.exp(-SWIGLU_ALPHA * gate))))
            y = jnp.dot(act.astype(BF16), wd_s[...], preferred_element_type=F32) + bd_ref[0]
            ybuf[slot, pl.ds(0, rows), :] = y.astype(ybuf.dtype)

        @pl.when(half_ref[t] > 0)
        def _():
            ffn(GM_TM // 2)

        @pl.when(half_ref[t] == 0)
        def _():
            ffn(GM_TM)

        scatter(t, slot)

        @pl.when(jnp.logical_not(has_next))
        def _():
            wait_tile(xs_hbm, xbuf, gsem, 1 - slot)
            wait_tile(ys_hbm, ybuf, ssem, slot)

            @pl.when(t >= 1)
            def _():
                wait_tile(ys_hbm, ybuf, ssem, 1 - slot)


def _experts(plan, xs, w, ys_rows):
    n_tiles = plan["tile_expert"].shape[0]
    by_expert = lambda shape: pl.BlockSpec((1,) + shape, lambda t, te, *_: (te[t],) + (0,) * len(shape))
    hbm = pl.BlockSpec(memory_space=pl.ANY)
    return pl.pallas_call(
        _experts_body,
        grid_spec=pltpu.PrefetchScalarGridSpec(
            num_scalar_prefetch=7, grid=(n_tiles,),
            in_specs=[hbm, hbm, by_expert((1, 2 * D_MODEL)), hbm, by_expert((1, D_MODEL))],
            out_specs=hbm,
            scratch_shapes=[pltpu.VMEM((2, GM_TM, D_MODEL), BF16), pltpu.VMEM((2, GM_TM, D_MODEL), BF16),
                            pltpu.VMEM((D_MODEL, 2 * D_MODEL), F32), pltpu.VMEM((D_MODEL, D_MODEL), F32),
                            pltpu.VMEM((D_MODEL, 2 * D_MODEL), BF16), pltpu.VMEM((D_MODEL, D_MODEL), BF16),
                            pltpu.SemaphoreType.DMA((2,)), pltpu.SemaphoreType.DMA((2,)),
                            pltpu.SemaphoreType.DMA((2,))]),
        out_shape=jax.ShapeDtypeStruct((ys_rows, D_MODEL), BF16),
        compiler_params=_cparams(("arbitrary",)),
        name="moe_experts",
    )(plan["tile_expert"], plan["tile_valid"], plan["tile_first"], plan["tile_next"], plan["tile_half"],
      plan["src"], plan["dst"],
      xs, w["w_gate_up"], w["b_gate_up"], w["w_down"], w["b_down"])


def _combine_body(used_ref, ld_ref, g_ref, x1_ref, ys_ref, y_ref, *, tile0):
    t = pl.program_id(0)
    row = lax.broadcasted_iota(jnp.int32, (SEG_ROWS, 1), 0)
    ys = jnp.where(row < used_ref[tile0 + t], ys_ref[...].astype(F32), 0.0).astype(BF16)
    col = lax.broadcasted_iota(jnp.int32, (MOE_TM, SEG_ROWS), 1)
    ld = ld_ref[...]
    g = g_ref[...]
    p = jnp.zeros((MOE_TM, SEG_ROWS), F32)
    for k in range(TOP_K):
        p = jnp.where(col == ld[:, k:k + 1], g[:, k:k + 1], p)
    y_ref[...] = x1_ref[...] + jnp.dot(p.astype(BF16), ys, preferred_element_type=F32)


def _combine(used, ld, gates, x1, ys, tile0):
    n = x1.shape[0]
    nt = n // MOE_TM
    rows = lambda width: pl.BlockSpec((MOE_TM, width), lambda t, *_: (t, 0))
    return pl.pallas_call(
        functools.partial(_combine_body, tile0=tile0),
        grid_spec=pltpu.PrefetchScalarGridSpec(
            num_scalar_prefetch=1, grid=(nt,),
            in_specs=[rows(LANE), rows(LANE), rows(D_MODEL),
                      pl.BlockSpec((SEG_ROWS, D_MODEL), lambda t, *_: (t + tile0, 0))],
            out_specs=rows(D_MODEL)),
        out_shape=jax.ShapeDtypeStruct((n, D_MODEL), F32),
        compiler_params=_cparams(("arbitrary",)),
        name="moe_combine",
    )(used, ld, gates, x1, ys)


def _moe(groups, xs, seglen, w):
    i32 = jnp.int32
    nt = seglen.shape[0]
    plen = (seglen + PIECE - 1) // PIECE * PIECE
    loff = jnp.cumsum(plen, axis=1) - plen
    used = jnp.sum(plen, axis=1).astype(i32)
    pp = plen // PIECE
    cp_end = jnp.cumsum(pp, axis=0)
    cp = cp_end - pp
    cnt_e = cp_end[-1]
    tiles_e = (cnt_e + NPIECE - 1) // NPIECE
    tile_end = jnp.cumsum(tiles_e)
    tile_start = tile_end - tiles_e
    n_tiles = (nt * SEG_ROWS + GM_TM - 1) // GM_TM + N_EXPERTS
    tix = jnp.arange(n_tiles, dtype=i32)
    total_tiles = tile_end[-1]
    tile_valid = (tix < total_tiles).astype(i32)
    expert_at = lambda tile: jnp.minimum((tile[:, None] >= tile_end[None, :]).astype(i32).sum(axis=1), N_EXPERTS - 1)
    last_expert = expert_at(jnp.maximum(total_tiles - 1, 0)[None])[0]
    tile_expert = jnp.where(tile_valid > 0, expert_at(tix), last_expert)
    hot_e = tile_expert[:, None] == jnp.arange(N_EXPERTS, dtype=i32)[None, :]
    per_tile = lambda v: jnp.sum(jnp.where(hot_e, v[None, :], 0), axis=1)
    per_tile_rows = lambda m: jnp.sum(jnp.where(hot_e[:, :, None], jnp.transpose(m)[None], 0), axis=1)
    start_t = per_tile(tile_start)
    tile_first = ((tix == start_t) & (tile_valid > 0)).astype(i32)
    following = per_tile(tile_end)
    tile_next = jnp.where(following < total_tiles, expert_at(following), -1)
    j = (tix - start_t)[:, None] * NPIECE + jnp.arange(NPIECE, dtype=i32)[None, :]
    cnt_t = per_tile(cnt_e)
    ok = (tile_valid[:, None] > 0) & (j < cnt_t[:, None])
    tile_half = ((tile_valid > 0) & (cnt_t - (tix - start_t) * NPIECE <= NPIECE // 2)).astype(i32)
    ends_t, cp_t, loff_t = per_tile_rows(cp_end), per_tile_rows(cp), per_tile_rows(loff)
    t_q = jnp.minimum((ends_t[:, None, :] <= j[:, :, None]).astype(i32).sum(axis=2), nt - 1)
    hot_t = t_q[:, :, None] == jnp.arange(nt, dtype=i32)[None, None, :]
    at_t = lambda m: jnp.sum(jnp.where(hot_t, m[:, None, :], 0), axis=2)
    piece = t_q * SEG_PIECES + at_t(loff_t) // PIECE + j - at_t(cp_t)
    src = jnp.where(ok, piece, 0)
    dump = nt * SEG_PIECES + (tix % 2)[:, None] * NPIECE + jnp.arange(NPIECE, dtype=i32)[None, :]
    dst = jnp.where(ok, piece, dump)
    plan = dict(tile_expert=tile_expert.astype(i32), tile_valid=tile_valid, tile_first=tile_first,
                tile_next=tile_next.astype(i32), tile_half=tile_half,
                src=src.reshape(-1).astype(i32), dst=dst.reshape(-1).astype(i32))

    ys = _experts(plan, xs, w, nt * SEG_ROWS + 2 * GM_TM)
    outs, r0 = [], 0
    for x1, ld, tg in groups:
        outs.append(_combine(used, ld, tg, x1, ys, r0 // MOE_TM))
        r0 += x1.shape[0]
    return outs


def _pair_major_to_rows(a3):
    return jnp.transpose(a3, (1, 0, 2)).reshape(a3.shape[1], 3 * LANE)


def kernel(x_prompt, x_sample, mem_prompt, cache_win_k, cache_win_v, cache_mem_k, cache_mem_v, norm1_g, w_in, gv_a, w_s, b_s, gq_b, gk_b, gq_m, gk_m, mem_norm_g, w_mem_kv, w_out, norm2_g, w_router, b_router, w_gate_up, b_gate_up, w_down, b_down):
    batch, seq, _ = x_prompt.shape
    bd, dec, _ = x_sample.shape
    depth = norm1_g.shape[0]
    assert depth == 1 and seq % SPAN == 0 and (bd * dec) % TM == 0 and PAST_LEN % CHUNK == 0
    w_buf = cache_win_k.shape[2]
    assert w_buf == MAX_WINDOW and dec <= 8
    l = 0
    two = lambda g: jnp.concatenate([g, g])[None, :]
    head = np.arange(LANE) // HEAD_DIM
    head2 = np.arange(2 * LANE) // HEAD_DIM
    wr = jnp.pad(w_router[l], ((0, 0), (0, LANE - N_EXPERTS)))
    wr_hi = wr.astype(BF16)
    wr_lo = (wr - wr_hi.astype(F32)).astype(BF16)
    w = dict(
        g1=norm1_g[l][None], w_in=w_in[l].astype(BF16), gva=gv_a[l][None],
        gq=two(gq_b[l]), gk=two(gk_b[l]), gqm=two(gq_m[l]), gkm=two(gk_m[l]),
        bd=jnp.asarray(head[:, None] == head[None, :], BF16),
        bd2=jnp.asarray(head2[:, None] == head2[None, :], BF16),
        gmem=mem_norm_g[l][None], w_mem_kv=w_mem_kv[l].astype(BF16),
        w_out=w_out[l].astype(BF16), g2=norm2_g[l][None],
        w_router=jnp.concatenate([wr_hi, wr_lo], axis=1),
        b_router=jnp.pad(b_router[l], (0, LANE - N_EXPERTS), constant_values=-jnp.inf)[None],
        w_gate_up=w_gate_up[l], b_gate_up=b_gate_up[l][:, None, :], w_down=w_down[l], b_down=b_down[l][:, None, :],
    )
    ngrp = WIDTH_A // HEAD_DIM
    wtri = jnp.where(jnp.tril(jnp.ones((CHUNK, CHUNK), bool)), w_s[l], 0).astype(BF16)
    wg_p = wtri.reshape(ngrp // 2, 2 * CHUNK, CHUNK)
    bg_p = jnp.repeat(jnp.transpose(b_s[l]), HEAD_DIM, axis=1)
    zero = jnp.zeros((ngrp,), F32)
    lanes = lambda tg_: jnp.tile(jnp.repeat(tg_, HEAD_DIM, axis=1), (bd, 1))
    wg_s = jnp.stack([lanes(jnp.stack([w_s[l][:, t, t - s] if t >= s else zero for t in range(dec)]))
                      for s in range(dec)])
    bg_s = lanes(jnp.transpose(b_s[l][:, :dec]))

    xs = x_sample.reshape(bd * dec, D_MODEL)
    tabs_s = _rope_tables(np.tile(PAST_LEN + np.arange(dec), bd))
    u_s, va_s, q3_s, k3_s, v3_s, qm_s = _premix(xs, tabs_s, bd * dec, w)
    q_s, k_s, v_s = (_pair_major_to_rows(a) for a in (q3_s, k3_s, v3_s))
    nb = WIDTH_B // HEAD_DIM
    nm = WIDTH_M // HEAD_DIM
    kt = jnp.transpose(cache_win_k[l], (0, 2, 3, 1)).reshape(bd, WIDTH_B, w_buf)
    vt = jnp.transpose(cache_win_v[l], (0, 2, 3, 1)).reshape(bd, WIDTH_B, w_buf)
    kmt = jnp.transpose(cache_mem_k[l], (0, 2, 3, 1)).reshape(bd, WIDTH_M, N_MEM)
    vmt = jnp.transpose(cache_mem_v[l], (0, 2, 3, 1)).reshape(bd, WIDTH_M, N_MEM)
    sample_parts = _sample_attn_parts(q_s, k_s, v_s, qm_s, kt, vt, kmt, vmt, dec)

    xp = x_prompt.reshape(batch * seq, D_MODEL)
    tabs_p = _rope_tables(np.arange(seq))
    premix_tm = batch * seq // sample_parts["steps"]
    assert seq % premix_tm == 0 and premix_tm % 8 == 0
    km, vm = _memkv(mem_prompt.reshape(batch * N_MEM, D_MODEL), w)
    (u_p, va_p, q3_p, k3_p, v3_p, _), (ob_s8, om_s8), om_p = _premix_with_sample_attn(
        _premix_parts(xp, tabs_p, seq, w, premix_tm, gate_dtype=BF16), sample_parts,
        km.reshape(batch, N_MEM, WIDTH_M), vm.reshape(batch, N_MEM, WIDTH_M))
    ob_s, om_s = _sample_attn_rows(ob_s8, om_s8, dec)
    ob_p = _attn_prompt(q3_p, k3_p, v3_p, batch, seq)
    tiles_p = batch * seq // MOE_TM
    tiles_all = tiles_p + bd * dec // MOE_TM
    x1_p, xs_all, ld_p, tg_p, seg_p = _post(xp, u_p, va_p, ob_p, om_p, wg_p, bg_p, w, CHUNK, tiles_all)

    x1_s, xs_all, ld_s, tg_s, seg_s = _post(xs, u_s, va_s, ob_s, om_s, wg_s, bg_s, w, None, tiles_all,
                                             tile0=tiles_p, xs_all=xs_all)

    seglen = jnp.concatenate([seg_p, seg_s])[::8, :N_EXPERTS]
    y_p, y_s = _moe([(x1_p, ld_p, tg_p), (x1_s, ld_s, tg_s)], xs_all, seglen, w)
    y_prompt = y_p.reshape(batch, seq, D_MODEL)
    y_sample = y_s.reshape(bd, dec, D_MODEL)

    n_keep = min(MAX_WINDOW, seq)

    def window_rows(a3):
        a = a3.reshape(3, batch, seq, LANE)[:, :, seq - n_keep:]
        return jnp.transpose(a, (1, 2, 0, 3)).reshape(1, batch, n_keep, nb, HEAD_DIM)

    return (y_prompt, y_sample,
            window_rows(k3_p), window_rows(v3_p),
            km.reshape(1, batch, N_MEM, nm, HEAD_DIM), vm.reshape(1, batch, N_MEM, nm, HEAD_DIM),
            k_s.reshape(1, bd, dec, nb, HEAD_DIM), v_s.reshape(1, bd, dec, nb, HEAD_DIM),
            va_s.reshape(1, bd, dec, WIDTH_A))
```

```python
import functools

import numpy as np
import jax
import jax.numpy as jnp
from jax import lax
from jax.experimental import pallas as pl
from jax.experimental.pallas import tpu as pltpu

F32 = jnp.float32
BF16 = jnp.bfloat16

D_MODEL = 1024
HEAD_DIM = 64
WIDTH_A = 384
WIDTH_B = 384
WIDTH_M = 256
IN_WIDTH = 2 * WIDTH_A + 3 * WIDTH_B + WIDTH_M
CHUNK = 128
DILATIONS = ((128, 1), (512, 4), (2048, 16))
N_SUB = 128
MAX_WINDOW = 2048
N_MEM = 256
ROPE_THETA = 500000.0
ROT_HALF = 8
SCALE = HEAD_DIM ** -0.5
N_EXPERTS = 32
TOP_K = 4
SWIGLU_LIMIT = 7.0
SWIGLU_ALPHA = 1.702
EPS = 1e-6
PAST_LEN = 8192

LANE = 128
NEG = -1e30
TM = 512
SPAN = 2048
QB = 128
SAMPLE_SEQS = 2
COMBINE_TILES = 2
MOE_TM = 256
GM_TM = 512
PIECE = 8
NPIECE = GM_TM // PIECE
SEG_ROWS = TOP_K * MOE_TM + N_EXPERTS * PIECE
SEG_PIECES = SEG_ROWS // PIECE
VMEM_LIMIT = 52 * 1024 * 1024


def _cparams(sem):
    return pltpu.CompilerParams(dimension_semantics=sem, vmem_limit_bytes=VMEM_LIMIT)


def _premix_body(x_ref, g1_ref, win_ref, gva_ref, gq_ref, gk_ref, gqm_ref, bd_ref,
                 rc_ref, rs1_ref, rs2_ref,
                 u_ref, va_ref, q3_ref, k3_ref, v3_ref, qm_ref):
    x = x_ref[...]
    tm = x.shape[0]
    ms = jnp.mean(x * x, axis=-1, keepdims=True)
    h = (x * lax.rsqrt(ms + EPS) * g1_ref[...]).astype(BF16)
    z = jnp.dot(h, win_ref[...], preferred_element_type=F32)
    u_ref[...] = z[:, :WIDTH_A].astype(u_ref.dtype)
    va = z[:, WIDTH_A:2 * WIDTH_A]
    va_ms = jnp.mean(va * va, axis=-1, keepdims=True)
    va_ref[...] = (va * lax.rsqrt(va_ms + EPS) * gva_ref[...]).astype(va_ref.dtype)
    q0, k0, v0, m0 = 2 * WIDTH_A, 2 * WIDTH_A + WIDTH_B, 2 * WIDTH_A + 2 * WIDTH_B, 2 * WIDTH_A + 3 * WIDTH_B
    tiles = ([z[:, q0 + LANE * j:q0 + LANE * (j + 1)] for j in range(3)]
             + [z[:, k0 + LANE * j:k0 + LANE * (j + 1)] for j in range(3)]
             + [z[:, m0 + LANE * j:m0 + LANE * (j + 1)] for j in range(2)])
    sq = [(t * t).astype(BF16) for t in tiles]
    sq = jnp.concatenate([jnp.concatenate(sq[i:i + 2], axis=1) for i in range(0, 8, 2)], axis=0)
    ssum = jnp.dot(sq, bd_ref[...], preferred_element_type=F32)
    inv = [lax.rsqrt(ssum[(i // 2) * tm:(i // 2 + 1) * tm, (i % 2) * LANE:(i % 2 + 1) * LANE] * (1.0 / HEAD_DIM) + EPS)
           for i in range(8)]
    rc, rs1, rs2 = rc_ref[...], rs1_ref[...], rs2_ref[...]

    def rope(t):
        return t * rc + pltpu.roll(t, LANE - ROT_HALF, 1) * rs1 + pltpu.roll(t, ROT_HALF, 1) * rs2

    for j in range(3):
        q3_ref[j] = rope(tiles[j] * inv[j] * gq_ref[...]) * SCALE
        k3_ref[j] = rope(tiles[3 + j] * inv[3 + j] * gk_ref[...])
        v3_ref[j] = z[:, v0 + LANE * j:v0 + LANE * (j + 1)]
    for j in range(2):
        qm_ref[:, LANE * j:LANE * (j + 1)] = (tiles[6 + j] * inv[6 + j] * gqm_ref[...] * SCALE).astype(BF16)


def _premix_parts(x, tabs, seq, w, tm, gate_dtype=F32):
    t = x.shape[0]
    n_tab_tiles = seq // tm
    full = lambda shape: pl.BlockSpec(shape, lambda i: (0,) * len(shape))
    tab = pl.BlockSpec((tm, LANE), lambda i: (i % n_tab_tiles, 0))
    row = lambda width: pl.BlockSpec((tm, width), lambda i: (i, 0))
    pair = pl.BlockSpec((3, tm, LANE), lambda i: (0, i, 0))
    return dict(
        steps=t // tm,
        in_specs=[row(D_MODEL), full((1, D_MODEL)), full((D_MODEL, IN_WIDTH)), full((1, WIDTH_A)),
                  full((1, LANE)), full((1, LANE)), full((1, LANE)), full((2 * LANE, 2 * LANE)), tab, tab, tab],
        args=[x, w["g1"], w["w_in"], w["gva"], w["gq"], w["gk"], w["gqm"], w["bd2"], *tabs],
        out_specs=[row(WIDTH_A), row(WIDTH_A), pair, pair, pair, row(WIDTH_M)],
        out_shape=[jax.ShapeDtypeStruct((t, WIDTH_A), gate_dtype), jax.ShapeDtypeStruct((t, WIDTH_A), gate_dtype),
                   jax.ShapeDtypeStruct((3, t, LANE), F32), jax.ShapeDtypeStruct((3, t, LANE), F32),
                   jax.ShapeDtypeStruct((3, t, LANE), F32), jax.ShapeDtypeStruct((t, WIDTH_M), BF16)])


def _premix(x, tabs, seq, w):
    p = _premix_parts(x, tabs, seq, w, TM)
    return pl.pallas_call(
        _premix_body, grid=(p["steps"],), in_specs=p["in_specs"], out_specs=p["out_specs"],
        out_shape=p["out_shape"], compiler_params=_cparams(("arbitrary",)), name="premix",
    )(*p["args"])


def _premix_sample_body(*refs, n_in, n_out):
    ins_a, ins_b = refs[:n_in[0]], refs[n_in[0]:n_in[0] + n_in[1]]
    km_ref, vm_ref = refs[n_in[0] + n_in[1]:n_in[0] + n_in[1] + 2]
    outs = refs[n_in[0] + n_in[1] + 2:]
    _premix_body(*ins_a, *outs[:n_out])
    _sample_attn_body(*ins_b, *outs[n_out:n_out + 2])
    _memattn_body(outs[n_out - 1], km_ref, vm_ref, outs[n_out + 2])


def _premix_with_sample_attn(a, b, km, vm):
    assert a["steps"] == b["steps"]
    tokens = a["out_shape"][0].shape[0]
    tm = tokens // a["steps"]
    steps_per_b = a["steps"] // km.shape[0]
    kv = pl.BlockSpec((1, N_MEM, WIDTH_M), lambda i: (i // steps_per_b, 0, 0))
    outs = pl.pallas_call(
        functools.partial(_premix_sample_body, n_in=(len(a["args"]), len(b["args"])), n_out=len(a["out_shape"])),
        grid=(a["steps"],), in_specs=a["in_specs"] + b["in_specs"] + [kv, kv],
        out_specs=a["out_specs"] + b["out_specs"] + [pl.BlockSpec((tm, WIDTH_M), lambda i: (i, 0))],
        out_shape=a["out_shape"] + b["out_shape"] + [jax.ShapeDtypeStruct((tokens, WIDTH_M), BF16)],
        compiler_params=_cparams(("arbitrary",)),
        name="premix_sample_attn",
    )(*a["args"], *b["args"], km, vm)
    na = len(a["out_shape"])
    return outs[:na], outs[na:na + 2], outs[na + 2]


def _rope_tables(pos):
    pos = np.asarray(pos, np.float64)
    inv_freq = np.power(ROPE_THETA, -np.arange(ROT_HALF, dtype=np.float64) / ROT_HALF)
    ang = pos[:, None] * inv_freq[None, :]
    cos, sin = np.cos(ang), np.sin(ang)
    t = pos.shape[0]
    rest = HEAD_DIM - 2 * ROT_HALF
    c = np.concatenate([cos, cos, np.ones((t, rest))], axis=1)
    s1 = np.concatenate([-sin, np.zeros((t, HEAD_DIM - ROT_HALF))], axis=1)
    s2 = np.concatenate([np.zeros((t, ROT_HALF)), sin, np.zeros((t, rest))], axis=1)
    two = lambda a: jnp.asarray(np.concatenate([a, a], axis=1), F32)
    return two(c), two(s1), two(s2)


def _attn_body(q_ref, kc_ref, kp_ref, vc_ref, vp_ref, o_ref, *scratch):
    states = [scratch[i:i + 3] for i in range(0, len(scratch), 3)]
    span_idx = pl.program_id(1)
    p0 = span_idx * SPAN
    lane = lax.broadcasted_iota(jnp.int32, (QB, LANE), 1)
    low = lane < HEAD_DIM
    qi = lax.broadcasted_iota(jnp.int32, (QB, 2 * QB), 0)
    kj = lax.broadcasted_iota(jnp.int32, (QB, 2 * QB), 1)
    band = (kj >= qi) & (kj <= qi + N_SUB)

    def rows(ref, start, d):
        if d == 1:
            return ref[0, pl.ds(start, QB), :]
        return ref[0, pl.ds(start, QB, stride=d), :]

    def unit(d, res, blk, state):
        qstart = res + d * QB * blk
        if blk == 0:
            older = (kp_ref, vp_ref, SPAN - QB * d + res)
            first_key = jnp.maximum((QB * d - p0 - res + d - 1) // d, 0)
            mask = band & (kj >= first_key)
        else:
            older = (kc_ref, vc_ref, qstart - QB * d)
            mask = band
        qb = rows(q_ref, qstart, d)
        kb = jnp.concatenate([rows(older[0], older[2], d), rows(kc_ref, qstart, d)], axis=0).astype(BF16)
        vb = jnp.concatenate([rows(older[1], older[2], d), rows(vc_ref, qstart, d)], axis=0).astype(BF16)
        qh = jnp.concatenate([jnp.where(low, qb, 0.0), jnp.where(low, 0.0, qb)], axis=0).astype(BF16)
        s2 = lax.dot_general(qh, kb, (((1,), (1,)), ((), ())), preferred_element_type=F32)
        ms, ls, es = [], [], []
        for h in range(2):
            s = jnp.where(mask, s2[h * QB:(h + 1) * QB], NEG)
            m = jnp.max(s, axis=-1, keepdims=True)
            e = jnp.exp(s - m)
            ms.append(m)
            ls.append(jnp.sum(e, axis=-1, keepdims=True))
            es.append(e.astype(BF16))
        acc2 = jnp.dot(jnp.concatenate(es, axis=0), vb, preferred_element_type=F32)
        stats = [(ms[h], ls[h], acc2[h * QB:(h + 1) * QB]) for h in range(2)]
        m_new = jnp.where(low, stats[0][0], stats[1][0])
        l_new = jnp.where(low, stats[0][1], stats[1][1])
        a_new = jnp.where(low, stats[0][2], stats[1][2])
        if state is not None:
            sl = (pl.ds(qstart, QB, stride=d), slice(None))
            state[0][sl], state[1][sl], state[2][sl] = m_new, l_new, a_new
        else:
            sl = (pl.ds(qstart, QB), slice(None))
            parts = [(m_new, l_new, a_new)] + [(st[0][sl], st[1][sl], st[2][sl]) for st in states]
            m_t = functools.reduce(jnp.maximum, [p[0] for p in parts])
            ws = [jnp.exp(p[0] - m_t) for p in parts]
            l_t = sum(w * p[1] for w, p in zip(ws, parts))
            a_t = sum(w * p[2] for w, p in zip(ws, parts))
            o_ref[sl] = (a_t / l_t).astype(o_ref.dtype)

    nblk = SPAN // QB
    for (_, d), state in zip(reversed(DILATIONS), states + [None]):
        assert (d == 1) == (state is None)
        per_res = nblk // d
        for res in range(d):
            for blk in range(per_res):
                unit(d, res, blk, state)


def _attn_prompt(q3, k3, v3, batch, seq):
    nspan = seq // SPAN
    cur = pl.BlockSpec((1, SPAN, LANE), lambda b, s, p: (p, b * nspan + s, 0))
    prv = pl.BlockSpec((1, SPAN, LANE), lambda b, s, p: (p, b * nspan + jnp.maximum(s - 1, 0), 0))
    return pl.pallas_call(
        _attn_body,
        grid=(batch, nspan, 3),
        in_specs=[cur, cur, prv, cur, prv],
        out_specs=pl.BlockSpec((SPAN, LANE), lambda b, s, p: (b * nspan + s, p)),
        out_shape=jax.ShapeDtypeStruct((batch * seq, WIDTH_B), BF16),
        scratch_shapes=[pltpu.VMEM((SPAN, LANE), F32)] * (3 * (len(DILATIONS) - 1)),
        compiler_params=_cparams(("arbitrary", "arbitrary", "arbitrary")),
        name="attn_prompt",
    )(q3, k3, k3, v3, v3)


def _memkv_body(mem_ref, g_ref, w_ref, gk_ref, bd_ref, k_ref, v_ref):
    x = mem_ref[...]
    ms = jnp.mean(x * x, axis=-1, keepdims=True)
    h = (x * lax.rsqrt(ms + EPS) * g_ref[...]).astype(BF16)
    kv = jnp.dot(h, w_ref[...], preferred_element_type=F32)
    n = x.shape[0]
    kt = [kv[:, LANE * j:LANE * (j + 1)] for j in range(2)]
    sq = jnp.concatenate([(t * t).astype(BF16) for t in kt], axis=0)
    ssum = jnp.dot(sq, bd_ref[...], preferred_element_type=F32)
    for j in range(2):
        inv = lax.rsqrt(ssum[j * n:(j + 1) * n] * (1.0 / HEAD_DIM) + EPS)
        k_ref[:, LANE * j:LANE * (j + 1)] = kt[j] * inv * gk_ref[...]
    v_ref[...] = kv[:, WIDTH_M:]


def _memkv(mem, w):
    n = mem.shape[0]
    return pl.pallas_call(
        _memkv_body,
        out_shape=[jax.ShapeDtypeStruct((n, WIDTH_M), F32)] * 2,
        compiler_params=pltpu.CompilerParams(vmem_limit_bytes=VMEM_LIMIT),
        name="memkv",
    )(mem, w["gmem"], w["w_mem_kv"], w["gkm"], w["bd"])


def _memattn_body(q_ref, k_ref, v_ref, o_ref):
    lane = lax.broadcasted_iota(jnp.int32, (q_ref.shape[0], LANE), 1)
    low = lane < HEAD_DIM
    for j in range(2):
        qp = q_ref[:, LANE * j:LANE * (j + 1)].astype(F32)
        kp = k_ref[0, :, LANE * j:LANE * (j + 1)].astype(BF16)
        vp = v_ref[0, :, LANE * j:LANE * (j + 1)].astype(BF16)
        n = qp.shape[0]
        qh = jnp.concatenate([jnp.where(low, qp, 0.0), jnp.where(low, 0.0, qp)], axis=0).astype(BF16)
        s = lax.dot_general(qh, kp, (((1,), (1,)), ((), ())), preferred_element_type=F32)
        m = jnp.max(s, axis=-1, keepdims=True)
        e = jnp.exp(s - m)
        l = jnp.sum(e, axis=-1, keepdims=True)
        out = jnp.dot(e.astype(BF16), vp, preferred_element_type=F32) / l
        o_ref[:, LANE * j:LANE * (j + 1)] = jnp.where(low, out[:n], out[n:]).astype(o_ref.dtype)


def _sample_attn_body(*refs):
    for i in range(refs[0].shape[0]):
        _sample_attn_one(i, *refs)


def _sample_attn_one(i, qbd_ref, kt_ref, vt_ref, kn_ref, vn_ref, cnt_ref, cntn_ref, hmask_ref,
                     qmbd_ref, kmt_ref, vmt_ref, hmaskm_ref, ob_ref, om_ref):
    dec = kn_ref.shape[1]
    qbd = qbd_ref[i]
    kt = kt_ref[i].astype(BF16)
    vt = vt_ref[i].astype(BF16)
    s = jnp.dot(qbd, kt, preferred_element_type=F32)
    qf = qbd.astype(F32)
    kn = kn_ref[i]
    vn = vn_ref[i]
    cnt = cnt_ref[...]
    cntn = cntn_ref[...]
    s_new = [jnp.sum(qf * kn[j:j + 1, :], axis=-1, keepdims=True) for j in range(dec)]
    m = jnp.max(jnp.where(cnt > 0, s, NEG), axis=-1, keepdims=True)
    for j in range(dec):
        m = jnp.maximum(m, jnp.where(cntn[:, j:j + 1] > 0, s_new[j], NEG))
    e = cnt * jnp.exp(jnp.where(cnt > 0, s - m, 0.0))
    l = jnp.sum(e, axis=-1, keepdims=True)
    acc = lax.dot_general(e.astype(BF16), vt, (((1,), (1,)), ((), ())), preferred_element_type=F32)
    for j in range(dec):
        w = cntn[:, j:j + 1]
        ej = w * jnp.exp(jnp.where(w > 0, s_new[j] - m, 0.0))
        l = l + ej
        acc = acc + ej * vn[j:j + 1, :]
    r = acc / l * hmask_ref[...]
    out = r[0:8]
    for h in range(1, WIDTH_B // HEAD_DIM):
        out = out + r[8 * h:8 * h + 8]
    ob_ref[i] = out
    qm = qmbd_ref[i]
    sm = jnp.dot(qm, kmt_ref[i].astype(BF16), preferred_element_type=F32)
    mm = jnp.max(sm, axis=-1, keepdims=True)
    em = jnp.exp(sm - mm)
    lm = jnp.sum(em, axis=-1, keepdims=True)
    am = lax.dot_general(em.astype(BF16), vmt_ref[i].astype(BF16), (((1,), (1,)), ((), ())),
                         preferred_element_type=F32)
    rm = am / lm * hmaskm_ref[...]
    outm = rm[0:8]
    for h in range(1, WIDTH_M // HEAD_DIM):
        outm = outm + rm[8 * h:8 * h + 8]
    om_ref[i] = outm


def _sample_counts(dec, w_buf):
    t = np.arange(8)[:, None]
    t = np.where(t < dec, t, 0)
    def mult(dist):
        c = np.zeros(dist.shape, np.float32)
        for window, dil in DILATIONS:
            c += ((dist >= 0) & (dist % dil == 0) & (dist <= window)).astype(np.float32)
        return c
    cache = mult(w_buf + t - np.arange(w_buf)[None, :])
    new = mult(t - np.arange(dec)[None, :])
    nb, nm = WIDTH_B // HEAD_DIM, WIDTH_M // HEAD_DIM
    hmask = (np.arange(8 * nb)[:, None] // 8 == np.arange(WIDTH_B)[None, :] // HEAD_DIM).astype(np.float32)
    hmaskm = (np.arange(8 * nm)[:, None] // 8 == np.arange(WIDTH_M)[None, :] // HEAD_DIM).astype(np.float32)
    return np.tile(cache, (nb, 1)), np.tile(new, (nb, 1)), hmask, hmaskm


def _block_diag_queries(q, dec, hmask):
    width = q.shape[-1]
    nh = width // HEAD_DIM
    qb = q.reshape(-1, 1, dec, width)
    qb = jnp.pad(qb, ((0, 0), (0, 0), (0, 8 - dec), (0, 0)))
    qb = jnp.broadcast_to(qb, (qb.shape[0], nh, 8, width)).reshape(-1, 8 * nh, width)
    return (qb * hmask[None]).astype(BF16)


def _sample_attn_parts(q, kn, vn, qm, kt, vt, kmt, vmt, dec):
    bd = kt.shape[0]
    w_buf = kt.shape[-1]
    cnt, cntn, hmask, hmaskm = _sample_counts(dec, w_buf)
    qbd = _block_diag_queries(q, dec, hmask)
    qmbd = _block_diag_queries(qm.astype(F32), dec, hmaskm)
    nb8, nm8 = qbd.shape[1], qmbd.shape[1]
    per_b = lambda shape: pl.BlockSpec((SAMPLE_SEQS,) + shape, lambda b: (b,) + (0,) * len(shape))
    full = lambda shape: pl.BlockSpec(shape, lambda b: (0,) * len(shape))
    return dict(
        steps=bd // SAMPLE_SEQS,
        in_specs=[per_b((nb8, WIDTH_B)), per_b((WIDTH_B, w_buf)), per_b((WIDTH_B, w_buf)),
                  per_b((dec, WIDTH_B)), per_b((dec, WIDTH_B)),
                  full((nb8, w_buf)), full((nb8, dec)), full((nb8, WIDTH_B)),
                  per_b((nm8, WIDTH_M)), per_b((WIDTH_M, N_MEM)), per_b((WIDTH_M, N_MEM)), full((nm8, WIDTH_M))],
        args=[qbd, kt, vt, kn.reshape(bd, dec, WIDTH_B), vn.reshape(bd, dec, WIDTH_B),
              jnp.asarray(cnt), jnp.asarray(cntn), jnp.asarray(hmask), qmbd, kmt, vmt, jnp.asarray(hmaskm)],
        out_specs=[per_b((8, WIDTH_B)), per_b((8, WIDTH_M))],
        out_shape=[jax.ShapeDtypeStruct((bd, 8, WIDTH_B), F32), jax.ShapeDtypeStruct((bd, 8, WIDTH_M), F32)])


def _sample_attn_rows(ob, om, dec):
    return (ob[:, :dec].reshape(-1, WIDTH_B).astype(BF16), om[:, :dec].reshape(-1, WIDTH_M).astype(BF16))


def _post_body(*refs, tc, aliased):
    (x_ref, u_ref, va_ref, ob_ref, om_ref, wg_ref, bg_ref, wout_ref, g2_ref, wr_ref, br_ref,
     tri_ref, upper_ref) = refs[:13]
    x1_ref, xs_ref, ld_ref, tg_ref, seg_ref = refs[13 + aliased:]
    tm = x_ref.shape[0]
    u = u_ref[...].astype(F32)
    if tc is None:
        vaf = va_ref[...]
        mixed = wg_ref[0] * vaf + bg_ref[...]
        for s in range(1, wg_ref.shape[0]):
            mixed = mixed + wg_ref[s] * pltpu.roll(vaf, s, 0)
        oa = u * mixed
    else:
        lane = lax.broadcasted_iota(jnp.int32, (tc, LANE), 1)
        low = lane < HEAD_DIM
        va = va_ref[...].astype(BF16)
        oa_rows = []
        for c in range(tm // tc):
            r0 = c * tc
            tiles = []
            for p in range(3):
                vp = va[r0:r0 + tc, LANE * p:LANE * (p + 1)]
                r = jnp.dot(wg_ref[p], vp, preferred_element_type=F32)
                tiles.append(jnp.where(low, r[:tc], r[tc:]))
            mixed = jnp.concatenate(tiles, axis=1) + bg_ref[...]
            oa_rows.append(u[r0:r0 + tc] * mixed)
        oa = jnp.concatenate(oa_rows, axis=0)
    mixed_all = jnp.concatenate([oa.astype(BF16), ob_ref[...], om_ref[...]], axis=1)
    x1 = x_ref[...] + jnp.dot(mixed_all, wout_ref[...], preferred_element_type=F32)
    x1_ref[...] = x1
    ms = jnp.mean(x1 * x1, axis=-1, keepdims=True)
    h2 = x1 * lax.rsqrt(ms + EPS) * g2_ref[...]
    h_hi = h2.astype(BF16)
    h_lo = (h2 - h_hi.astype(F32)).astype(BF16)
    hw = jnp.dot(h_hi, wr_ref[...], preferred_element_type=F32)
    logits = (hw[:, :LANE] + hw[:, LANE:]
              + jnp.dot(h_lo, wr_ref[:, :LANE], preferred_element_type=F32)) + br_ref[...]
    lane_i = lax.broadcasted_iota(jnp.int32, (tm, LANE), 1)
    lane_r = lane_i.astype(F32)
    vals = logits
    tops, idxs = [], []
    for _ in range(TOP_K):
        mk = jnp.max(vals, axis=-1, keepdims=True)
        ik = jnp.min(jnp.where(vals == mk, lane_r, float(LANE)), axis=-1, keepdims=True)
        vals = jnp.where(lane_r == ik, -jnp.inf, vals)
        tops.append(mk)
        idxs.append(ik)
    es = [jnp.exp(t - tops[0]) for t in tops]
    den = es[0] + es[1] + es[2] + es[3]
    tg = jnp.zeros((tm, LANE), F32)
    for k in range(TOP_K):
        tg = jnp.where(lane_i == k, es[k] / den, tg)
    tg_ref[...] = tg
    col = lax.broadcasted_iota(jnp.int32, (MOE_TM, SEG_ROWS), 1).astype(F32)
    lane_t = lax.broadcasted_iota(jnp.int32, (MOE_TM, LANE), 1)
    hot_all = [lane_r == idxs[k] for k in range(TOP_K)]
    for hf in range(tm // MOE_TM):
        r0 = hf * MOE_TM
        hot = [h[r0:r0 + MOE_TM] for h in hot_all]
        sel = jnp.zeros((MOE_TM, LANE), F32)
        for k in range(TOP_K):
            sel = sel + jnp.where(hot[k], 1.0, 0.0)
        rank = jnp.dot(tri_ref[...], sel.astype(BF16), preferred_element_type=F32)
        length = jnp.sum(sel, axis=0, keepdims=True)
        plen = jnp.floor((length + (PIECE - 1)) * (1.0 / PIECE)) * PIECE
        loff = jnp.dot(jnp.broadcast_to(plen, (8, LANE)).astype(BF16), upper_ref[...],
                       preferred_element_type=F32)[0:1]
        base = loff + rank
        hit = None
        ld = jnp.zeros((MOE_TM, LANE), F32)
        for k in range(TOP_K):
            ld_k = jnp.sum(jnp.where(hot[k], base, 0.0), axis=-1, keepdims=True)
            hit = (col == ld_k) if hit is None else hit | (col == ld_k)
            ld = jnp.where(lane_t == k, ld_k, ld)
        q = jnp.where(hit, 1.0, 0.0)
        xs_ref[hf * SEG_ROWS:(hf + 1) * SEG_ROWS, :] = lax.dot_general(
            q.astype(BF16), h_hi[r0:r0 + MOE_TM], (((0,), (0,)), ((), ())),
            preferred_element_type=F32).astype(xs_ref.dtype)
        ld_ref[r0:r0 + MOE_TM, :] = ld.astype(jnp.int32)
        seg_ref[hf * 8:(hf + 1) * 8, :] = jnp.broadcast_to(length, (8, LANE)).astype(jnp.int32)


def _post(x, u, va, ob, om, wg, bg, w, tc, total_tiles, tile0=0, xs_all=None):
    t = x.shape[0]
    nt = t // TM
    full = lambda shape: pl.BlockSpec(shape, lambda i: (0,) * len(shape))
    row = lambda width: pl.BlockSpec((TM, width), lambda i: (i, 0))
    per = TM // MOE_TM
    ix = np.arange(MOE_TM)
    tri = jnp.asarray(ix[:, None] > ix[None, :], BF16)
    ex = np.arange(LANE)
    upper = jnp.asarray(ex[:, None] < ex[None, :], BF16)
    step0 = tile0 // per
    in_specs = [row(D_MODEL), row(WIDTH_A), row(WIDTH_A), row(WIDTH_B), row(WIDTH_M),
                full(wg.shape), full(bg.shape), full((D_MODEL, D_MODEL)), full((1, D_MODEL)),
                full((D_MODEL, 2 * LANE)), full((1, LANE)), full((MOE_TM, MOE_TM)), full((LANE, LANE))]
    args = [x, u, va, ob, om, wg, bg, w["w_out"], w["g2"], w["w_router"], w["b_router"], tri, upper]
    aliases = {}
    if xs_all is not None:
        in_specs.append(pl.BlockSpec(memory_space=pl.ANY))
        args.append(xs_all)
        aliases = {len(args) - 1: 1}
    return pl.pallas_call(
        functools.partial(_post_body, tc=tc, aliased=int(xs_all is not None)),
        grid=(nt,),
        in_specs=in_specs,
        out_specs=[row(D_MODEL), pl.BlockSpec((per * SEG_ROWS, D_MODEL), lambda i: (i + step0, 0)),
                   row(LANE), row(LANE), pl.BlockSpec((per * 8, LANE), lambda i: (i, 0))],
        out_shape=[jax.ShapeDtypeStruct((t, D_MODEL), F32),
                   jax.ShapeDtypeStruct((total_tiles * SEG_ROWS, D_MODEL), BF16),
                   jax.ShapeDtypeStruct((t, LANE), jnp.int32), jax.ShapeDtypeStruct((t, LANE), F32),
                   jax.ShapeDtypeStruct((t // MOE_TM * 8, LANE), jnp.int32)],
        input_output_aliases=aliases,
        compiler_params=_cparams(("arbitrary",)),
        name="post",
    )(*args)


def _experts_body(te_ref, valid_ref, first_ref, next_ref, half_ref, src_ref, dst_ref,
                  xs_hbm, wgu_hbm, bgu_ref, wd_hbm, bd_ref, ys_hbm,
                  xbuf, ybuf, wgu_f, wd_f, wgu_s, wd_s, gsem, ssem, wsem):
    t = pl.program_id(0)
    nt = pl.num_programs(0)
    slot = t % 2

    def gather(tile, sl):
        for i in range(NPIECE):
            s = src_ref[tile * NPIECE + i]
            pltpu.make_async_copy(xs_hbm.at[pl.ds(pl.multiple_of(s * PIECE, PIECE), PIECE), :],
                                  xbuf.at[sl, pl.ds(i * PIECE, PIECE), :], gsem.at[sl]).start()

    def scatter(tile, sl):
        for i in range(NPIECE):
            d = dst_ref[tile * NPIECE + i]
            pltpu.make_async_copy(ybuf.at[sl, pl.ds(i * PIECE, PIECE), :],
                                  ys_hbm.at[pl.ds(pl.multiple_of(d * PIECE, PIECE), PIECE), :], ssem.at[sl]).start()

    def wait_tile(hbm, buf, sem, sl):
        pltpu.make_async_copy(hbm.at[pl.ds(0, GM_TM), :], buf.at[sl], sem.at[sl]).wait()

    def weight_copies(e):
        return (pltpu.make_async_copy(wgu_hbm.at[e], wgu_f, wsem.at[0]),
                pltpu.make_async_copy(wd_hbm.at[e], wd_f, wsem.at[1]))

    @pl.when(t == 0)
    def _():
        for c in weight_copies(te_ref[0]):
            c.start()
        gather(0, 0)
        ybuf[...] = jnp.zeros(ybuf.shape, ybuf.dtype)

    @pl.when(valid_ref[t] > 0)
    def _():
        nxt = jnp.minimum(t + 1, nt - 1)
        has_next = jnp.logical_and(t + 1 < nt, valid_ref[nxt] > 0)

        @pl.when(first_ref[t] > 0)
        def _():
            for c in weight_copies(te_ref[t]):
                c.wait()
            wgu_s[...] = wgu_f[...].astype(BF16)
            wd_s[...] = wd_f[...].astype(BF16)

            @pl.when(next_ref[t] >= 0)
            def _():
                for c in weight_copies(next_ref[t]):
                    c.start()

        @pl.when(t >= 2)
        def _():
            wait_tile(ys_hbm, ybuf, ssem, slot)

        gather(jnp.where(has_next, t + 1, t), 1 - slot)
        wait_tile(xs_hbm, xbuf, gsem, slot)

        def ffn(rows):
            x = xbuf[slot, pl.ds(0, rows), :]
            gu = jnp.dot(x, wgu_s[...], preferred_element_type=F32) + bgu_ref[0]
            gate = jnp.minimum(gu[:, :D_MODEL], SWIGLU_LIMIT)
            up = jnp.clip(gu[:, D_MODEL:], -SWIGLU_LIMIT, SWIGLU_LIMIT)
            act = (up + 1.0) * (gate * (1.0 / (1.0 + jnp.exp(-SWIGLU_ALPHA * gate))))
            y = jnp.dot(act.astype(BF16), wd_s[...], preferred_element_type=F32) + bd_ref[0]
            ybuf[slot, pl.ds(0, rows), :] = y.astype(ybuf.dtype)

        @pl.when(half_ref[t] > 0)
        def _():
            ffn(GM_TM // 2)

        @pl.when(half_ref[t] == 0)
        def _():
            ffn(GM_TM)

        scatter(t, slot)

        @pl.when(jnp.logical_not(has_next))
        def _():
            wait_tile(xs_hbm, xbuf, gsem, 1 - slot)
            wait_tile(ys_hbm, ybuf, ssem, slot)

            @pl.when(t >= 1)
            def _():
                wait_tile(ys_hbm, ybuf, ssem, 1 - slot)


def _experts(plan, xs, w, ys_rows):
    n_tiles = plan["tile_expert"].shape[0]
    by_expert = lambda shape: pl.BlockSpec((1,) + shape, lambda t, te, *_: (te[t],) + (0,) * len(shape))
    hbm = pl.BlockSpec(memory_space=pl.ANY)
    return pl.pallas_call(
        _experts_body,
        grid_spec=pltpu.PrefetchScalarGridSpec(
            num_scalar_prefetch=7, grid=(n_tiles,),
            in_specs=[hbm, hbm, by_expert((1, 2 * D_MODEL)), hbm, by_expert((1, D_MODEL))],
            out_specs=hbm,
            scratch_shapes=[pltpu.VMEM((2, GM_TM, D_MODEL), BF16), pltpu.VMEM((2, GM_TM, D_MODEL), BF16),
                            pltpu.VMEM((D_MODEL, 2 * D_MODEL), F32), pltpu.VMEM((D_MODEL, D_MODEL), F32),
                            pltpu.VMEM((D_MODEL, 2 * D_MODEL), BF16), pltpu.VMEM((D_MODEL, D_MODEL), BF16),
                            pltpu.SemaphoreType.DMA((2,)), pltpu.SemaphoreType.DMA((2,)),
                            pltpu.SemaphoreType.DMA((2,))]),
        out_shape=jax.ShapeDtypeStruct((ys_rows, D_MODEL), BF16),
        compiler_params=_cparams(("arbitrary",)),
        name="moe_experts",
    )(plan["tile_expert"], plan["tile_valid"], plan["tile_first"], plan["tile_next"], plan["tile_half"],
      plan["src"], plan["dst"],
      xs, w["w_gate_up"], w["b_gate_up"], w["w_down"], w["b_down"])


def _combine_body(used_ref, ld_ref, g_ref, x1_ref, ys_ref, y_ref, *, tile0):
    t = pl.program_id(0)
    row = lax.broadcasted_iota(jnp.int32, (SEG_ROWS, 1), 0)
    col = lax.broadcasted_iota(jnp.int32, (MOE_TM, SEG_ROWS), 1)
    for i in range(COMBINE_TILES):
        tok = pl.ds(i * MOE_TM, MOE_TM)
        used = used_ref[tile0 + t * COMBINE_TILES + i]
        ys = jnp.where(row < used, ys_ref[pl.ds(i * SEG_ROWS, SEG_ROWS), :].astype(F32), 0.0).astype(BF16)
        ld = ld_ref[tok, :]
        g = g_ref[tok, :]
        p = jnp.zeros((MOE_TM, SEG_ROWS), F32)
        for k in range(TOP_K):
            p = jnp.where(col == ld[:, k:k + 1], g[:, k:k + 1], p)
        y_ref[tok, :] = x1_ref[tok, :] + jnp.dot(p.astype(BF16), ys, preferred_element_type=F32)


def _combine(used, ld, gates, x1, ys, tile0):
    n = x1.shape[0]
    step_rows = COMBINE_TILES * MOE_TM
    nt = n // step_rows
    assert n % step_rows == 0 and tile0 % COMBINE_TILES == 0
    step0 = tile0 // COMBINE_TILES
    rows = lambda width: pl.BlockSpec((step_rows, width), lambda t, *_: (t, 0))
    return pl.pallas_call(
        functools.partial(_combine_body, tile0=tile0),
        grid_spec=pltpu.PrefetchScalarGridSpec(
            num_scalar_prefetch=1, grid=(nt,),
            in_specs=[rows(LANE), rows(LANE), rows(D_MODEL),
                      pl.BlockSpec((COMBINE_TILES * SEG_ROWS, D_MODEL), lambda t, *_: (t + step0, 0))],
            out_specs=rows(D_MODEL)),
        out_shape=jax.ShapeDtypeStruct((n, D_MODEL), F32),
        compiler_params=_cparams(("arbitrary",)),
        name="moe_combine",
    )(used, ld, gates, x1, ys)


def _moe(groups, xs, seglen, w):
    i32 = jnp.int32
    nt = seglen.shape[0]
    plen = (seglen + PIECE - 1) // PIECE * PIECE
    loff = jnp.cumsum(plen, axis=1) - plen
    used = jnp.sum(plen, axis=1).astype(i32)
    pp = plen // PIECE
    cp_end = jnp.cumsum(pp, axis=0)
    cp = cp_end - pp
    cnt_e = cp_end[-1]
    tiles_e = (cnt_e + NPIECE - 1) // NPIECE
    tile_end = jnp.cumsum(tiles_e)
    tile_start = tile_end - tiles_e
    n_tiles = (nt * SEG_ROWS + GM_TM - 1) // GM_TM + N_EXPERTS
    tix = jnp.arange(n_tiles, dtype=i32)
    total_tiles = tile_end[-1]
    tile_valid = (tix < total_tiles).astype(i32)
    expert_at = lambda tile: jnp.minimum((tile[:, None] >= tile_end[None, :]).astype(i32).sum(axis=1), N_EXPERTS - 1)
    last_expert = expert_at(jnp.maximum(total_tiles - 1, 0)[None])[0]
    tile_expert = jnp.where(tile_valid > 0, expert_at(tix), last_expert)
    hot_e = tile_expert[:, None] == jnp.arange(N_EXPERTS, dtype=i32)[None, :]
    per_tile = lambda v: jnp.sum(jnp.where(hot_e, v[None, :], 0), axis=1)
    per_tile_rows = lambda m: jnp.sum(jnp.where(hot_e[:, :, None], jnp.transpose(m)[None], 0), axis=1)
    start_t = per_tile(tile_start)
    tile_first = ((tix == start_t) & (tile_valid > 0)).astype(i32)
    following = per_tile(tile_end)
    tile_next = jnp.where(following < total_tiles, expert_at(following), -1)
    j = (tix - start_t)[:, None] * NPIECE + jnp.arange(NPIECE, dtype=i32)[None, :]
    cnt_t = per_tile(cnt_e)
    ok = (tile_valid[:, None] > 0) & (j < cnt_t[:, None])
    tile_half = ((tile_valid > 0) & (cnt_t - (tix - start_t) * NPIECE <= NPIECE // 2)).astype(i32)
    ends_t, cp_t, loff_t = per_tile_rows(cp_end), per_tile_rows(cp), per_tile_rows(loff)
    t_q = jnp.minimum((ends_t[:, None, :] <= j[:, :, None]).astype(i32).sum(axis=2), nt - 1)
    hot_t = t_q[:, :, None] == jnp.arange(nt, dtype=i32)[None, None, :]
    at_t = lambda m: jnp.sum(jnp.where(hot_t, m[:, None, :], 0), axis=2)
    piece = t_q * SEG_PIECES + at_t(loff_t) // PIECE + j - at_t(cp_t)
    src = jnp.where(ok, piece, 0)
    dump = nt * SEG_PIECES + (tix % 2)[:, None] * NPIECE + jnp.arange(NPIECE, dtype=i32)[None, :]
    dst = jnp.where(ok, piece, dump)
    plan = dict(tile_expert=tile_expert.astype(i32), tile_valid=tile_valid, tile_first=tile_first,
                tile_next=tile_next.astype(i32), tile_half=tile_half,
                src=src.reshape(-1).astype(i32), dst=dst.reshape(-1).astype(i32))

    ys = _experts(plan, xs, w, nt * SEG_ROWS + 2 * GM_TM)
    outs, r0 = [], 0
    for x1, ld, tg in groups:
        outs.append(_combine(used, ld, tg, x1, ys, r0 // MOE_TM))
        r0 += x1.shape[0]
    return outs


def _pair_major_to_rows(a3):
    return jnp.transpose(a3, (1, 0, 2)).reshape(a3.shape[1], 3 * LANE)


def kernel(x_prompt, x_sample, mem_prompt, cache_win_k, cache_win_v, cache_mem_k, cache_mem_v, norm1_g, w_in, gv_a, w_s, b_s, gq_b, gk_b, gq_m, gk_m, mem_norm_g, w_mem_kv, w_out, norm2_g, w_router, b_router, w_gate_up, b_gate_up, w_down, b_down):
    batch, seq, _ = x_prompt.shape
    bd, dec, _ = x_sample.shape
    depth = norm1_g.shape[0]
    assert depth == 1 and seq % SPAN == 0 and (bd * dec) % TM == 0 and PAST_LEN % CHUNK == 0
    w_buf = cache_win_k.shape[2]
    assert w_buf == MAX_WINDOW and dec <= 8
    l = 0
    two = lambda g: jnp.concatenate([g, g])[None, :]
    head = np.arange(LANE) // HEAD_DIM
    head2 = np.arange(2 * LANE) // HEAD_DIM
    wr = jnp.pad(w_router[l], ((0, 0), (0, LANE - N_EXPERTS)))
    wr_hi = wr.astype(BF16)
    wr_lo = (wr - wr_hi.astype(F32)).astype(BF16)
    w = dict(
        g1=norm1_g[l][None], w_in=w_in[l].astype(BF16), gva=gv_a[l][None],
        gq=two(gq_b[l]), gk=two(gk_b[l]), gqm=two(gq_m[l]), gkm=two(gk_m[l]),
        bd=jnp.asarray(head[:, None] == head[None, :], BF16),
        bd2=jnp.asarray(head2[:, None] == head2[None, :], BF16),
        gmem=mem_norm_g[l][None], w_mem_kv=w_mem_kv[l].astype(BF16),
        w_out=w_out[l].astype(BF16), g2=norm2_g[l][None],
        w_router=jnp.concatenate([wr_hi, wr_lo], axis=1),
        b_router=jnp.pad(b_router[l], (0, LANE - N_EXPERTS), constant_values=-jnp.inf)[None],
        w_gate_up=w_gate_up[l], b_gate_up=b_gate_up[l][:, None, :], w_down=w_down[l], b_down=b_down[l][:, None, :],
    )
    ngrp = WIDTH_A // HEAD_DIM
    wtri = jnp.where(jnp.tril(jnp.ones((CHUNK, CHUNK), bool)), w_s[l], 0).astype(BF16)
    wg_p = wtri.reshape(ngrp // 2, 2 * CHUNK, CHUNK)
    bg_p = jnp.repeat(jnp.transpose(b_s[l]), HEAD_DIM, axis=1)
    zero = jnp.zeros((ngrp,), F32)
    lanes = lambda tg_: jnp.tile(jnp.repeat(tg_, HEAD_DIM, axis=1), (bd, 1))
    wg_s = jnp.stack([lanes(jnp.stack([w_s[l][:, t, t - s] if t >= s else zero for t in range(dec)]))
                      for s in range(dec)])
    bg_s = lanes(jnp.transpose(b_s[l][:, :dec]))

    xs = x_sample.reshape(bd * dec, D_MODEL)
    tabs_s = _rope_tables(np.tile(PAST_LEN + np.arange(dec), bd))
    u_s, va_s, q3_s, k3_s, v3_s, qm_s = _premix(xs, tabs_s, bd * dec, w)
    q_s, k_s, v_s = (_pair_major_to_rows(a) for a in (q3_s, k3_s, v3_s))
    nb = WIDTH_B // HEAD_DIM
    nm = WIDTH_M // HEAD_DIM
    kt = jnp.transpose(cache_win_k[l], (0, 2, 3, 1)).reshape(bd, WIDTH_B, w_buf)
    vt = jnp.transpose(cache_win_v[l], (0, 2, 3, 1)).reshape(bd, WIDTH_B, w_buf)
    kmt = jnp.transpose(cache_mem_k[l], (0, 2, 3, 1)).reshape(bd, WIDTH_M, N_MEM)
    vmt = jnp.transpose(cache_mem_v[l], (0, 2, 3, 1)).reshape(bd, WIDTH_M, N_MEM)
    sample_parts = _sample_attn_parts(q_s, k_s, v_s, qm_s, kt, vt, kmt, vmt, dec)

    xp = x_prompt.reshape(batch * seq, D_MODEL)
    tabs_p = _rope_tables(np.arange(seq))
    premix_tm = batch * seq // sample_parts["steps"]
    assert seq % premix_tm == 0 and premix_tm % 8 == 0
    km, vm = _memkv(mem_prompt.reshape(batch * N_MEM, D_MODEL), w)
    (u_p, va_p, q3_p, k3_p, v3_p, _), (ob_s8, om_s8), om_p = _premix_with_sample_attn(
        _premix_parts(xp, tabs_p, seq, w, premix_tm, gate_dtype=BF16), sample_parts,
        km.reshape(batch, N_MEM, WIDTH_M), vm.reshape(batch, N_MEM, WIDTH_M))
    ob_s, om_s = _sample_attn_rows(ob_s8, om_s8, dec)
    ob_p = _attn_prompt(q3_p, k3_p, v3_p, batch, seq)
    tiles_p = batch * seq // MOE_TM
    tiles_all = tiles_p + bd * dec // MOE_TM
    x1_p, xs_all, ld_p, tg_p, seg_p = _post(xp, u_p, va_p, ob_p, om_p, wg_p, bg_p, w, CHUNK, tiles_all)

    x1_s, xs_all, ld_s, tg_s, seg_s = _post(xs, u_s, va_s, ob_s, om_s, wg_s, bg_s, w, None, tiles_all,
                                             tile0=tiles_p, xs_all=xs_all)

    seglen = jnp.concatenate([seg_p, seg_s])[::8, :N_EXPERTS]
    y_p, y_s = _moe([(x1_p, ld_p, tg_p), (x1_s, ld_s, tg_s)], xs_all, seglen, w)
    y_prompt = y_p.reshape(batch, seq, D_MODEL)
    y_sample = y_s.reshape(bd, dec, D_MODEL)

    n_keep = min(MAX_WINDOW, seq)

    def window_rows(a3):
        a = a3.reshape(3, batch, seq, LANE)[:, :, seq - n_keep:]
        return jnp.transpose(a, (1, 2, 0, 3)).reshape(1, batch, n_keep, nb, HEAD_DIM)

    return (y_prompt, y_sample,
            window_rows(k3_p), window_rows(v3_p),
            km.reshape(1, batch, N_MEM, nm, HEAD_DIM), vm.reshape(1, batch, N_MEM, nm, HEAD_DIM),
            k_s.reshape(1, bd, dec, nb, HEAD_DIM), v_s.reshape(1, bd, dec, nb, HEAD_DIM),
            va_s.reshape(1, bd, dec, WIDTH_A))
```
